```python
import math
import jax, jax.numpy as jnp
from jax import lax
import numpy as np

D_MODEL = 2048
BATCH = 4
SEQ = 4096
DEPTH = 2

CTX_LEN = 256
GRID_W = 64
F32 = jnp.float32
EPS = 1e-6
D_MIX = D_MODEL
N_MIXERS = 4
GROUP_W = D_MIX // N_MIXERS
D_FF = 4 * D_MODEL
N_MOD = 6
MLA_NOPE = 128
MLA_ROPE = 64
MLA_QK = MLA_NOPE + MLA_ROPE
MLA_V = 128
MLA_HEADS = GROUP_W // MLA_V
Q_LORA = GROUP_W
KV_LORA = GROUP_W // 2
MLA_SCALE = 1.0 / math.sqrt(MLA_QK)
ROPE_BASE = 10000.0
Q_BLOCK = 128
S5_CH = GROUP_W
S5_GROUP = 16
S5_NGROUPS = S5_CH // S5_GROUP
S5_STATE = 64
HG_DK = 128
HG_DV = 128
HG_HEADS = GROUP_W // HG_DV
HG_CHUNK = 64
HY_CH = GROUP_W
HY_ORDER = 2
HY_EMB = 33
HY_BANDS = (HY_EMB - 1) // 2
HY_HIDDEN = 64
HY_TARGET = 1e-2
HY_FAST = 0.3
HY_SLOW = 1.5
HY_SHIFT = 0.05
SHORT_CONV = 3

COL_LAYOUT = (('ckv', KV_LORA), ('krope', MLA_ROPE), ('s5u', S5_CH), ('hg_f_fwd', GROUP_W),
              ('hg_f_bwd', GROUP_W), ('hg_i', GROUP_W), ('cq', Q_LORA), ('hg_q', GROUP_W),
              ('hg_g', GROUP_W), ('hy', 3 * HY_CH))
N_STATE_COLS = KV_LORA + MLA_ROPE + S5_CH + 3 * GROUP_W
N_IN_COLS = N_STATE_COLS + Q_LORA + 2 * GROUP_W + 3 * HY_CH

kernel_name = 'hybrid_mla_s5_hgrn2_hyena_dit'


def rmsnorm(x, g):
    xf = x.astype(F32)
    y = xf * lax.rsqrt(jnp.mean(xf * xf, axis=-1, keepdims=True) + EPS)
    return (y * g.astype(F32)).astype(x.dtype)


def modulate(x, g, shift, scale):
    return rmsnorm(x, g) * (1.0 + scale) + shift


def split_cols(proj):
    out, off = {}, 0
    for name, width in COL_LAYOUT:
        if off + width > proj.shape[-1]:
            break
        out[name] = proj[..., off:off + width]
        off += width
    return out


def to_heads(t, d):
    return t.astype(F32).reshape(t.shape[0], t.shape[1], -1, d)


def rope_2d_tables(n_rows):
    row = jnp.repeat(jnp.arange(n_rows, dtype=F32), GRID_W)
    col = jnp.tile(jnp.arange(GRID_W, dtype=F32), n_rows)
    n_freq = MLA_ROPE // 4
    inv_freq = ROPE_BASE ** (-jnp.arange(n_freq, dtype=F32) / n_freq)
    ang_r = row[:, None] * inv_freq
    ang_c = col[:, None] * inv_freq
    ang = jnp.concatenate([ang_r, ang_r, ang_c, ang_c], axis=-1)
    return jnp.cos(ang), jnp.sin(ang)


def apply_rope(t, cos, sin):
    nope, pe = t[..., :MLA_NOPE], t[..., MLA_NOPE:]
    r1, r2, c1, c2 = jnp.split(pe, 4, axis=-1)
    rot = jnp.concatenate([-r2, r1, -c2, c1], axis=-1)
    return jnp.concatenate([nope, pe * cos + rot * sin], axis=-1)


def mla_kv(ckv, krope, kv_norm_g, w_ukv, k_norm_g, rope):
    b, n, _ = ckv.shape
    kv = (rmsnorm(ckv, kv_norm_g) @ w_ukv).reshape(b, n, MLA_HEADS, MLA_NOPE + MLA_V)
    k_pe = jnp.broadcast_to(krope[:, :, None, :], (b, n, MLA_HEADS, MLA_ROPE))
    k = jnp.concatenate([kv[..., :MLA_NOPE], k_pe], axis=-1)
    k = rmsnorm(k, k_norm_g).astype(F32).transpose(0, 2, 1, 3)
    if rope is not None:
        k = apply_rope(k, *rope)
    return k, kv[..., MLA_NOPE:].astype(F32).transpose(0, 2, 1, 3)


def mla_q(cq, q_a_norm_g, w_uq, q_norm_g, rope):
    b, n, _ = cq.shape
    q = (rmsnorm(cq, q_a_norm_g) @ w_uq).reshape(b, n, MLA_HEADS, MLA_QK)
    q = rmsnorm(q, q_norm_g).astype(F32).transpose(0, 2, 1, 3)
    if rope is not None:
        q = apply_rope(q, *rope)
    return q


def attend(q, k, v):
    s = jnp.einsum('bhqd,bhkd->bhqk', q, k) * MLA_SCALE
    return jnp.einsum('bhqk,bhkd->bhqd', jax.nn.softmax(s, axis=-1), v)


def merge_heads(o):
    b, h, n, d = o.shape
    return o.transpose(0, 2, 1, 3).reshape(b, n, h * d)


def block_attention(q, k, v):
    b, h, n, d = q.shape
    nb = n // Q_BLOCK
    qb = q.reshape(b, h, nb, Q_BLOCK, d).transpose(2, 0, 1, 3, 4)
    o = lax.map(lambda qi: attend(qi, k, v), qb)
    return o.transpose(1, 0, 3, 2, 4).reshape(b, n, h * MLA_V)


def s5_groups(t):
    return t.astype(F32).reshape(t.shape[0], t.shape[1], S5_NGROUPS, S5_GROUP)


def s5_discretize(lam_re, lam_im, log_step, b_re, b_im):
    lam = lax.complex(lam_re.astype(F32), lam_im.astype(F32))
    step = jnp.exp(log_step.astype(F32))[:, None]
    lam_bar = jnp.exp(lam * step)
    b_bar = ((lam_bar - 1.0) / lam)[..., None] * lax.complex(b_re.astype(F32), b_im.astype(F32))
    return lam_bar, b_bar


def _affine_combine(e1, e2):
    a1, b1 = e1
    a2, b2 = e2
    return a1 * a2, a2 * b1 + b2


def s5_states(u, lam_bar, b_bar, h0, reverse):
    bu = jnp.einsum('gpn,blgn->blgp', b_bar, u.astype(jnp.complex64))
    if reverse:
        bu = jnp.flip(bu, 1)
    a = jnp.broadcast_to(lam_bar, bu.shape)
    a_cum, h = lax.associative_scan(_affine_combine, (a, bu), axis=1)
    if h0 is not None:
        h = h + a_cum * h0[:, None]
    final = h[:, -1]
    if reverse:
        h = jnp.flip(h, 1)
    return h, final


def s5_readout(u, h_f, h_b, c_f, c_b, d, w_glu, b_glu):
    b, n = u.shape[:2]
    y = jnp.real(jnp.einsum('gnp,blgp->blgn', c_f, h_f)) + jnp.real(jnp.einsum('gnp,blgp->blgn', c_b, h_b))
    y = y.reshape(b, n, S5_CH) + d.astype(F32) * u.reshape(b, n, S5_CH)
    y = jax.nn.gelu(y)
    return y * jax.nn.sigmoid(y @ w_glu.astype(F32) + b_glu.astype(F32))


def hg_forget(z, lb):
    g = jnp.logaddexp(jnp.log(lb), jnp.log1p(-lb) + jax.nn.log_sigmoid(z.astype(F32)))
    g = g.reshape(z.shape[0], z.shape[1], HG_HEADS, HG_DK)
    return g, -jnp.expm1(g)


def hgrn2_chunked(q, k, v, g, s0):
    b, n, h, _ = q.shape
    nc = n // HG_CHUNK

    def to_chunks(t):
        return t.reshape(b, nc, HG_CHUNK, h, t.shape[-1]).transpose(1, 0, 3, 2, 4)

    mask = jnp.tril(jnp.ones((HG_CHUNK, HG_CHUNK), dtype=bool))

    def step(s, blk):
        qc, kc, vc, gc = blk
        gcum = jnp.cumsum(gc, axis=2)
        rel = jnp.where(mask[:, :, None], gcum[:, :, :, None, :] - gcum[:, :, None, :, :], -jnp.inf)
        attn = jnp.einsum('bhtd,bhsd,bhtsd->bhts', qc, kc, jnp.exp(rel))
        o = jnp.einsum('bhts,bhsv->bhtv', attn, vc) + jnp.einsum('bhtd,bhdv->bhtv', qc * jnp.exp(gcum), s)
        glast = gcum[:, :, -1, :]
        s = jnp.exp(glast)[..., None] * s + jnp.einsum('bhsd,bhsv->bhdv', kc * jnp.exp(glast[:, :, None, :] - gcum), vc)
        return s, o

    s_fin, o = lax.scan(step, s0, (to_chunks(q), to_chunks(k), to_chunks(v), to_chunks(g)))
    return o.transpose(1, 0, 3, 2, 4).reshape(b, n, h, HG_DV), s_fin


def hgrn2_bidir(q, v, k_f, g_f, k_b, g_b, s0_f, s0_b):
    o_f, s_f = hgrn2_chunked(q, k_f, v, g_f, s0_f)
    o_b, s_b = hgrn2_chunked(jnp.flip(q, 1), jnp.flip(k_b, 1), jnp.flip(v, 1), jnp.flip(g_b, 1), s0_b)
    return o_f + jnp.flip(o_b, 1), s_f, s_b


def hg_final_state(k, v, g):
    gcum = jnp.cumsum(g, axis=1)
    return jnp.einsum('blhd,blhv->bhdv', k * jnp.exp(gcum[:, -1:] - gcum), v)


def hg_readout(o, gate, o_norm_g):
    b, n = o.shape[:2]
    o = rmsnorm(o, o_norm_g) * jax.nn.silu(to_heads(gate, HG_DV))
    return o.reshape(b, n, GROUP_W)


def short_conv(t, w, b):
    n = t.shape[1]
    pad = SHORT_CONV // 2
    tp = jnp.pad(t, ((0, 0), (pad, pad), (0, 0)))
    return sum(tp[:, j:j + n] * w[j] for j in range(SHORT_CONV)) + b


def hyena_filter_spectra(n, w1, b1, w2, b2, w3):
    t = jnp.arange(n, dtype=F32)
    t_norm = t / max(n - 1, 1)
    bands = jnp.linspace(1e-4, HY_BANDS - 1, HY_BANDS, dtype=F32)
    ang = (2.0 * math.pi * t / n)[:, None] * bands[None, :]
    emb = jnp.concatenate([t_norm[:, None], jnp.cos(ang), -jnp.sin(ang)], axis=-1)
    h = jnp.sin(emb @ w1.astype(F32) + b1.astype(F32))
    h = jnp.sin(h @ w2.astype(F32) + b2.astype(F32))
    h = (h @ w3.astype(F32)).reshape(n, HY_ORDER, 2, HY_CH)
    deltas = jnp.linspace(math.log(HY_TARGET) / HY_SLOW, math.log(HY_TARGET) / HY_FAST, HY_CH, dtype=F32)
    window = jnp.exp(-t_norm[:, None] * jnp.abs(deltas)[None, :]) + HY_SHIFT
    h = h * window[:, None, None, :]
    h_circ = jnp.concatenate([h[:, :, 0], jnp.zeros((1, HY_ORDER, HY_CH), F32), jnp.flip(h[:n - 1, :, 1], 0)], axis=0)
    h_circ = h_circ / jnp.sum(jnp.abs(h_circ), axis=0, keepdims=True)
    return jnp.fft.rfft(h_circ, n=2 * n, axis=0)


def fft_long_conv(u, hf, bias):
    n = u.shape[1]
    y = jnp.fft.irfft(jnp.fft.rfft(u, n=2 * n, axis=1) * hf[None], n=2 * n, axis=1)[:, :n]
    return y + u * bias


def hyena_mixer(proj, conv_w, conv_b, w1, b1, w2, b2, w3, bias):
    n = proj.shape[1]
    u = short_conv(proj.astype(F32), conv_w.astype(F32), conv_b.astype(F32))
    v, x1, x2 = jnp.split(u, 3, axis=-1)
    hf = hyena_filter_spectra(n, w1, b1, w2, b2, w3)
    z = v
    for o, gate in enumerate((x1, x2)):
        z = gate * fft_long_conv(z, hf[:, o], bias[o].astype(F32))
    return z


def sqrelu_mlp(h, w1, w2):
    return jnp.square(jax.nn.relu(h @ w1)) @ w2


def setup_inputs(seed: int = 0) -> dict:
    key = jax.random.key(seed)
    keys = iter(jax.random.split(key, 48))

    def nrm(shape, scale):
        return scale * jax.random.normal(next(keys), shape, F32)

    def gain(shape):
        return 1.0 + nrm(shape, 0.02)

    G, P, N = S5_NGROUPS, S5_STATE, S5_GROUP
    return {
        'x': nrm((BATCH, SEQ, D_MODEL), 1.0),
        'c': nrm((BATCH, D_MODEL), 1.0),
        'ctx': nrm((BATCH, CTX_LEN, D_MODEL), 1.0),
        'c_ctx': nrm((D_MODEL,), 1.0),
        'w_mod': nrm((DEPTH, D_MODEL, N_MOD * D_MODEL), 0.5 * D_MODEL ** -0.5),
        'b_mod': nrm((DEPTH, N_MOD * D_MODEL), 0.02),
        'norm1': gain((DEPTH, D_MODEL)),
        'norm2': gain((DEPTH, D_MODEL)),
        'w_in': nrm((DEPTH, D_MODEL, N_IN_COLS), D_MODEL ** -0.5),
        'w_out': nrm((DEPTH, D_MIX, D_MODEL), D_MIX ** -0.5),
        'mla_q_a_norm': gain((DEPTH, Q_LORA)),
        'mla_kv_a_norm': gain((DEPTH, KV_LORA)),
        'mla_w_uq': nrm((DEPTH, Q_LORA, MLA_HEADS * MLA_QK), Q_LORA ** -0.5),
        'mla_w_ukv': nrm((DEPTH, KV_LORA, MLA_HEADS * (MLA_NOPE + MLA_V)), KV_LORA ** -0.5),
        'mla_q_norm': gain((DEPTH, MLA_QK)),
        'mla_k_norm': gain((DEPTH, MLA_QK)),
        's5_lam_re': -0.5 + nrm((DEPTH, 2, G, P), 0.01),
        's5_lam_im': math.pi * jnp.arange(P, dtype=F32) + nrm((DEPTH, 2, G, P), 0.01),
        's5_log_step': jax.random.uniform(next(keys), (DEPTH, 2, G), F32, minval=math.log(1e-3), maxval=math.log(1e-1)),
        's5_b_re': nrm((DEPTH, 2, G, P, N), (0.5 / N) ** 0.5),
        's5_b_im': nrm((DEPTH, 2, G, P, N), (0.5 / N) ** 0.5),
        's5_c_re': nrm((DEPTH, 2, G, N, P), 0.5),
        's5_c_im': nrm((DEPTH, 2, G, N, P), 0.5),
        's5_d': nrm((DEPTH, S5_CH), 1.0),
        's5_w_glu': nrm((DEPTH, S5_CH, S5_CH), S5_CH ** -0.5),
        's5_b_glu': nrm((DEPTH, S5_CH), 0.02),
        'hg_lower_bounds': nrm((2, DEPTH, GROUP_W), 0.5),
        'hg_o_norm': gain((DEPTH, HG_DV)),
        'hy_conv_w': nrm((DEPTH, SHORT_CONV, 3 * HY_CH), SHORT_CONV ** -0.5),
        'hy_conv_b': nrm((DEPTH, 3 * HY_CH), 0.02),
        'hy_w1': nrm((DEPTH, HY_EMB, HY_HIDDEN), 1.0),
        'hy_b1': nrm((DEPTH, HY_HIDDEN), 0.1),
        'hy_w2': nrm((DEPTH, HY_HIDDEN, HY_HIDDEN), HY_HIDDEN ** -0.5),
        'hy_b2': nrm((DEPTH, HY_HIDDEN), 0.1),
        'hy_w3': nrm((DEPTH, HY_HIDDEN, HY_ORDER * 2 * HY_CH), HY_HIDDEN ** -0.5),
        'hy_bias': nrm((DEPTH, HY_ORDER, HY_CH), 1.0),
        'mlp_w1': nrm((DEPTH, D_MODEL, D_FF), D_MODEL ** -0.5),
        'mlp_w2': nrm((DEPTH, D_FF, D_MODEL), D_FF ** -0.5),
    }


def reference(x, c, ctx, c_ctx, w_mod, b_mod, norm1, norm2, w_in, w_out,
              mla_q_a_norm, mla_kv_a_norm, mla_w_uq, mla_w_ukv, mla_q_norm, mla_k_norm,
              s5_lam_re, s5_lam_im, s5_log_step, s5_b_re, s5_b_im, s5_c_re, s5_c_im, s5_d, s5_w_glu, s5_b_glu,
              hg_lower_bounds, hg_o_norm,
              hy_conv_w, hy_conv_b, hy_w1, hy_b1, hy_w2, hy_b2, hy_w3, hy_bias,
              mlp_w1, mlp_w2):
    dtype = x.dtype
    bsz, seq_len, _ = x.shape
    n_rows = seq_len // GRID_W
    rope = rope_2d_tables(n_rows)
    sm = jax.nn.softmax(hg_lower_bounds.astype(F32), axis=1)
    lower_bounds = jnp.clip(jnp.cumsum(sm, axis=1) - sm[:, :1], 0.0, 1.0)
    xc = ctx
    for l in range(DEPTH):
        last = l == DEPTH - 1
        mod_l = [m[:, None, :] for m in jnp.split(jax.nn.silu(c) @ w_mod[l] + b_mod[l], N_MOD, axis=-1)]
        mod_c = [m[None, None, :] for m in jnp.split(jax.nn.silu(c_ctx) @ w_mod[l] + b_mod[l], N_MOD, axis=-1)]
        h_l = modulate(x, norm1[l], mod_l[0], mod_l[1])
        h_c = modulate(xc, norm1[l], mod_c[0], mod_c[1])
        pl = split_cols(h_l @ w_in[l])
        pc = split_cols(h_c @ (w_in[l][:, :N_STATE_COLS] if last else w_in[l]))

        k_c, v_c = mla_kv(pc['ckv'], pc['krope'], mla_kv_a_norm[l], mla_w_ukv[l], mla_k_norm[l], None)
        k_l, v_l = mla_kv(pl['ckv'], pl['krope'], mla_kv_a_norm[l], mla_w_ukv[l], mla_k_norm[l], rope)
        q_l = mla_q(pl['cq'], mla_q_a_norm[l], mla_w_uq[l], mla_q_norm[l], rope)
        y_a = block_attention(q_l, jnp.concatenate([k_l, k_c], axis=2), jnp.concatenate([v_l, v_c], axis=2))

        lam_f, bb_f = s5_discretize(s5_lam_re[l, 0], s5_lam_im[l, 0], s5_log_step[l, 0], s5_b_re[l, 0], s5_b_im[l, 0])
        lam_b, bb_b = s5_discretize(s5_lam_re[l, 1], s5_lam_im[l, 1], s5_log_step[l, 1], s5_b_re[l, 1], s5_b_im[l, 1])
        c_f = lax.complex(s5_c_re[l, 0].astype(F32), s5_c_im[l, 0].astype(F32))
        c_b = lax.complex(s5_c_re[l, 1].astype(F32), s5_c_im[l, 1].astype(F32))
        u_c = s5_groups(pc['s5u'])
        u_l = s5_groups(pl['s5u'])
        hc_f, fin_f = s5_states(u_c, lam_f, bb_f, None, False)
        hc_b, fin_b = s5_states(u_c, lam_b, bb_b, None, True)
        hl_f, _ = s5_states(u_l, lam_f, bb_f, fin_f, False)
        hl_b, _ = s5_states(u_l, lam_b, bb_b, fin_b, True)
        y_b = s5_readout(u_l, hl_f, hl_b, c_f, c_b, s5_d[l], s5_w_glu[l], s5_b_glu[l])

        g_cf, k_cf = hg_forget(pc['hg_f_fwd'], lower_bounds[0, l])
        g_cb, k_cb = hg_forget(pc['hg_f_bwd'], lower_bounds[1, l])
        i_c = to_heads(pc['hg_i'], HG_DV)
        if last:
            s_f = hg_final_state(k_cf, i_c, g_cf)
            s_b = hg_final_state(jnp.flip(k_cb, 1), jnp.flip(i_c, 1), jnp.flip(g_cb, 1))
        else:
            q_hc = jax.nn.silu(to_heads(pc['hg_q'], HG_DK))
            s0 = jnp.zeros((bsz, HG_HEADS, HG_DK, HG_DV), F32)
            o_hc, s_f, s_b = hgrn2_bidir(q_hc, i_c, k_cf, g_cf, k_cb, g_cb, s0, s0)
        g_lf, k_lf = hg_forget(pl['hg_f_fwd'], lower_bounds[0, l])
        g_lb, k_lb = hg_forget(pl['hg_f_bwd'], lower_bounds[1, l])
        q_hl = jax.nn.silu(to_heads(pl['hg_q'], HG_DK))
        o_hl, _, _ = hgrn2_bidir(q_hl, to_heads(pl['hg_i'], HG_DV), k_lf, g_lf, k_lb, g_lb, s_f, s_b)
        y_c = hg_readout(o_hl, pl['hg_g'], hg_o_norm[l])

        hy_p = (hy_conv_w[l], hy_conv_b[l], hy_w1[l], hy_b1[l], hy_w2[l], hy_b2[l], hy_w3[l], hy_bias[l])
        y_d = hyena_mixer(pl['hy'], *hy_p)

        mix_l = jnp.concatenate([y_a, y_b, y_c, y_d], axis=-1).astype(dtype) @ w_out[l]
        x = x + mod_l[2] * mix_l
        x = x + mod_l[5] * sqrelu_mlp(modulate(x, norm2[l], mod_l[3], mod_l[4]), mlp_w1[l], mlp_w2[l])

        if not last:
            y_ac = merge_heads(attend(mla_q(pc['cq'], mla_q_a_norm[l], mla_w_uq[l], mla_q_norm[l], None), k_c, v_c))
            y_bc = s5_readout(u_c, hc_f, hc_b, c_f, c_b, s5_d[l], s5_w_glu[l], s5_b_glu[l])
            y_cc = hg_readout(o_hc, pc['hg_g'], hg_o_norm[l])
            y_dc = hyena_mixer(pc['hy'], *hy_p)
            mix_c = jnp.concatenate([y_ac, y_bc, y_cc, y_dc], axis=-1).astype(dtype) @ w_out[l]
            xc = xc + mod_c[2] * mix_c
            xc = xc + mod_c[5] * sqrelu_mlp(modulate(xc, norm2[l], mod_c[3], mod_c[4]), mlp_w1[l], mlp_w2[l])
    return x
```

```python
import functools
import math

import numpy as np
import jax
import jax.numpy as jnp
from jax import lax
from jax.experimental import pallas as pl
from jax.experimental.pallas import tpu as pltpu

F32 = jnp.float32
BF16 = jnp.bfloat16
EPS = 1e-6

GRID_W = 64
GROUP_W = 512
N_MOD = 6
MLA_NOPE = 128
MLA_ROPE = 64
MLA_QK = MLA_NOPE + MLA_ROPE
MLA_HEADS = 4
KV_LORA = 256
MLA_SCALE = 1.0 / math.sqrt(MLA_QK)
ROPE_BASE = 10000.0
S5_G = 32
S5_N = 16
S5_P = 64
S5_T = 32
HG_HEADS = 4
HG_D = 128
HG_SUB = 16
HY_EMB = 33
HY_BANDS = 16
HY_HIDDEN = 64
HY_TARGET = 1e-2
HY_FAST = 0.3
HY_SLOW = 1.5
HY_SHIFT = 0.05

LANES = 128
VMEM_LIMIT_MB = 56

PB_S5U, PB_HGF, PB_HGB, PB_HGI, PB_CQ, PB_HGQ, PB_HGG, PB_HY, PB_KV = 0, 1, 2, 3, 4, 5, 6, 7, 10
N_PROJ = 11 * GROUP_W


def _pc(kernel, out_shape, grid, in_specs, out_specs, scratch=(), vmem_mb=VMEM_LIMIT_MB):
    return pl.pallas_call(
        kernel, out_shape=out_shape, grid=grid, in_specs=in_specs, out_specs=out_specs,
        scratch_shapes=list(scratch),
        compiler_params=pltpu.CompilerParams(
            dimension_semantics=("arbitrary",) * len(grid), vmem_limit_bytes=vmem_mb << 20))


def _tile(n, pref):
    t = min(n, pref)
    while n % t:
        t //= 2
    return t


def _silu(x):
    return x * jax.nn.sigmoid(x)


def _modvec_kernel(c_ref, w_ref, b_ref, o_ref):
    s = _silu(c_ref[...])
    o_ref[...] = jnp.dot(s.astype(BF16), w_ref[...], preferred_element_type=F32) + b_ref[...]


def modvec(c8, w, b):
    d, n = w.shape
    tn = _tile(n, 1536)
    return _pc(_modvec_kernel, jax.ShapeDtypeStruct((8, n), F32), (n // tn,),
               [pl.BlockSpec((8, d), lambda j: (0, 0)),
                pl.BlockSpec((d, tn), lambda j: (0, j)),
                pl.BlockSpec((1, tn), lambda j: (0, j))],
               pl.BlockSpec((8, tn), lambda j: (0, j)))(c8, w, b)


def _modproj_kernel(x_ref, sh_ref, sc_ref, g_ref, w_ref, o_ref, h_ref, *, act):
    @pl.when(pl.program_id(2) == 0)
    def _():
        x = x_ref[...]
        ms = jnp.mean(x * x, axis=-1, keepdims=True)
        y = x * lax.rsqrt(ms + EPS) * g_ref[...]
        h_ref[...] = (y * (1.0 + sc_ref[...]) + sh_ref[...]).astype(BF16)

    acc = jnp.dot(h_ref[...], w_ref[...], preferred_element_type=F32)
    if act == "relu2":
        acc = jnp.square(jnp.maximum(acc, 0.0))
    o_ref[...] = acc.astype(o_ref.dtype)


def modproj(x, shift, scale, g, w, out_dtype, act=None, tm=512, tn=512):
    b, l, d = x.shape
    n = w.shape[1]
    tm, tn = _tile(l, tm), _tile(n, tn)
    per_b = shift.shape[0] == b
    mod_map = (lambda bi, i, j: (bi, 0, 0)) if per_b else (lambda bi, i, j: (0, 0, 0))
    return _pc(functools.partial(_modproj_kernel, act=act),
               jax.ShapeDtypeStruct((b, l, n), out_dtype), (b, l // tm, n // tn),
               [pl.BlockSpec((None, tm, d), lambda bi, i, j: (bi, i, 0)),
                pl.BlockSpec((None, 1, d), mod_map),
                pl.BlockSpec((None, 1, d), mod_map),
                pl.BlockSpec((1, d), lambda bi, i, j: (0, 0)),
                pl.BlockSpec((d, tn), lambda bi, i, j: (0, j))],
               pl.BlockSpec((None, tm, tn), lambda bi, i, j: (bi, i, j)),
               scratch=[pltpu.VMEM((tm, d), BF16)])(x, shift, scale, g, w)


def _mmres_kernel(*refs, n_lhs, kw):
    a_refs = refs[:n_lhs]
    w_ref, res_ref, gate_ref, o_ref = refs[n_lhs:]
    acc = None
    for i, a in enumerate(a_refs):
        d = jnp.dot(a[...], w_ref[i * kw:(i + 1) * kw, :], preferred_element_type=F32)
        acc = d if acc is None else acc + d
    o_ref[...] = res_ref[...] + gate_ref[...] * acc


def mix_out(ys, yd_hlay, w, res, gate, tm=512, tn=1024):
    b, l, kw = ys[0].shape
    n = w.shape[1]
    tm, tn = _tile(l, tm), _tile(n, tn)
    per_b = gate.shape[0] == b
    gmap = (lambda bi, i, j: (bi, 0, j)) if per_b else (lambda bi, i, j: (0, 0, j))
    y_spec = pl.BlockSpec((None, tm, kw), lambda bi, i, j: (bi, i, 0))
    yd_spec = pl.BlockSpec((None, tm, kw), lambda bi, i, j: (bi % 2, i, bi // 2))
    return _pc(functools.partial(_mmres_kernel, n_lhs=4, kw=kw),
               jax.ShapeDtypeStruct(res.shape, F32), (b, l // tm, n // tn),
               [y_spec, y_spec, y_spec, yd_spec,
                pl.BlockSpec((4 * kw, tn), lambda bi, i, j: (0, j)),
                pl.BlockSpec((None, tm, tn), lambda bi, i, j: (bi, i, j)),
                pl.BlockSpec((None, 1, tn), gmap)],
               pl.BlockSpec((None, tm, tn), lambda bi, i, j: (bi, i, j)))(*ys, yd_hlay, w, res, gate)


def _mmresk_kernel(a_ref, w_ref, res_ref, gate_ref, o_ref, acc_ref):
    k = pl.program_id(3)

    @pl.when(k == 0)
    def _():
        acc_ref[...] = jnp.zeros_like(acc_ref)

    acc_ref[...] += jnp.dot(a_ref[...], w_ref[...], preferred_element_type=F32)

    @pl.when(k == pl.num_programs(3) - 1)
    def _():
        o_ref[...] = res_ref[...] + gate_ref[...] * acc_ref[...]


def mlp_out(a, w, res, gate, tm=1024, tn=1024, tk=1024):
    b, l, kk = a.shape
    n = w.shape[1]
    tm, tn, tk = _tile(l, tm), _tile(n, tn), _tile(kk, tk)
    per_b = gate.shape[0] == b
    gmap = (lambda bi, i, j, k: (bi, 0, j)) if per_b else (lambda bi, i, j, k: (0, 0, j))
    return _pc(_mmresk_kernel, jax.ShapeDtypeStruct(res.shape, F32), (b, l // tm, n // tn, kk // tk),
               [pl.BlockSpec((None, tm, tk), lambda bi, i, j, k: (bi, i, k)),
                pl.BlockSpec((tk, tn), lambda bi, i, j, k: (k, j)),
                pl.BlockSpec((None, tm, tn), lambda bi, i, j, k: (bi, i, j)),
                pl.BlockSpec((None, 1, tn), gmap)],
               pl.BlockSpec((None, tm, tn), lambda bi, i, j, k: (bi, i, j)),
               scratch=[pltpu.VMEM((tm, tn), F32)])(a, w, res, gate)


def _mla_prep_kernel(kv_ref, cq_ref, cs_ref, gkv_ref, gqa_ref, gkn_ref, gkt_ref, gqn_ref, gqt_ref,
                     wkv_ref, wq_ref, k_ref, v_ref, q_ref):
    lane = lax.broadcasted_iota(jnp.int32, (1, LANES), 1)
    low = (lane < MLA_ROPE).astype(F32)
    cs = cs_ref[...]
    kv = kv_ref[...]
    ckv = kv[:, :KV_LORA]
    ckvn = ckv * lax.rsqrt(jnp.mean(ckv * ckv, axis=-1, keepdims=True) + EPS) * gkv_ref[...]
    kvp = jnp.dot(ckvn.astype(BF16), wkv_ref[...], preferred_element_type=F32)
    pe2 = kv[:, KV_LORA:KV_LORA + LANES]
    ss_pe = jnp.sum(pe2 * pe2 * low, axis=-1, keepdims=True)
    pe_tab = cs * gkt_ref[...]
    for h in range(MLA_HEADS):
        knope = kvp[:, 256 * h:256 * h + 128]
        rs = lax.rsqrt((jnp.sum(knope * knope, axis=-1, keepdims=True) + ss_pe) * (1.0 / MLA_QK) + EPS)
        t = pe2 * pe_tab * rs
        t = (t + pltpu.roll(t, MLA_ROPE, axis=1)) * low
        k_ref[h, :, 0:128] = (knope * gkn_ref[...] * rs).astype(BF16)
        k_ref[h, :, 128:256] = t.astype(BF16)
        v_ref[h] = kvp[:, 256 * h + 128:256 * h + 256].astype(BF16)
    if q_ref is not None:
        cq = cq_ref[...]
        cqn = cq * lax.rsqrt(jnp.mean(cq * cq, axis=-1, keepdims=True) + EPS) * gqa_ref[...]
        qp = jnp.dot(cqn.astype(BF16), wq_ref[...], preferred_element_type=F32)
        q_tab = cs * gqt_ref[...]
        for h in range(MLA_HEADS):
            qn = qp[:, 256 * h:256 * h + 128]
            qe = qp[:, 256 * h + 128:256 * h + 256]
            ss = jnp.sum(qn * qn, axis=-1, keepdims=True) + jnp.sum(qe * qe * low, axis=-1, keepdims=True)
            rs = lax.rsqrt(ss * (1.0 / MLA_QK) + EPS) * MLA_SCALE
            t = qe * q_tab * rs
            t = t + pltpu.roll(t, MLA_ROPE, axis=1)
            q_ref[h, :, 0:128] = (qn * gqn_ref[...] * rs).astype(BF16)
            q_ref[h, :, 128:256] = t.astype(BF16)


def _mla_prep_kernel_noq(kv_ref, cs_ref, gkv_ref, gkn_ref, gkt_ref, wkv_ref, k_ref, v_ref):
    _mla_prep_kernel(kv_ref, None, cs_ref, gkv_ref, None, gkn_ref, gkt_ref, None, None, wkv_ref, None,
                     k_ref, v_ref, None)


def mla_prep(proj, cs, p, with_q, tm=512):
    b, l, _ = proj.shape
    tm = _tile(l, tm)
    row = lambda w: pl.BlockSpec((1, w), lambda bi, i: (0, 0))
    kv_spec = pl.BlockSpec((None, tm, GROUP_W), lambda bi, i: (bi, i, PB_KV))
    cq_spec = pl.BlockSpec((None, tm, GROUP_W), lambda bi, i: (bi, i, PB_CQ))
    cs_spec = pl.BlockSpec((tm, LANES), lambda bi, i: (i, 0))
    wkv_spec = pl.BlockSpec((KV_LORA, 1024), lambda bi, i: (0, 0))
    wq_spec = pl.BlockSpec((GROUP_W, 1024), lambda bi, i: (0, 0))
    hd = lambda w: pl.BlockSpec((None, MLA_HEADS, tm, w), lambda bi, i: (bi, 0, i, 0))
    k_sh = jax.ShapeDtypeStruct((b, MLA_HEADS, l, 256), BF16)
    v_sh = jax.ShapeDtypeStruct((b, MLA_HEADS, l, 128), BF16)
    if with_q:
        return _pc(_mla_prep_kernel, (k_sh, v_sh, k_sh), (b, l // tm),
                   [kv_spec, cq_spec, cs_spec, row(KV_LORA), row(GROUP_W), row(128), row(128), row(128), row(128),
                    wkv_spec, wq_spec],
                   (hd(256), hd(128), hd(256)))(
            proj, proj, cs, p["gkv"], p["gqa"], p["gkn"], p["gkt"], p["gqn"], p["gqt"], p["wkv"], p["wq"])
    k, v = _pc(_mla_prep_kernel_noq, (k_sh, v_sh), (b, l // tm),
               [kv_spec, cs_spec, row(KV_LORA), row(128), row(128), wkv_spec],
               (hd(256), hd(128)))(proj, cs, p["gkv"], p["gkn"], p["gkt"], p["wkv"])
    return k, v, None


def _attn_kernel(*refs, n_seg):
    q = refs[0][...]
    o_ref = refs[1 + 2 * n_seg]
    scores = [lax.dot_general(q, refs[1 + 2 * i][...], (((1,), (1,)), ((), ())), preferred_element_type=F32)
              for i in range(n_seg)]
    m = scores[0].max(axis=-1, keepdims=True)
    for s in scores[1:]:
        m = jnp.maximum(m, s.max(axis=-1, keepdims=True))
    den, acc = None, None
    for i, s in enumerate(scores):
        p = jnp.exp(s - m)
        d = p.sum(axis=-1, keepdims=True)
        a = jnp.dot(p.astype(BF16), refs[2 + 2 * i][...], preferred_element_type=F32)
        den = d if den is None else den + d
        acc = a if acc is None else acc + a
    o_ref[...] = (acc / den).astype(o_ref.dtype)


def attention(q, kvs, tq=256):
    b, h, l, _ = q.shape
    tq = _tile(l, tq)
    specs = [pl.BlockSpec((None, None, tq, 256), lambda bi, hi, i: (bi, hi, i, 0))]
    args = [q]
    for k, v in kvs:
        lk = k.shape[2]
        specs += [pl.BlockSpec((None, None, lk, 256), lambda bi, hi, i: (bi, hi, 0, 0)),
                  pl.BlockSpec((None, None, lk, 128), lambda bi, hi, i: (bi, hi, 0, 0))]
        args += [k, v]
    return _pc(functools.partial(_attn_kernel, n_seg=len(kvs)),
               jax.ShapeDtypeStruct((b, l, h * 128), BF16), (b, h, l // tq), specs,
               pl.BlockSpec((None, tq, 128), lambda bi, hi, i: (bi, i, hi)))(*args)


def s5_tables(lam_re, lam_im, log_step, b_re, b_im, c_re, c_im):
    t = S5_T
    step = jnp.exp(log_step.astype(F32))[..., None]
    re, im = lam_re.astype(F32) * step, lam_im.astype(F32) * step
    d = jnp.arange(t + 1, dtype=F32)[:, None, None, None]
    mag = jnp.exp(d * re)
    ar, ai = mag * jnp.cos(d * im), mag * jnp.sin(d * im)
    lr, li = lam_re.astype(F32), lam_im.astype(F32)
    den = lr * lr + li * li
    er, ei = ar[1] - 1.0, ai[1]
    cr, ci = (er * lr + ei * li) / den, (ei * lr - er * li) / den
    br, bi = b_re.astype(F32), b_im.astype(F32)
    bbr = cr[..., None] * br - ci[..., None] * bi
    bbi = cr[..., None] * bi + ci[..., None] * br
    ccr, cci = c_re.astype(F32), c_im.astype(F32)
    car = ccr[None] * ar[:, :, :, None, :] - cci[None] * ai[:, :, :, None, :]
    cai = ccr[None] * ai[:, :, :, None, :] + cci[None] * ar[:, :, :, None, :]
    kd = jnp.einsum("dxgnp,xgpm->xgdnm", car[:t], bbr) - jnp.einsum("dxgnp,xgpm->xgdnm", cai[:t], bbi)
    dt = jnp.arange(t)[:, None] - jnp.arange(t)[None, :]
    kf = kd[0][:, jnp.clip(dt, 0, t - 1)] * (dt >= 0)[None, :, :, None, None].astype(F32)
    kb = kd[1][:, jnp.clip(-dt, 0, t - 1)] * (dt <= 0)[None, :, :, None, None].astype(F32)
    m = (kf + kb).transpose(0, 2, 4, 1, 3).reshape(S5_G, t * S5_N, t * S5_N)
    pw_f = jnp.arange(t - 1, -1, -1)
    pw_b = jnp.arange(t)

    def state_w(x, pw):
        a_r, a_i = ar[pw, x], ai[pw, x]
        wr = a_r[..., None] * bbr[x][None] - a_i[..., None] * bbi[x][None]
        wi = a_r[..., None] * bbi[x][None] + a_i[..., None] * bbr[x][None]
        w = jnp.concatenate([wr, wi], axis=2)
        return w.transpose(1, 0, 3, 2).reshape(S5_G, t * S5_N, 2 * S5_P)

    wm = jnp.concatenate([m, state_w(0, pw_f), state_w(1, pw_b)], axis=-1)
    pv_f = jnp.arange(1, t + 1)
    pv_b = jnp.arange(t, 0, -1)

    def read_v(x, pv):
        v = jnp.concatenate([car[pv, x], -cai[pv, x]], axis=-1)
        return v.transpose(1, 3, 0, 2).reshape(S5_G, 2 * S5_P, t * S5_N)

    vm = jnp.concatenate([read_v(0, pv_f), read_v(1, pv_b)], axis=1)
    a1 = jnp.concatenate([ar[t], ar[t]], axis=-1)
    a2 = jnp.concatenate([-ai[t], ai[t]], axis=-1)
    dec = jnp.stack([jnp.concatenate([a1[0], a1[1]], axis=-1),
                     jnp.concatenate([a2[0], a2[1]], axis=-1)], axis=1)
    return wm.astype(BF16), vm.astype(BF16), dec


def _s5a_kernel(u_ref, wm_ref, y_ref, s_ref):
    r = jnp.dot(u_ref[...], wm_ref[...], preferred_element_type=F32)
    y_ref[...] = r[:, :GROUP_W]
    s_ref[...] = r[:, GROUP_W:]


def _s5b_kernel(yi_ref, s_ref, x0_ref, dec_ref, vm_ref, y_ref, xf_ref, xin_ref, *, nchunk):
    a1 = dec_ref[0:1, :]
    a2 = dec_ref[1:2, :]

    def step(x, s):
        sw = jnp.concatenate([pltpu.roll(x[:, :128], 64, axis=1), pltpu.roll(x[:, 128:], 64, axis=1)], axis=1)
        return a1 * x + a2 * sw + s

    def fwd(j, x):
        r0 = pl.multiple_of(j * 8, 8)
        xin_ref[pl.ds(r0, 8), 0:128] = x
        xs = step(jnp.concatenate([x, x], axis=1), s_ref[pl.ds(r0, 8), :])
        return xs[:, :128]

    def bwd(jj, x):
        r0 = pl.multiple_of((nchunk - 1 - jj) * 8, 8)
        xin_ref[pl.ds(r0, 8), 128:256] = x
        xs = step(jnp.concatenate([x, x], axis=1), s_ref[pl.ds(r0, 8), :])
        return xs[:, 128:]

    x0 = x0_ref[...]
    xf = lax.fori_loop(0, nchunk, fwd, x0[:, :128])
    xb = lax.fori_loop(0, nchunk, bwd, x0[:, 128:])
    xf_ref[:, 0:128] = xf
    xf_ref[:, 128:256] = xb
    y_ref[...] = yi_ref[...] + jnp.dot(xin_ref[...].astype(BF16), vm_ref[...], preferred_element_type=F32)


def s5_scan(u, x0, tabs):
    wm, vm, dec = tabs
    b, l, _ = u.shape
    j = l // S5_T
    r = j * 8
    ug = u.astype(BF16).reshape(b, j, S5_T, S5_G, S5_N).transpose(3, 1, 0, 2, 4)
    ug = jnp.pad(ug, ((0, 0), (0, 0), (0, 8 - b), (0, 0), (0, 0))).reshape(S5_G, r, GROUP_W)
    g3 = lambda w: pl.BlockSpec((None, r, w), lambda g: (g, 0, 0))
    yi, s = _pc(_s5a_kernel,
                (jax.ShapeDtypeStruct((S5_G, r, GROUP_W), F32), jax.ShapeDtypeStruct((S5_G, r, 256), F32)),
                (S5_G,),
                [g3(GROUP_W), pl.BlockSpec((None, GROUP_W, 768), lambda g: (g, 0, 0))],
                (g3(GROUP_W), g3(256)))(ug, wm)
    y, xf = _pc(functools.partial(_s5b_kernel, nchunk=j),
                (jax.ShapeDtypeStruct((S5_G, r, GROUP_W), F32), jax.ShapeDtypeStruct((S5_G, 8, 256), F32)),
                (S5_G,),
                [g3(GROUP_W), g3(256),
                 pl.BlockSpec((None, 8, 256), lambda g: (g, 0, 0)),
                 pl.BlockSpec((None, 2, 256), lambda g: (g, 0, 0)),
                 pl.BlockSpec((None, 256, GROUP_W), lambda g: (g, 0, 0))],
                (g3(GROUP_W), pl.BlockSpec((None, 8, 256), lambda g: (g, 0, 0))),
                scratch=[pltpu.VMEM((r, 256), F32)])(yi, s, x0, dec, vm)
    y = y.reshape(S5_G, j, 8, S5_T, S5_N)[:, :, :b].transpose(2, 1, 3, 0, 4).reshape(b, l, GROUP_W)
    return y, xf


def _s5_out_kernel(y_ref, u_ref, d_ref, w_ref, b_ref, o_ref):
    y = jax.nn.gelu(y_ref[...] + d_ref[...] * u_ref[...])
    z = jnp.dot(y.astype(BF16), w_ref[...], preferred_element_type=F32) + b_ref[...]
    o_ref[...] = (y * jax.nn.sigmoid(z)).astype(o_ref.dtype)


def s5_out(y, proj, d, w, bias, tm=512):
    b, l, _ = y.shape
    tm = _tile(l, tm)
    row = pl.BlockSpec((1, GROUP_W), lambda bi, i: (0, 0))
    return _pc(_s5_out_kernel, jax.ShapeDtypeStruct((b, l, GROUP_W), BF16), (b, l // tm),
               [pl.BlockSpec((None, tm, GROUP_W), lambda bi, i: (bi, i, 0)),
                pl.BlockSpec((None, tm, GROUP_W), lambda bi, i: (bi, i, PB_S5U)),
                row, pl.BlockSpec((GROUP_W, GROUP_W), lambda bi, i: (0, 0)), row],
               pl.BlockSpec((None, tm, GROUP_W), lambda bi, i: (bi, i, 0)))(y, proj, d, w, bias)


def _hgrn_kernel(z_ref, i_ref, q_ref, la_ref, lb_ref, s0_ref, o_ref, sfin_ref, st_ref, *, reverse, nsub):
    c = HG_SUB

    @pl.when(pl.program_id(1) == 0)
    def _():
        st_ref[...] = s0_ref[...]

    ones = jnp.ones((HG_D, HG_D), BF16)
    row = lax.broadcasted_iota(jnp.int32, (c, HG_D), 0)

    def sub(jj, carry):
        j = (nsub - 1 - jj) if reverse else jj
        rows = pl.ds(pl.multiple_of(j * c, c), c)
        for h in range(HG_HEADS):
            sl = slice(HG_D * h, HG_D * (h + 1))
            z = z_ref[rows, sl]
            v = i_ref[rows, sl]
            q = _silu(q_ref[rows, sl])
            a = la_ref[:, sl]
            bb = lb_ref[:, sl] + (jnp.minimum(z, 0.0) - jnp.log1p(jnp.exp(-jnp.abs(z))))
            g = jnp.maximum(a, bb) + jnp.log1p(jnp.exp(-jnp.abs(a - bb)))
            k = 1.0 - jnp.exp(g)
            gc = g
            for sh in (1, 2, 4, 8):
                if reverse:
                    gc = gc + jnp.where(row < c - sh, pltpu.roll(gc, c - sh, axis=0), 0.0)
                else:
                    gc = gc + jnp.where(row >= sh, pltpu.roll(gc, sh, axis=0), 0.0)
            o = jnp.zeros((c, HG_D), F32)
            for s in range(c):
                keep = (row <= s) if reverse else (row >= s)
                w = jnp.exp(jnp.where(keep, gc - gc[s:s + 1, :], -1e30))
                p = (q * w) * k[s:s + 1, :]
                o = o + jnp.dot(p.astype(BF16), ones, preferred_element_type=F32) * v[s:s + 1, :]
            st = st_ref[h]
            o = o + lax.dot_general((q * jnp.exp(gc)).astype(BF16), st.astype(BF16),
                                    (((1,), (1,)), ((), ())), preferred_element_type=F32)
            gl = gc[0:1, :] if reverse else gc[c - 1:c, :]
            kd = k * jnp.exp(gl - gc)
            ds = lax.dot_general(v.astype(BF16), kd.astype(BF16), (((0,), (0,)), ((), ())),
                                 preferred_element_type=F32)
            st_ref[h] = st * jnp.exp(gl) + ds
            o_ref[rows, sl] = o
        return carry

    lax.fori_loop(0, nsub, sub, 0)

    @pl.when(pl.program_id(1) == pl.num_programs(1) - 1)
    def _():
        sfin_ref[...] = st_ref[...]


def hgrn_dir(proj, la, lb1, s0, reverse, tl=256):
    b, l, _ = proj.shape
    tl = _tile(l, tl)
    nb = l // tl
    blk = (lambda i: nb - 1 - i) if reverse else (lambda i: i)
    col = lambda c: pl.BlockSpec((None, tl, GROUP_W), lambda bi, i: (bi, blk(i), c))
    row = pl.BlockSpec((1, GROUP_W), lambda bi, i: (0, 0))
    st = pl.BlockSpec((None, HG_HEADS, HG_D, HG_D), lambda bi, i: (bi, 0, 0, 0))
    return _pc(functools.partial(_hgrn_kernel, reverse=reverse, nsub=tl // HG_SUB),
               (jax.ShapeDtypeStruct((b, l, GROUP_W), F32), jax.ShapeDtypeStruct(s0.shape, F32)),
               (b, nb),
               [col(PB_HGB if reverse else PB_HGF), col(PB_HGI), col(PB_HGQ), row, row, st],
               (pl.BlockSpec((None, tl, GROUP_W), lambda bi, i: (bi, blk(i), 0)), st),
               scratch=[pltpu.VMEM((HG_HEADS, HG_D, HG_D), F32)])(proj, proj, proj, la, lb1, s0)


def _hg_out_kernel(of_ref, ob_ref, g_ref, gn_ref, y_ref):
    o = of_ref[...] + ob_ref[...]
    gate = _silu(g_ref[...])
    for h in range(HG_HEADS):
        sl = slice(HG_D * h, HG_D * (h + 1))
        oh = o[:, sl]
        y = oh * lax.rsqrt(jnp.mean(oh * oh, axis=-1, keepdims=True) + EPS) * gn_ref[...]
        y_ref[:, sl] = (y * gate[:, sl]).astype(y_ref.dtype)


def hg_out(of, ob, proj, gn, tm=512):
    b, l, _ = of.shape
    tm = _tile(l, tm)
    t3 = lambda c: pl.BlockSpec((None, tm, GROUP_W), lambda bi, i: (bi, i, c))
    return _pc(_hg_out_kernel, jax.ShapeDtypeStruct((b, l, GROUP_W), BF16), (b, l // tm),
               [t3(0), t3(0), t3(PB_HGG), pl.BlockSpec((1, HG_D), lambda bi, i: (0, 0))],
               t3(0))(of, ob, proj, gn)


def _fft_factors(n_fft):
    n1 = {8192: 64, 512: 32, 1024: 32, 2048: 32, 4096: 64}[n_fft]
    return n1, n_fft // n1


def _dft_tables(n_fft, k1, m2, inverse):
    n1, n2 = _fft_factors(n_fft)
    sgn = 1.0 if inverse else -1.0

    def stacked(ang, scale=1.0):
        fr, fi = np.cos(ang) * scale, np.sin(ang) * scale
        return np.block([[fr, -fi], [fi, fr]])

    f1 = stacked(sgn * 2 * np.pi * np.outer(np.arange(n1), np.arange(k1)) / n1)
    f2 = stacked(sgn * 2 * np.pi * np.outer(np.arange(m2), np.arange(n2)) / n2, (1.0 / n_fft) if inverse else 1.0)
    ang = sgn * 2 * np.pi * np.outer(np.arange(n2), np.arange(n1)) / n_fft
    tw = np.stack([np.cos(ang), np.sin(ang)])[..., None] * np.ones((1, 1, 1, LANES))
    return jnp.asarray(f1, BF16), jnp.asarray(tw, F32), jnp.asarray(f2, BF16)


def _fft1_kernel(z_ref, f_ref, tw_ref, o_ref, *, g, pc, n1):
    z = z_ref[...]
    zin = z.reshape(z.shape[0] * z.shape[1], z.shape[2])
    a = jnp.dot(f_ref[...], zin, preferred_element_type=F32)
    rep = pc // LANES
    for i in range(g):
        ar, ai = a[:n1, i * pc:(i + 1) * pc], a[n1:, i * pc:(i + 1) * pc]
        twr, twi = jnp.tile(tw_ref[0, i], (1, rep)), jnp.tile(tw_ref[1, i], (1, rep))
        o_ref[0, i] = (ar * twr - ai * twi).astype(o_ref.dtype)
        o_ref[1, i] = (ar * twi + ai * twr).astype(o_ref.dtype)


def fft_stage1(z, f1, tw, n_fft, g=4):
    n1, n2 = _fft_factors(n_fft)
    k1 = z.shape[1] // n2
    pc = z.shape[2]
    g = min(g, n2)
    zv = z.reshape(2, k1, n2 * pc)
    return _pc(functools.partial(_fft1_kernel, g=g, pc=pc, n1=n1),
               jax.ShapeDtypeStruct((2, n2, n1, pc), BF16), (n2 // g,),
               [pl.BlockSpec((2, k1, g * pc), lambda j: (0, 0, j)),
                pl.BlockSpec((2 * n1, 2 * k1), lambda j: (0, 0)),
                pl.BlockSpec((2, g, n1, LANES), lambda j: (0, j, 0, 0))],
               pl.BlockSpec((2, g, n1, pc), lambda j: (0, j, 0, 0)))(zv, f1, tw)


def _fft2_kernel(*refs, mode, m2, c, pairs):
    a_ref, f_ref = refs[0], refs[1]
    o_ref = refs[-1]
    a = a_ref[...]
    x = a.reshape(a.shape[0] * a.shape[1], a.shape[2])
    z = jnp.dot(f_ref[...], x, preferred_element_type=F32)
    zr, zi = z[:m2], z[m2:]
    if mode == "plain":
        o_ref[0] = zr.astype(o_ref.dtype)
        o_ref[1] = zi.astype(o_ref.dtype)
    elif mode == "filter":
        h_ref = refs[2]
        hr, hi = h_ref[0].astype(F32), h_ref[1].astype(F32)
        for p in range(pairs):
            sl = slice(p * c, (p + 1) * c)
            o_ref[0, :, sl] = (zr[:, sl] * hr - zi[:, sl] * hi).astype(o_ref.dtype)
            o_ref[1, :, sl] = (zr[:, sl] * hi + zi[:, sl] * hr).astype(o_ref.dtype)
    else:
        v_ref, g_ref, b_ref = refs[2], refs[3], refs[4]
        bias = jnp.tile(b_ref[...], (1, pairs))
        o_ref[0] = (g_ref[0].astype(F32) * (zr + v_ref[0].astype(F32) * bias)).astype(o_ref.dtype)
        o_ref[1] = (g_ref[1].astype(F32) * (zi + v_ref[1].astype(F32) * bias)).astype(o_ref.dtype)


def fft_stage2(a, f2, n_fft, mode, extra=(), order=0, n_orders=1):
    n1, n2 = _fft_factors(n_fft)
    pc = a.shape[3]
    m2 = f2.shape[0] // 2
    c = GROUP_W
    av = a.reshape(2, n2, n1 * pc)
    specs = [pl.BlockSpec((2, n2, pc), lambda k: (0, 0, k)),
             pl.BlockSpec((2 * m2, 2 * n2), lambda k: (0, 0))]
    args = [av, f2]
    if mode == "filter":
        (h,) = extra
        specs.append(pl.BlockSpec((2, n2, c), lambda k: (0, 0, k * n_orders + order)))
        args.append(h.reshape(2, n2, n1 * n_orders * c))
    elif mode == "final":
        v, gate, bias = extra
        nat = pl.BlockSpec((2, m2, pc), lambda k: (0, 0, k))
        specs += [nat, nat, pl.BlockSpec((1, c), lambda k: (0, 0))]
        args += [v.reshape(2, m2, n1 * pc), gate.reshape(2, m2, n1 * pc), bias]
    out = _pc(functools.partial(_fft2_kernel, mode=mode, m2=m2, c=c, pairs=pc // c),
              jax.ShapeDtypeStruct((2, m2, n1 * pc), BF16), (n1,), specs,
              pl.BlockSpec((2, m2, pc), lambda k: (0, 0, k)))(*args)
    return out.reshape(2, m2 * n1, pc)


def _hy_filter_kernel(emb_ref, w1_ref, b1_ref, w2_ref, b2_ref, w3f_ref, w3b_ref, win_ref, hf_ref, hb_ref, *, n):
    hp = lax.Precision.HIGHEST
    h = jnp.sin(jnp.dot(emb_ref[...], w1_ref[...], precision=hp, preferred_element_type=F32) + b1_ref[...])
    h = jnp.sin(jnp.dot(h, w2_ref[...], precision=hp, preferred_element_type=F32) + b2_ref[...])
    win = win_ref[...]
    hf = jnp.dot(h, w3f_ref[...], precision=hp, preferred_element_type=F32) * win
    hb = jnp.dot(h, w3b_ref[...], precision=hp, preferred_element_type=F32) * win
    row = lax.broadcasted_iota(jnp.int32, hb.shape, 0)
    den = (jnp.sum(jnp.abs(hf), axis=0, keepdims=True)
           + jnp.sum(jnp.where(row < n - 1, jnp.abs(hb), 0.0), axis=0, keepdims=True))
    hf_ref[...] = hf / den
    hb_ref[...] = hb / den


def hyena_filters(n, w1, b1, w2, b2, w3):
    t = np.arange(n, dtype=np.float32)
    t_norm = t / max(n - 1, 1)
    bands = np.linspace(1e-4, HY_BANDS - 1, HY_BANDS, dtype=np.float32)
    ang = (2.0 * math.pi * t / n)[:, None] * bands[None, :]
    emb = np.concatenate([t_norm[:, None], np.cos(ang), -np.sin(ang)], axis=-1).astype(np.float32)
    emb = jnp.asarray(np.pad(emb, ((0, 0), (0, LANES - HY_EMB))))
    deltas = np.linspace(math.log(HY_TARGET) / HY_SLOW, math.log(HY_TARGET) / HY_FAST, GROUP_W, dtype=np.float32)
    win = jnp.asarray(np.exp(-t_norm[:, None] * np.abs(deltas)[None, :]) + np.float32(HY_SHIFT), F32)
    w1p = jnp.pad(w1.astype(F32), ((0, LANES - HY_EMB), (0, 0)))
    w3r = w3.astype(F32).reshape(HY_HIDDEN, 2, 2, GROUP_W)
    w3f, w3b = w3r[:, :, 0].reshape(HY_HIDDEN, 2 * GROUP_W), w3r[:, :, 1].reshape(HY_HIDDEN, 2 * GROUP_W)
    tc = 256
    full = lambda a: pl.BlockSpec(a.shape, lambda j: (0, 0))
    colw = pl.BlockSpec((HY_HIDDEN, tc), lambda j: (0, j))
    outs = pl.BlockSpec((n, tc), lambda j: (0, j))
    b1r, b2r = b1.reshape(1, -1).astype(F32), b2.reshape(1, -1).astype(F32)
    w2f = w2.astype(F32)
    hf, hb = _pc(functools.partial(_hy_filter_kernel, n=n),
                 (jax.ShapeDtypeStruct((n, 2 * GROUP_W), F32),) * 2, (2 * GROUP_W // tc,),
                 [full(emb), full(w1p), full(b1r), full(w2f), full(b2r), colw, colw,
                  pl.BlockSpec((n, tc), lambda j: (0, j % (GROUP_W // tc)))],
                 (outs, outs))(emb, w1p, b1r, w2f, b2r, w3f, w3b, win)
    circ = jnp.concatenate([hf, jnp.zeros((1, 2 * GROUP_W), F32), jnp.flip(hb[:n - 1], 0)], axis=0)
    return jnp.stack([circ, jnp.zeros_like(circ)]).astype(BF16)


def hyena_spectrum(n, w1, b1, w2, b2, w3):
    n_fft = 2 * n
    n1, n2 = _fft_factors(n_fft)
    circ = hyena_filters(n, w1, b1, w2, b2, w3)
    f1, tw, f2 = _dft_tables(n_fft, n1, n2, inverse=False)
    return fft_stage2(fft_stage1(circ, f1, tw, n_fft), f2, n_fft, "plain")


def _shortconv_kernel(p_ref, w_ref, b_ref, o_ref):
    p = p_ref[...]
    n = p.shape[0]
    row = lax.broadcasted_iota(jnp.int32, p.shape, 0)
    prev = jnp.where(row >= 1, pltpu.roll(p, 1, axis=0), 0.0)
    nxt = jnp.where(row < n - 1, pltpu.roll(p, n - 1, axis=0), 0.0)
    o_ref[...] = (prev * w_ref[0:1, :] + p * w_ref[1:2, :] + nxt * w_ref[2:3, :] + b_ref[...]).astype(o_ref.dtype)


def hyena_shortconv(proj, w, bias):
    b, n, _ = proj.shape
    return _pc(_shortconv_kernel, jax.ShapeDtypeStruct((3, 2, n, (b // 2) * GROUP_W), BF16), (b, 3),
               [pl.BlockSpec((None, n, GROUP_W), lambda bi, j: (bi, 0, PB_HY + j)),
                pl.BlockSpec((3, GROUP_W), lambda bi, j: (0, j)),
                pl.BlockSpec((1, GROUP_W), lambda bi, j: (0, j))],
               pl.BlockSpec((None, None, n, GROUP_W), lambda bi, j: (j, bi % 2, 0, bi // 2)))(proj, w, bias)


def hyena_mixer(proj, spec, conv_w, conv_b, bias):
    b, n, _ = proj.shape
    n_fft = 2 * n
    n1, n2 = _fft_factors(n_fft)
    pc = (b // 2) * GROUP_W
    u = hyena_shortconv(proj, conv_w, conv_b)
    f1, tw, f2 = _dft_tables(n_fft, n1 // 2, n2, inverse=False)
    g1, twi, g2 = _dft_tables(n_fft, n1, n2 // 2, inverse=True)
    z = u[0]
    for o in range(2):
        zf = fft_stage2(fft_stage1(z, f1, tw, n_fft), f2, n_fft, "filter", (spec,), order=o, n_orders=2)
        z = fft_stage2(fft_stage1(zf, g1, twi, n_fft), g2, n_fft, "final", (z, u[1 + o], bias[o:o + 1]))
    return z


def _rot_cols(w):
    return jnp.concatenate([-w[..., 16:32], w[..., 0:16], -w[..., 48:64], w[..., 32:48]], axis=-1)


def _rot_perm(g):
    return jnp.concatenate([g[16:32], g[0:16], g[48:64], g[32:48]])


def _prep_layer(w_in, w_uq, w_ukv, q_a_g, kv_a_g, q_g, k_g):
    d = w_in.shape[0]
    ckv, krope, rest = w_in[:, :256], w_in[:, 256:320], w_in[:, 320:]
    w_in_p = jnp.concatenate([rest, ckv, krope, _rot_cols(krope), jnp.zeros((d, 128), w_in.dtype)], axis=1)
    wq = w_uq.reshape(GROUP_W, MLA_HEADS, MLA_QK)
    wq = jnp.concatenate([wq, _rot_cols(wq[..., MLA_NOPE:])], axis=-1).reshape(GROUP_W, MLA_HEADS * 256)
    r2 = lambda a: a.reshape(1, -1).astype(F32)
    mla = dict(wkv=w_ukv.astype(BF16), wq=wq.astype(BF16), gkv=r2(kv_a_g), gqa=r2(q_a_g),
               gkn=r2(k_g[:MLA_NOPE]), gkt=r2(jnp.concatenate([k_g[MLA_NOPE:], _rot_perm(k_g[MLA_NOPE:])])),
               gqn=r2(q_g[:MLA_NOPE]), gqt=r2(jnp.concatenate([q_g[MLA_NOPE:], _rot_perm(q_g[MLA_NOPE:])])))
    return w_in_p.astype(BF16), mla


def _rope_table(n_rows):
    row = np.repeat(np.arange(n_rows, dtype=np.float32), GRID_W)
    col = np.tile(np.arange(GRID_W, dtype=np.float32), n_rows)
    n_freq = MLA_ROPE // 4
    inv_freq = (ROPE_BASE ** (-np.arange(n_freq, dtype=np.float32) / n_freq)).astype(np.float32)
    ang_r, ang_c = row[:, None] * inv_freq, col[:, None] * inv_freq
    ang = np.concatenate([ang_r, ang_r, ang_c, ang_c], axis=-1)
    return jnp.asarray(np.concatenate([np.cos(ang), np.sin(ang)], axis=-1), F32)


def kernel(x, c, ctx, c_ctx, w_mod, b_mod, norm1, norm2, w_in, w_out, mla_q_a_norm, mla_kv_a_norm, mla_w_uq, mla_w_ukv, mla_q_norm, mla_k_norm, s5_lam_re, s5_lam_im, s5_log_step, s5_b_re, s5_b_im, s5_c_re, s5_c_im, s5_d, s5_w_glu, s5_b_glu, hg_lower_bounds, hg_o_norm, hy_conv_w, hy_conv_b, hy_w1, hy_b1, hy_w2, hy_b2, hy_w3, hy_bias, mlp_w1, mlp_w2):
    bsz, seq, dm = x.shape
    n_ctx = ctx.shape[1]
    depth = w_mod.shape[0]
    assert bsz % 2 == 0 and bsz <= 8 and seq % GRID_W == 0 and S5_T * S5_N == GROUP_W
    r2 = lambda a: a.reshape(1, -1).astype(F32)

    cs_l = _rope_table(seq // GRID_W)
    cs_c = jnp.concatenate([jnp.ones((n_ctx, MLA_ROPE), F32), jnp.zeros((n_ctx, MLA_ROPE), F32)], axis=-1)
    sm = jax.nn.softmax(hg_lower_bounds.astype(F32), axis=1)
    lower = jnp.clip(jnp.cumsum(sm, axis=1) - sm[:, :1], 0.0, 1.0)
    c8 = jnp.concatenate([c.astype(F32), c_ctx.astype(F32)[None], jnp.zeros((7 - bsz, dm), F32)], axis=0)
    zero_state = jnp.zeros((bsz, HG_HEADS, HG_D, HG_D), F32)
    zero_x = jnp.zeros((S5_G, 8, 256), F32)

    xc = ctx
    for l in range(depth):
        last = l == depth - 1
        mod = modvec(c8, w_mod[l].astype(BF16), r2(b_mod[l]))
        mod_l = [mod[:bsz, i * dm:(i + 1) * dm][:, None, :] for i in range(N_MOD)]
        mod_c = [mod[bsz:bsz + 1, i * dm:(i + 1) * dm][:, None, :] for i in range(N_MOD)]
        w_in_p, mla_p = _prep_layer(w_in[l], mla_w_uq[l], mla_w_ukv[l], mla_q_a_norm[l], mla_kv_a_norm[l],
                                    mla_q_norm[l], mla_k_norm[l])
        pl_ = modproj(x, mod_l[0], mod_l[1], r2(norm1[l]), w_in_p, F32)
        pc_ = modproj(xc, mod_c[0], mod_c[1], r2(norm1[l]), w_in_p, F32)

        k_c, v_c, q_c = mla_prep(pc_, cs_c, mla_p, with_q=not last)
        k_l, v_l, q_l = mla_prep(pl_, cs_l, mla_p, with_q=True)
        y_a = attention(q_l, [(k_l, v_l), (k_c, v_c)])

        tabs = s5_tables(s5_lam_re[l], s5_lam_im[l], s5_log_step[l], s5_b_re[l], s5_b_im[l], s5_c_re[l], s5_c_im[l])
        yc_s5, fin = s5_scan(pc_[..., PB_S5U * GROUP_W:(PB_S5U + 1) * GROUP_W], zero_x, tabs)
        yl_s5, _ = s5_scan(pl_[..., PB_S5U * GROUP_W:(PB_S5U + 1) * GROUP_W], fin, tabs)
        glu_w, glu_b = s5_w_glu[l].astype(BF16), r2(s5_b_glu[l])
        y_b = s5_out(yl_s5, pl_, r2(s5_d[l]), glu_w, glu_b)

        la_f, lb_f = r2(jnp.log(lower[0, l])), r2(jnp.log1p(-lower[0, l]))
        la_b, lb_b = r2(jnp.log(lower[1, l])), r2(jnp.log1p(-lower[1, l]))
        oc_f, s_f = hgrn_dir(pc_, la_f, lb_f, zero_state, reverse=False)
        oc_b, s_b = hgrn_dir(pc_, la_b, lb_b, zero_state, reverse=True)
        ol_f, _ = hgrn_dir(pl_, la_f, lb_f, s_f, reverse=False)
        ol_b, _ = hgrn_dir(pl_, la_b, lb_b, s_b, reverse=True)
        y_c = hg_out(ol_f, ol_b, pl_, r2(hg_o_norm[l]))

        hy_args = (hy_conv_w[l].astype(F32), r2(hy_conv_b[l]), hy_bias[l].astype(F32))
        spec_l = hyena_spectrum(seq, hy_w1[l], hy_b1[l], hy_w2[l], hy_b2[l], hy_w3[l])
        y_d = hyena_mixer(pl_, spec_l, *hy_args)

        w_out_b = w_out[l].astype(BF16)
        w1_b, w2_b = mlp_w1[l].astype(BF16), mlp_w2[l].astype(BF16)
        x = mix_out([y_a, y_b, y_c], y_d, w_out_b, x, mod_l[2])
        h1 = modproj(x, mod_l[3], mod_l[4], r2(norm2[l]), w1_b, BF16, act="relu2", tm=1024)
        x = mlp_out(h1, w2_b, x, mod_l[5])

        if not last:
            y_ac = attention(q_c, [(k_c, v_c)])
            y_bc = s5_out(yc_s5, pc_, r2(s5_d[l]), glu_w, glu_b)
            y_cc = hg_out(oc_f, oc_b, pc_, r2(hg_o_norm[l]))
            spec_c = hyena_spectrum(n_ctx, hy_w1[l], hy_b1[l], hy_w2[l], hy_b2[l], hy_w3[l])
            y_dc = hyena_mixer(pc_, spec_c, *hy_args)
            xc = mix_out([y_ac, y_bc, y_cc], y_dc, w_out_b, xc, mod_c[2])
            h1c = modproj(xc, mod_c[3], mod_c[4], r2(norm2[l]), w1_b, BF16, act="relu2", tm=1024)
            xc = mlp_out(h1c, w2_b, xc, mod_c[5])
    return x
```

```python
import functools
import math

import numpy as np
import jax
import jax.numpy as jnp
from jax import lax
from jax.experimental import pallas as pl
from jax.experimental.pallas import tpu as pltpu

F32 = jnp.float32
BF16 = jnp.bfloat16
EPS = 1e-6

GRID_W = 64
GROUP_W = 512
N_MOD = 6
MLA_NOPE = 128
MLA_ROPE = 64
MLA_QK = MLA_NOPE + MLA_ROPE
MLA_HEADS = 4
KV_LORA = 256
MLA_SCALE = 1.0 / math.sqrt(MLA_QK)
ROPE_BASE = 10000.0
S5_G = 32
S5_N = 16
S5_P = 64
S5_T = 32
HG_HEADS = 4
HG_D = 128
HG_SUB = 16
HY_EMB = 33
HY_BANDS = 16
HY_HIDDEN = 64
HY_TARGET = 1e-2
HY_FAST = 0.3
HY_SLOW = 1.5
HY_SHIFT = 0.05

LANES = 128
VMEM_LIMIT_MB = 56

PB_S5U, PB_HGF, PB_HGB, PB_HGI, PB_CQ, PB_HGQ, PB_HGG, PB_HY, PB_KV = 0, 1, 2, 3, 4, 5, 6, 7, 10
N_PROJ = 11 * GROUP_W


def _pc(kernel, out_shape, grid, in_specs, out_specs, scratch=(), vmem_mb=VMEM_LIMIT_MB, name=None):
    return pl.pallas_call(
        kernel, out_shape=out_shape, grid=grid, in_specs=in_specs, out_specs=out_specs,
        scratch_shapes=list(scratch), name=name,
        compiler_params=pltpu.CompilerParams(
            dimension_semantics=("arbitrary",) * len(grid), vmem_limit_bytes=vmem_mb << 20))


def _tile(n, pref):
    t = min(n, pref)
    while n % t:
        t //= 2
    return t


def _silu(x):
    return x * jax.nn.sigmoid(x)


def _modvec_kernel(c_ref, w_ref, b_ref, o_ref):
    s = _silu(c_ref[...])
    o_ref[...] = jnp.dot(s.astype(BF16), w_ref[...].astype(BF16), preferred_element_type=F32) + b_ref[...]


def modvec(c8, w, b):
    d, n = w.shape
    tn = _tile(n, 1536)
    return _pc(_modvec_kernel, jax.ShapeDtypeStruct((8, n), F32), (n // tn,),
               [pl.BlockSpec((8, d), lambda j: (0, 0)),
                pl.BlockSpec((d, tn), lambda j: (0, j)),
                pl.BlockSpec((1, tn), lambda j: (0, j))],
               pl.BlockSpec((8, tn), lambda j: (0, j)), name="modvec")(c8, w, b)


def _modproj_kernel(x_ref, sh_ref, sc_ref, g_ref, w_ref, o_ref, h_ref, *, act):
    @pl.when(pl.program_id(2) == 0)
    def _():
        x = x_ref[...]
        ms = jnp.mean(x * x, axis=-1, keepdims=True)
        y = x * lax.rsqrt(ms + EPS) * g_ref[...]
        h_ref[...] = (y * (1.0 + sc_ref[...]) + sh_ref[...]).astype(BF16)

    acc = jnp.dot(h_ref[...], w_ref[...], preferred_element_type=F32)
    if act == "relu2":
        acc = jnp.square(jnp.maximum(acc, 0.0))
    o_ref[...] = acc.astype(o_ref.dtype)


def modproj(x, shift, scale, g, w, out_dtype, act=None, tm=512, tn=512):
    b, l, d = x.shape
    n = w.shape[1]
    tm, tn = _tile(l, tm), _tile(n, tn)
    per_b = shift.shape[0] == b
    mod_map = (lambda bi, i, j: (bi, 0, 0)) if per_b else (lambda bi, i, j: (0, 0, 0))
    return _pc(functools.partial(_modproj_kernel, act=act),
               jax.ShapeDtypeStruct((b, l, n), out_dtype), (b, l // tm, n // tn),
               [pl.BlockSpec((None, tm, d), lambda bi, i, j: (bi, i, 0)),
                pl.BlockSpec((None, 1, d), mod_map),
                pl.BlockSpec((None, 1, d), mod_map),
                pl.BlockSpec((1, d), lambda bi, i, j: (0, 0)),
                pl.BlockSpec((d, tn), lambda bi, i, j: (0, j))],
               pl.BlockSpec((None, tm, tn), lambda bi, i, j: (bi, i, j)),
               scratch=[pltpu.VMEM((tm, d), BF16)], name="modproj_" + (act or "lin"))(x, shift, scale, g, w)


def _mmres_kernel(*refs, n_lhs, kw):
    a_refs = refs[:n_lhs]
    w_ref, res_ref, gate_ref, o_ref = refs[n_lhs:]
    acc = None
    for i, a in enumerate(a_refs):
        d = jnp.dot(a[...], w_ref[i * kw:(i + 1) * kw, :], preferred_element_type=F32)
        acc = d if acc is None else acc + d
    o_ref[...] = res_ref[...] + gate_ref[...] * acc


def mix_out(ys, yd_hlay, w, res, gate, tm=512, tn=1024):
    b, l, kw = ys[0].shape
    n = w.shape[1]
    tm, tn = _tile(l, tm), _tile(n, tn)
    per_b = gate.shape[0] == b
    gmap = (lambda bi, i, j: (bi, 0, j)) if per_b else (lambda bi, i, j: (0, 0, j))
    y_spec = pl.BlockSpec((None, tm, kw), lambda bi, i, j: (bi, i, 0))
    yd_spec = pl.BlockSpec((None, tm, kw), lambda bi, i, j: (bi % 2, i, bi // 2))
    return _pc(functools.partial(_mmres_kernel, n_lhs=4, kw=kw),
               jax.ShapeDtypeStruct(res.shape, F32), (b, l // tm, n // tn),
               [y_spec, y_spec, y_spec, yd_spec,
                pl.BlockSpec((4 * kw, tn), lambda bi, i, j: (0, j)),
                pl.BlockSpec((None, tm, tn), lambda bi, i, j: (bi, i, j)),
                pl.BlockSpec((None, 1, tn), gmap)],
               pl.BlockSpec((None, tm, tn), lambda bi, i, j: (bi, i, j)), name="mix_out")(*ys, yd_hlay, w, res, gate)


def _mmresk_kernel(a_ref, w_ref, res_ref, gate_ref, o_ref, acc_ref):
    k = pl.program_id(3)

    @pl.when(k == 0)
    def _():
        acc_ref[...] = jnp.zeros_like(acc_ref)

    acc_ref[...] += jnp.dot(a_ref[...], w_ref[...], preferred_element_type=F32)

    @pl.when(k == pl.num_programs(3) - 1)
    def _():
        o_ref[...] = res_ref[...] + gate_ref[...] * acc_ref[...]


def mlp_out(a, w, res, gate, tm=1024, tn=1024, tk=1024):
    b, l, kk = a.shape
    n = w.shape[1]
    tm, tn, tk = _tile(l, tm), _tile(n, tn), _tile(kk, tk)
    per_b = gate.shape[0] == b
    gmap = (lambda bi, i, j, k: (bi, 0, j)) if per_b else (lambda bi, i, j, k: (0, 0, j))
    return _pc(_mmresk_kernel, jax.ShapeDtypeStruct(res.shape, F32), (b, l // tm, n // tn, kk // tk),
               [pl.BlockSpec((None, tm, tk), lambda bi, i, j, k: (bi, i, k)),
                pl.BlockSpec((tk, tn), lambda bi, i, j, k: (k, j)),
                pl.BlockSpec((None, tm, tn), lambda bi, i, j, k: (bi, i, j)),
                pl.BlockSpec((None, 1, tn), gmap)],
               pl.BlockSpec((None, tm, tn), lambda bi, i, j, k: (bi, i, j)),
               scratch=[pltpu.VMEM((tm, tn), F32)], name="mlp_out")(a, w, res, gate)


def _mla_prep_kernel(kv_ref, cq_ref, cs_ref, gkv_ref, gqa_ref, gkn_ref, gkt_ref, gqn_ref, gqt_ref,
                     wkv_ref, wq_ref, k_ref, v_ref, q_ref):
    lane = lax.broadcasted_iota(jnp.int32, (1, LANES), 1)
    low = (lane < MLA_ROPE).astype(F32)
    cs = cs_ref[...]
    kv = kv_ref[...]
    ckv = kv[:, :KV_LORA]
    ckvn = ckv * lax.rsqrt(jnp.mean(ckv * ckv, axis=-1, keepdims=True) + EPS) * gkv_ref[...]
    kvp = jnp.dot(ckvn.astype(BF16), wkv_ref[...], preferred_element_type=F32)
    pe2 = kv[:, KV_LORA:KV_LORA + LANES]
    ss_pe = jnp.sum(pe2 * pe2 * low, axis=-1, keepdims=True)
    pe_tab = cs * gkt_ref[...]
    for h in range(MLA_HEADS):
        knope = kvp[:, 256 * h:256 * h + 128]
        rs = lax.rsqrt((jnp.sum(knope * knope, axis=-1, keepdims=True) + ss_pe) * (1.0 / MLA_QK) + EPS)
        t = pe2 * pe_tab * rs
        t = (t + pltpu.roll(t, MLA_ROPE, axis=1)) * low
        k_ref[h, :, 0:128] = (knope * gkn_ref[...] * rs).astype(BF16)
        k_ref[h, :, 128:256] = t.astype(BF16)
        v_ref[h] = kvp[:, 256 * h + 128:256 * h + 256].astype(BF16)
    if q_ref is not None:
        cq = cq_ref[...]
        cqn = cq * lax.rsqrt(jnp.mean(cq * cq, axis=-1, keepdims=True) + EPS) * gqa_ref[...]
        qp = jnp.dot(cqn.astype(BF16), wq_ref[...], preferred_element_type=F32)
        q_tab = cs * gqt_ref[...]
        for h in range(MLA_HEADS):
            qn = qp[:, 256 * h:256 * h + 128]
            qe = qp[:, 256 * h + 128:256 * h + 256]
            ss = jnp.sum(qn * qn, axis=-1, keepdims=True) + jnp.sum(qe * qe * low, axis=-1, keepdims=True)
            rs = lax.rsqrt(ss * (1.0 / MLA_QK) + EPS) * MLA_SCALE
            t = qe * q_tab * rs
            t = t + pltpu.roll(t, MLA_ROPE, axis=1)
            q_ref[h, :, 0:128] = (qn * gqn_ref[...] * rs).astype(BF16)
            q_ref[h, :, 128:256] = t.astype(BF16)


def _mla_prep_kernel_noq(kv_ref, cs_ref, gkv_ref, gkn_ref, gkt_ref, wkv_ref, k_ref, v_ref):
    _mla_prep_kernel(kv_ref, None, cs_ref, gkv_ref, None, gkn_ref, gkt_ref, None, None, wkv_ref, None,
                     k_ref, v_ref, None)


def mla_prep(proj, cs, p, with_q, tm=512):
    b, l, _ = proj.shape
    tm = _tile(l, tm)
    row = lambda w: pl.BlockSpec((1, w), lambda bi, i: (0, 0))
    kv_spec = pl.BlockSpec((None, tm, GROUP_W), lambda bi, i: (bi, i, PB_KV))
    cq_spec = pl.BlockSpec((None, tm, GROUP_W), lambda bi, i: (bi, i, PB_CQ))
    cs_spec = pl.BlockSpec((tm, LANES), lambda bi, i: (i, 0))
    wkv_spec = pl.BlockSpec((KV_LORA, 1024), lambda bi, i: (0, 0))
    wq_spec = pl.BlockSpec((GROUP_W, 1024), lambda bi, i: (0, 0))
    hd = lambda w: pl.BlockSpec((None, MLA_HEADS, tm, w), lambda bi, i: (bi, 0, i, 0))
    k_sh = jax.ShapeDtypeStruct((b, MLA_HEADS, l, 256), BF16)
    v_sh = jax.ShapeDtypeStruct((b, MLA_HEADS, l, 128), BF16)
    if with_q:
        return _pc(_mla_prep_kernel, (k_sh, v_sh, k_sh), (b, l // tm),
                   [kv_spec, cq_spec, cs_spec, row(KV_LORA), row(GROUP_W), row(128), row(128), row(128), row(128),
                    wkv_spec, wq_spec],
                   (hd(256), hd(128), hd(256)), name="mla_prep")(
            proj, proj, cs, p["gkv"], p["gqa"], p["gkn"], p["gkt"], p["gqn"], p["gqt"], p["wkv"], p["wq"])
    k, v = _pc(_mla_prep_kernel_noq, (k_sh, v_sh), (b, l // tm),
               [kv_spec, cs_spec, row(KV_LORA), row(128), row(128), wkv_spec],
               (hd(256), hd(128)), name="mla_prep_kv")(proj, cs, p["gkv"], p["gkn"], p["gkt"], p["wkv"])
    return k, v, None


def _attn_kernel(*refs, n_seg):
    q = refs[0][...]
    o_ref = refs[1 + 2 * n_seg]
    scores = [lax.dot_general(q, refs[1 + 2 * i][...], (((1,), (1,)), ((), ())), preferred_element_type=F32)
              for i in range(n_seg)]
    m = scores[0].max(axis=-1, keepdims=True)
    for s in scores[1:]:
        m = jnp.maximum(m, s.max(axis=-1, keepdims=True))
    den, acc = None, None
    for i, s in enumerate(scores):
        p = jnp.exp(s - m)
        d = p.sum(axis=-1, keepdims=True)
        a = jnp.dot(p.astype(BF16), refs[2 + 2 * i][...], preferred_element_type=F32)
        den = d if den is None else den + d
        acc = a if acc is None else acc + a
    o_ref[...] = (acc / den).astype(o_ref.dtype)


def attention(q, kvs, tq=256):
    b, h, l, _ = q.shape
    tq = _tile(l, tq)
    specs = [pl.BlockSpec((None, None, tq, 256), lambda bi, hi, i: (bi, hi, i, 0))]
    args = [q]
    for k, v in kvs:
        lk = k.shape[2]
        specs += [pl.BlockSpec((None, None, lk, 256), lambda bi, hi, i: (bi, hi, 0, 0)),
                  pl.BlockSpec((None, None, lk, 128), lambda bi, hi, i: (bi, hi, 0, 0))]
        args += [k, v]
    return _pc(functools.partial(_attn_kernel, n_seg=len(kvs)),
               jax.ShapeDtypeStruct((b, l, h * 128), BF16), (b, h, l // tq), specs,
               pl.BlockSpec((None, tq, 128), lambda bi, hi, i: (bi, i, hi)), name="attention")(*args)


def s5_tables(lam_re, lam_im, log_step, b_re, b_im, c_re, c_im):
    t = S5_T
    step = jnp.exp(log_step.astype(F32))[..., None]
    re, im = lam_re.astype(F32) * step, lam_im.astype(F32) * step
    d = jnp.arange(t + 1, dtype=F32)[:, None, None, None]
    mag = jnp.exp(d * re)
    ar, ai = mag * jnp.cos(d * im), mag * jnp.sin(d * im)
    lr, li = lam_re.astype(F32), lam_im.astype(F32)
    den = lr * lr + li * li
    er, ei = ar[1] - 1.0, ai[1]
    cr, ci = (er * lr + ei * li) / den, (ei * lr - er * li) / den
    br, bi = b_re.astype(F32), b_im.astype(F32)
    bbr = cr[..., None] * br - ci[..., None] * bi
    bbi = cr[..., None] * bi + ci[..., None] * br
    ccr, cci = c_re.astype(F32), c_im.astype(F32)
    car = ccr[None] * ar[:, :, :, None, :] - cci[None] * ai[:, :, :, None, :]
    cai = ccr[None] * ai[:, :, :, None, :] + cci[None] * ar[:, :, :, None, :]
    kd = jnp.einsum("dxgnp,xgpm->xgdnm", car[:t], bbr) - jnp.einsum("dxgnp,xgpm->xgdnm", cai[:t], bbi)
    kcat = jnp.stack([kd[0], kd[1][:, ::-1]], axis=1).transpose(0, 1, 4, 2, 3).reshape(S5_G, 2, S5_N, t * S5_N)
    pw = (jnp.arange(t - 1, -1, -1), jnp.arange(t))
    w_re, w_im = [], []
    for x in range(2):
        a_r, a_i = ar[pw[x], x], ai[pw[x], x]
        w_re.append(a_r[..., None] * bbr[x][None] - a_i[..., None] * bbi[x][None])
        w_im.append(a_r[..., None] * bbi[x][None] + a_i[..., None] * bbr[x][None])
    ws = jnp.concatenate(w_re + w_im, axis=2).transpose(1, 0, 3, 2).reshape(S5_G, t * S5_N, 4 * S5_P)
    pv = (jnp.arange(1, t + 1), jnp.arange(t, 0, -1))
    vm = jnp.concatenate([car[pv[0], 0], car[pv[1], 1], -cai[pv[0], 0], -cai[pv[1], 1]], axis=-1)
    vm = vm.transpose(1, 3, 0, 2).reshape(S5_G, 4 * S5_P, t * S5_N)
    dec = jnp.stack([jnp.concatenate([ar[t, 0], ar[t, 1]], axis=-1),
                     jnp.concatenate([ai[t, 0], ai[t, 1]], axis=-1)], axis=1)
    return kcat, ws.astype(BF16), vm.astype(BF16), dec


def _s5a_kernel(u_ref, k_ref, ws_ref, y_ref, s_ref, m_ref):
    kf, kb = k_ref[0], k_ref[1]
    lane = lax.broadcasted_iota(jnp.int32, kf.shape, 1)
    for s in range(S5_T):
        lo, hi = S5_N * s, S5_N * (s + 1)
        f = kf if s == 0 else jnp.where(lane >= lo, pltpu.roll(kf, lo, axis=1), 0.0)
        bk = kb if s == S5_T - 1 else jnp.where(lane < hi, pltpu.roll(kb, hi, axis=1), 0.0)
        m_ref[lo:hi, :] = (f + bk).astype(BF16)
    u = u_ref[...]
    y_ref[...] = jnp.dot(u, m_ref[...], preferred_element_type=F32)
    s_ref[...] = jnp.dot(u, ws_ref[...], preferred_element_type=F32)


def _s5b_kernel(yi_ref, s_ref, x0_ref, dec_ref, vm_ref, y_ref, xf_ref, xa_ref, xb_ref, *, nchunk):
    ar = dec_ref[0:1, :]
    ai = dec_ref[1:2, :]
    fwd_lane = lax.broadcasted_iota(jnp.int32, (8, 128), 1) < S5_P

    def step(j, carry):
        xr, xi = carry
        rf = pl.ds(pl.multiple_of(j * 8, 8), 8)
        rb = pl.ds(pl.multiple_of((nchunk - 1 - j) * 8, 8), 8)
        x = jnp.concatenate([xr, xi], axis=1)
        xa_ref[rf, :] = x
        xb_ref[rb, :] = x
        sf, sb = s_ref[rf, :], s_ref[rb, :]
        sr = jnp.where(fwd_lane, sf[:, :128], sb[:, :128])
        si = jnp.where(fwd_lane, sf[:, 128:], sb[:, 128:])
        return ar * xr - ai * xi + sr, ar * xi + ai * xr + si

    x0 = x0_ref[...]
    xr, xi = lax.fori_loop(0, nchunk, step, (x0[:, :128], x0[:, 128:]), unroll=4)
    xf_ref[...] = jnp.concatenate([xr, xi], axis=1)
    pick = (lax.broadcasted_iota(jnp.int32, xa_ref.shape, 1) % 128) < S5_P
    xin = jnp.where(pick, xa_ref[...], xb_ref[...])
    y_ref[...] = yi_ref[...] + jnp.dot(xin.astype(BF16), vm_ref[...], preferred_element_type=F32)


def s5_scan(u, x0, tabs):
    kcat, ws, vm, dec = tabs
    b, l, _ = u.shape
    j = l // S5_T
    r = j * 8
    ug = u.astype(BF16).reshape(b, j, S5_T, S5_G, S5_N).transpose(3, 1, 0, 2, 4)
    ug = jnp.pad(ug, ((0, 0), (0, 0), (0, 8 - b), (0, 0), (0, 0))).reshape(S5_G, r, GROUP_W)
    g3 = lambda w: pl.BlockSpec((None, r, w), lambda g: (g, 0, 0))
    yi, s = _pc(_s5a_kernel,
                (jax.ShapeDtypeStruct((S5_G, r, GROUP_W), F32), jax.ShapeDtypeStruct((S5_G, r, 256), F32)),
                (S5_G,),
                [g3(GROUP_W), pl.BlockSpec((None, 2, S5_N, GROUP_W), lambda g: (g, 0, 0, 0)),
                 pl.BlockSpec((None, GROUP_W, 256), lambda g: (g, 0, 0))],
                (g3(GROUP_W), g3(256)),
                scratch=[pltpu.VMEM((GROUP_W, GROUP_W), BF16)], name="s5a")(ug, kcat, ws)
    y, xf = _pc(functools.partial(_s5b_kernel, nchunk=j),
                (jax.ShapeDtypeStruct((S5_G, r, GROUP_W), F32), jax.ShapeDtypeStruct((S5_G, 8, 256), F32)),
                (S5_G,),
                [g3(GROUP_W), g3(256),
                 pl.BlockSpec((None, 8, 256), lambda g: (g, 0, 0)),
                 pl.BlockSpec((None, 2, 128), lambda g: (g, 0, 0)),
                 pl.BlockSpec((None, 256, GROUP_W), lambda g: (g, 0, 0))],
                (g3(GROUP_W), pl.BlockSpec((None, 8, 256), lambda g: (g, 0, 0))),
                scratch=[pltpu.VMEM((r, 256), F32), pltpu.VMEM((r, 256), F32)], name="s5b")(yi, s, x0, dec, vm)
    y = y.reshape(S5_G, j, 8, S5_T, S5_N)[:, :, :b].transpose(2, 1, 3, 0, 4).reshape(b, l, GROUP_W)
    return y, xf


def _s5_out_kernel(y_ref, u_ref, d_ref, w_ref, b_ref, o_ref):
    y = jax.nn.gelu(y_ref[...] + d_ref[...] * u_ref[...])
    z = jnp.dot(y.astype(BF16), w_ref[...], preferred_element_type=F32) + b_ref[...]
    o_ref[...] = (y * jax.nn.sigmoid(z)).astype(o_ref.dtype)


def s5_out(y, proj, d, w, bias, tm=512):
    b, l, _ = y.shape
    tm = _tile(l, tm)
    row = pl.BlockSpec((1, GROUP_W), lambda bi, i: (0, 0))
    return _pc(_s5_out_kernel, jax.ShapeDtypeStruct((b, l, GROUP_W), BF16), (b, l // tm),
               [pl.BlockSpec((None, tm, GROUP_W), lambda bi, i: (bi, i, 0)),
                pl.BlockSpec((None, tm, GROUP_W), lambda bi, i: (bi, i, PB_S5U)),
                row, pl.BlockSpec((GROUP_W, GROUP_W), lambda bi, i: (0, 0)), row],
               pl.BlockSpec((None, tm, GROUP_W), lambda bi, i: (bi, i, 0)), name="s5_out")(y, proj, d, w, bias)


def _hgrn_kernel(z_ref, i_ref, q_ref, la_ref, lb_ref, s0_ref, o_ref, sfin_ref, st_ref, *, reverse, nsub):
    c = HG_SUB

    @pl.when(pl.program_id(1) == 0)
    def _():
        st_ref[...] = s0_ref[...]

    hlf = c // 2
    ones = jnp.ones((HG_D, HG_D), BF16)
    row = lax.broadcasted_iota(jnp.int32, (c, HG_D), 0)
    row8 = lax.broadcasted_iota(jnp.int32, (hlf, HG_D), 0)

    def sub(jj, carry):
        j = (nsub - 1 - jj) if reverse else jj
        rows = pl.ds(pl.multiple_of(j * c, c), c)
        for h in range(HG_HEADS):
            sl = slice(HG_D * h, HG_D * (h + 1))
            z = z_ref[rows, sl]
            v = i_ref[rows, sl]
            q = _silu(q_ref[rows, sl])
            a = la_ref[:, sl]
            bb = lb_ref[:, sl] + (jnp.minimum(z, 0.0) - jnp.log1p(jnp.exp(-jnp.abs(z))))
            g = jnp.maximum(a, bb) + jnp.log1p(jnp.exp(-jnp.abs(a - bb)))
            k = 1.0 - jnp.exp(g)
            gc = g
            for sh in (1, 2, 4, 8):
                if reverse:
                    gc = gc + jnp.where(row < c - sh, pltpu.roll(gc, c - sh, axis=0), 0.0)
                else:
                    gc = gc + jnp.where(row >= sh, pltpu.roll(gc, sh, axis=0), 0.0)
            blocks, meta = [], []
            for s in range(c):
                for half in range(2):
                    r0 = half * hlf
                    if (r0 > s) if reverse else (r0 + hlf - 1 < s):
                        continue
                    e = gc[r0:r0 + hlf, :] - gc[s:s + 1, :]
                    if not ((r0 + hlf - 1 <= s) if reverse else (r0 >= s)):
                        keep = (row8 + r0 <= s) if reverse else (row8 + r0 >= s)
                        e = jnp.where(keep, e, -1e30)
                    blocks.append(q[r0:r0 + hlf, :] * jnp.exp(e) * k[s:s + 1, :])
                    meta.append((half, s))
            rsum = jnp.dot(jnp.concatenate(blocks, axis=0).astype(BF16), ones, preferred_element_type=F32)
            halves = [jnp.zeros((hlf, HG_D), F32), jnp.zeros((hlf, HG_D), F32)]
            for i, (half, s) in enumerate(meta):
                halves[half] = halves[half] + rsum[hlf * i:hlf * (i + 1), :] * v[s:s + 1, :]
            o = jnp.concatenate(halves, axis=0)
            st = st_ref[h]
            o = o + lax.dot_general((q * jnp.exp(gc)).astype(BF16), st.astype(BF16),
                                    (((1,), (1,)), ((), ())), preferred_element_type=F32)
            gl = gc[0:1, :] if reverse else gc[c - 1:c, :]
            kd = k * jnp.exp(gl - gc)
            ds = lax.dot_general(v.astype(BF16), kd.astype(BF16), (((0,), (0,)), ((), ())),
                                 preferred_element_type=F32)
            st_ref[h] = st * jnp.exp(gl) + ds
            o_ref[rows, sl] = o
        return carry

    lax.fori_loop(0, nsub, sub, 0)

    @pl.when(pl.program_id(1) == pl.num_programs(1) - 1)
    def _():
        sfin_ref[...] = st_ref[...]


def hgrn_dir(proj, la, lb1, s0, reverse, tl=256):
    b, l, _ = proj.shape
    tl = _tile(l, tl)
    nb = l // tl
    blk = (lambda i: nb - 1 - i) if reverse else (lambda i: i)
    col = lambda c: pl.BlockSpec((None, tl, GROUP_W), lambda bi, i: (bi, blk(i), c))
    row = pl.BlockSpec((1, GROUP_W), lambda bi, i: (0, 0))
    st = pl.BlockSpec((None, HG_HEADS, HG_D, HG_D), lambda bi, i: (bi, 0, 0, 0))
    return _pc(functools.partial(_hgrn_kernel, reverse=reverse, nsub=tl // HG_SUB),
               (jax.ShapeDtypeStruct((b, l, GROUP_W), F32), jax.ShapeDtypeStruct(s0.shape, F32)),
               (b, nb),
               [col(PB_HGB if reverse else PB_HGF), col(PB_HGI), col(PB_HGQ), row, row, st],
               (pl.BlockSpec((None, tl, GROUP_W), lambda bi, i: (bi, blk(i), 0)), st),
               scratch=[pltpu.VMEM((HG_HEADS, HG_D, HG_D), F32)],
               name="hgrn_bwd" if reverse else "hgrn_fwd")(proj, proj, proj, la, lb1, s0)


def _hg_out_kernel(of_ref, ob_ref, g_ref, gn_ref, y_ref):
    o = of_ref[...] + ob_ref[...]
    gate = _silu(g_ref[...])
    for h in range(HG_HEADS):
        sl = slice(HG_D * h, HG_D * (h + 1))
        oh = o[:, sl]
        y = oh * lax.rsqrt(jnp.mean(oh * oh, axis=-1, keepdims=True) + EPS) * gn_ref[...]
        y_ref[:, sl] = (y * gate[:, sl]).astype(y_ref.dtype)


def hg_out(of, ob, proj, gn, tm=512):
    b, l, _ = of.shape
    tm = _tile(l, tm)
    t3 = lambda c: pl.BlockSpec((None, tm, GROUP_W), lambda bi, i: (bi, i, c))
    return _pc(_hg_out_kernel, jax.ShapeDtypeStruct((b, l, GROUP_W), BF16), (b, l // tm),
               [t3(0), t3(0), t3(PB_HGG), pl.BlockSpec((1, HG_D), lambda bi, i: (0, 0))],
               t3(0), name="hg_out")(of, ob, proj, gn)


def _fft_factors(n_fft):
    n1 = {8192: 64, 512: 32, 1024: 32, 2048: 32, 4096: 64}[n_fft]
    return n1, n_fft // n1


def _dft_tables(n_fft, k1, m2, inverse):
    n1, n2 = _fft_factors(n_fft)
    sgn = 1.0 if inverse else -1.0

    def stacked(ang, scale=1.0):
        fr, fi = np.cos(ang) * scale, np.sin(ang) * scale
        return np.block([[fr, -fi], [fi, fr]])

    f1 = stacked(sgn * 2 * np.pi * np.outer(np.arange(n1), np.arange(k1)) / n1)
    f2 = stacked(sgn * 2 * np.pi * np.outer(np.arange(m2), np.arange(n2)) / n2, (1.0 / n_fft) if inverse else 1.0)
    ang = sgn * 2 * np.pi * np.outer(np.arange(n2), np.arange(n1)) / n_fft
    tw = np.stack([np.cos(ang), np.sin(ang)])[..., None] * np.ones((1, 1, 1, LANES))
    return jnp.asarray(f1, BF16), jnp.asarray(tw, F32), jnp.asarray(f2, BF16)


def _fft1_kernel(z_ref, f_ref, tw_ref, o_ref, *, g, n1, is_complex):
    rep = o_ref.shape[-1] // LANES
    f = f_ref[...]
    for i in range(g):
        if is_complex:
            z = z_ref[:, :, i, :]
            z = z.reshape(z.shape[0] * z.shape[1], z.shape[2])
        else:
            z = z_ref[:, i, :]
        a = jnp.dot(f, z, preferred_element_type=F32)
        ar, ai = a[:n1], a[n1:]
        twr, twi = jnp.tile(tw_ref[0, i], (1, rep)), jnp.tile(tw_ref[1, i], (1, rep))
        o_ref[0, :, i, :] = (ar * twr - ai * twi).astype(o_ref.dtype)
        o_ref[1, :, i, :] = (ar * twi + ai * twr).astype(o_ref.dtype)


def fft_stage1(z, f1, tw, n_fft, g=16):
    n1, n2 = _fft_factors(n_fft)
    is_complex = z.ndim == 3
    pc = z.shape[-1]
    k1 = z.shape[-2] // n2
    g = min(g, n2)
    if is_complex:
        zv, zspec = z.reshape(2, k1, n2, pc), pl.BlockSpec((2, k1, g, pc), lambda j: (0, 0, j, 0))
    else:
        zv, zspec = z.reshape(k1, n2, pc), pl.BlockSpec((k1, g, pc), lambda j: (0, j, 0))
    return _pc(functools.partial(_fft1_kernel, g=g, n1=n1, is_complex=is_complex),
               jax.ShapeDtypeStruct((2, n1, n2, pc), BF16), (n2 // g,),
               [zspec, pl.BlockSpec(f1.shape, lambda j: (0, 0)),
                pl.BlockSpec((2, g, n1, LANES), lambda j: (0, j, 0, 0))],
               pl.BlockSpec((2, n1, g, pc), lambda j: (0, 0, j, 0)), name="fft1")(zv, f1, tw)


def _fft2_kernel(*refs, mode, m2, kb):
    a_ref, f_ref = refs[0], refs[1]
    o_ref = refs[-1]
    f = f_ref[...]
    for i in range(kb):
        x = a_ref[:, i]
        z = jnp.dot(f, x.reshape(x.shape[0] * x.shape[1], x.shape[2]), preferred_element_type=F32)
        zr, zi = z[:m2], z[m2:]
        if mode == "filter":
            h_ref = refs[2]
            hr, hi = h_ref[0, :, i, :].astype(F32), h_ref[1, :, i, :].astype(F32)
            zr, zi = zr * hr - zi * hi, zr * hi + zi * hr
        elif mode == "final":
            v_ref, g_ref, b_ref = refs[2], refs[3], refs[4]
            zr = g_ref[0, :, i, :].astype(F32) * (zr + v_ref[0, :, i, :].astype(F32) * b_ref[...])
            zi = g_ref[1, :, i, :].astype(F32) * (zi + v_ref[1, :, i, :].astype(F32) * b_ref[...])
        o_ref[0, :, i, :] = zr.astype(o_ref.dtype)
        o_ref[1, :, i, :] = zi.astype(o_ref.dtype)


def fft_stage2(a, f2, n_fft, mode, extra=(), order=0, kb=16):
    n1, n2 = _fft_factors(n_fft)
    pc = a.shape[3]
    m2 = f2.shape[0] // 2
    c = GROUP_W
    kb = min(kb, n1)
    nat = pl.BlockSpec((2, m2, kb, c), lambda k, p: (0, 0, k, p))
    specs = [pl.BlockSpec((2, kb, n2, c), lambda k, p: (0, k, 0, p)),
             pl.BlockSpec((2 * m2, 2 * n2), lambda k, p: (0, 0))]
    args = [a, f2]
    if mode == "filter":
        (h,) = extra
        specs.append(pl.BlockSpec((2, n2, kb, c), lambda k, p: (0, 0, k, order)))
        args.append(h)
    elif mode == "final":
        v, gate, bias = extra
        specs += [nat, nat, pl.BlockSpec((1, c), lambda k, p: (0, 0))]
        args += [v.reshape(2, m2, n1, pc), gate.reshape(2, m2, n1, pc), bias]
    return _pc(functools.partial(_fft2_kernel, mode=mode, m2=m2, kb=kb),
               jax.ShapeDtypeStruct((2, m2, n1, pc), BF16), (n1 // kb, pc // c), specs, nat,
               name="fft2_" + mode)(*args)


def _hy_filter_kernel(emb_ref, w1_ref, b1_ref, w2_ref, b2_ref, w3f_ref, w3b_ref, win_ref, o_ref, *, n):
    hp = lax.Precision.HIGHEST
    h = jnp.sin(jnp.dot(emb_ref[...], w1_ref[...], precision=hp, preferred_element_type=F32) + b1_ref[...])
    h = jnp.sin(jnp.dot(h, w2_ref[...], precision=hp, preferred_element_type=F32) + b2_ref[...])
    hf = jnp.dot(h[:n], w3f_ref[...], precision=hp, preferred_element_type=F32) * win_ref[0:n, :]
    hb = jnp.dot(h[n:], w3b_ref[...], precision=hp, preferred_element_type=F32) * win_ref[n:2 * n, :]
    den = jnp.sum(jnp.abs(hf), axis=0, keepdims=True) + jnp.sum(jnp.abs(hb), axis=0, keepdims=True)
    o_ref[0:n, :] = (hf / den).astype(o_ref.dtype)
    o_ref[n:2 * n, :] = (hb / den).astype(o_ref.dtype)


def hyena_filters(n, w1, b1, w2, b2, w3):
    t = np.arange(n, dtype=np.float32)
    t_norm = t / max(n - 1, 1)
    bands = np.linspace(1e-4, HY_BANDS - 1, HY_BANDS, dtype=np.float32)
    ang = (2.0 * math.pi * t / n)[:, None] * bands[None, :]
    emb = np.concatenate([t_norm[:, None], np.cos(ang), -np.sin(ang)], axis=-1).astype(np.float32)
    emb = np.pad(emb, ((0, 0), (0, LANES - HY_EMB)))
    deltas = np.linspace(math.log(HY_TARGET) / HY_SLOW, math.log(HY_TARGET) / HY_FAST, GROUP_W, dtype=np.float32)
    win = (np.exp(-t_norm[:, None] * np.abs(deltas)[None, :]) + np.float32(HY_SHIFT)).astype(np.float32)
    back = np.concatenate([[0], np.arange(n - 2, -1, -1)])
    emb2 = jnp.asarray(np.concatenate([emb, emb[back]], axis=0))
    win_b = win[back]
    win_b[0] = 0.0
    win2 = jnp.asarray(np.concatenate([win, win_b], axis=0))
    w1p = jnp.pad(w1.astype(F32), ((0, LANES - HY_EMB), (0, 0)))
    w3r = w3.astype(F32).reshape(HY_HIDDEN, 2, 2, GROUP_W)
    w3f, w3b = w3r[:, :, 0].reshape(HY_HIDDEN, 2 * GROUP_W), w3r[:, :, 1].reshape(HY_HIDDEN, 2 * GROUP_W)
    tc = 256
    full = lambda a: pl.BlockSpec(a.shape, lambda j: (0, 0))
    colw = pl.BlockSpec((HY_HIDDEN, tc), lambda j: (0, j))
    b1r, b2r = b1.reshape(1, -1).astype(F32), b2.reshape(1, -1).astype(F32)
    w2f = w2.astype(F32)
    return _pc(functools.partial(_hy_filter_kernel, n=n),
               jax.ShapeDtypeStruct((2 * n, 2 * GROUP_W), BF16), (2 * GROUP_W // tc,),
               [full(emb2), full(w1p), full(b1r), full(w2f), full(b2r), colw, colw,
                pl.BlockSpec((2 * n, tc), lambda j: (0, j % (GROUP_W // tc)))],
               pl.BlockSpec((2 * n, tc), lambda j: (0, j)), name="hy_filter")(
        emb2, w1p, b1r, w2f, b2r, w3f, w3b, win2)


def hyena_spectrum(n, w1, b1, w2, b2, w3):
    n_fft = 2 * n
    n1, n2 = _fft_factors(n_fft)
    circ = hyena_filters(n, w1, b1, w2, b2, w3)
    f1, tw, f2 = _dft_tables(n_fft, n1, n2, inverse=False)
    return fft_stage2(fft_stage1(circ, f1[:, :n1], tw, n_fft), f2, n_fft, "plain")


def _shortconv_kernel(p_ref, w_ref, b_ref, o_ref):
    p = p_ref[...]
    n = p.shape[0]
    row = lax.broadcasted_iota(jnp.int32, p.shape, 0)
    prev = jnp.where(row >= 1, pltpu.roll(p, 1, axis=0), 0.0)
    nxt = jnp.where(row < n - 1, pltpu.roll(p, n - 1, axis=0), 0.0)
    o_ref[...] = (prev * w_ref[0:1, :] + p * w_ref[1:2, :] + nxt * w_ref[2:3, :] + b_ref[...]).astype(o_ref.dtype)


def hyena_shortconv(proj, w, bias):
    b, n, _ = proj.shape
    return _pc(_shortconv_kernel, jax.ShapeDtypeStruct((3, 2, n, (b // 2) * GROUP_W), BF16), (b, 3),
               [pl.BlockSpec((None, n, GROUP_W), lambda bi, j: (bi, 0, PB_HY + j)),
                pl.BlockSpec((3, GROUP_W), lambda bi, j: (0, j)),
                pl.BlockSpec((1, GROUP_W), lambda bi, j: (0, j))],
               pl.BlockSpec((None, None, n, GROUP_W), lambda bi, j: (j, bi % 2, 0, bi // 2)),
               name="shortconv")(proj, w, bias)


def hyena_mixer(proj, spec, conv_w, conv_b, bias):
    b, n, _ = proj.shape
    n_fft = 2 * n
    n1, n2 = _fft_factors(n_fft)
    pc = (b // 2) * GROUP_W
    u = hyena_shortconv(proj, conv_w, conv_b)
    f1, tw, f2 = _dft_tables(n_fft, n1 // 2, n2, inverse=False)
    g1, twi, g2 = _dft_tables(n_fft, n1, n2 // 2, inverse=True)
    z = u[0]
    for o in range(2):
        zf = fft_stage2(fft_stage1(z, f1, tw, n_fft), f2, n_fft, "filter", (spec,), order=o)
        zf = zf.reshape(2, n_fft, pc)
        z = fft_stage2(fft_stage1(zf, g1, twi, n_fft), g2, n_fft, "final", (z, u[1 + o], bias[o:o + 1]))
        z = z.reshape(2, n, pc)
    return z


def _rot_cols(w):
    return jnp.concatenate([-w[..., 16:32], w[..., 0:16], -w[..., 48:64], w[..., 32:48]], axis=-1)


def _rot_perm(g):
    return jnp.concatenate([g[16:32], g[0:16], g[48:64], g[32:48]])


def _prep_layer(w_in, w_uq, w_ukv, q_a_g, kv_a_g, q_g, k_g):
    d = w_in.shape[0]
    ckv, krope, rest = w_in[:, :256], w_in[:, 256:320], w_in[:, 320:]
    w_in_p = jnp.concatenate([rest, ckv, krope, _rot_cols(krope), jnp.zeros((d, 128), w_in.dtype)], axis=1)
    wq = w_uq.reshape(GROUP_W, MLA_HEADS, MLA_QK)
    wq = jnp.concatenate([wq, _rot_cols(wq[..., MLA_NOPE:])], axis=-1).reshape(GROUP_W, MLA_HEADS * 256)
    r2 = lambda a: a.reshape(1, -1).astype(F32)
    mla = dict(wkv=w_ukv.astype(BF16), wq=wq.astype(BF16), gkv=r2(kv_a_g), gqa=r2(q_a_g),
               gkn=r2(k_g[:MLA_NOPE]), gkt=r2(jnp.concatenate([k_g[MLA_NOPE:], _rot_perm(k_g[MLA_NOPE:])])),
               gqn=r2(q_g[:MLA_NOPE]), gqt=r2(jnp.concatenate([q_g[MLA_NOPE:], _rot_perm(q_g[MLA_NOPE:])])))
    return w_in_p.astype(BF16), mla


def _rope_table(n_rows):
    row = np.repeat(np.arange(n_rows, dtype=np.float32), GRID_W)
    col = np.tile(np.arange(GRID_W, dtype=np.float32), n_rows)
    n_freq = MLA_ROPE // 4
    inv_freq = (ROPE_BASE ** (-np.arange(n_freq, dtype=np.float32) / n_freq)).astype(np.float32)
    ang_r, ang_c = row[:, None] * inv_freq, col[:, None] * inv_freq
    ang = np.concatenate([ang_r, ang_r, ang_c, ang_c], axis=-1)
    return jnp.asarray(np.concatenate([np.cos(ang), np.sin(ang)], axis=-1), F32)


def kernel(x, c, ctx, c_ctx, w_mod, b_mod, norm1, norm2, w_in, w_out, mla_q_a_norm, mla_kv_a_norm, mla_w_uq, mla_w_ukv, mla_q_norm, mla_k_norm, s5_lam_re, s5_lam_im, s5_log_step, s5_b_re, s5_b_im, s5_c_re, s5_c_im, s5_d, s5_w_glu, s5_b_glu, hg_lower_bounds, hg_o_norm, hy_conv_w, hy_conv_b, hy_w1, hy_b1, hy_w2, hy_b2, hy_w3, hy_bias, mlp_w1, mlp_w2):
    bsz, seq, dm = x.shape
    n_ctx = ctx.shape[1]
    depth = w_mod.shape[0]
    assert bsz % 2 == 0 and bsz <= 8 and seq % GRID_W == 0 and S5_T * S5_N == GROUP_W
    r2 = lambda a: a.reshape(1, -1).astype(F32)

    cs_l = _rope_table(seq // GRID_W)
    cs_c = jnp.concatenate([jnp.ones((n_ctx, MLA_ROPE), F32), jnp.zeros((n_ctx, MLA_ROPE), F32)], axis=-1)
    sm = jax.nn.softmax(hg_lower_bounds.astype(F32), axis=1)
    lower = jnp.clip(jnp.cumsum(sm, axis=1) - sm[:, :1], 0.0, 1.0)
    c8 = jnp.concatenate([c.astype(F32), c_ctx.astype(F32)[None], jnp.zeros((7 - bsz, dm), F32)], axis=0)
    zero_state = jnp.zeros((bsz, HG_HEADS, HG_D, HG_D), F32)
    zero_x = jnp.zeros((S5_G, 8, 256), F32)

    xc = ctx
    for l in range(depth):
        last = l == depth - 1
        mod = modvec(c8, w_mod[l].astype(F32), r2(b_mod[l]))
        mod_l = [mod[:bsz, i * dm:(i + 1) * dm][:, None, :] for i in range(N_MOD)]
        mod_c = [mod[bsz:bsz + 1, i * dm:(i + 1) * dm][:, None, :] for i in range(N_MOD)]
        w_in_p, mla_p = _prep_layer(w_in[l], mla_w_uq[l], mla_w_ukv[l], mla_q_a_norm[l], mla_kv_a_norm[l],
                                    mla_q_norm[l], mla_k_norm[l])
        pl_ = modproj(x, mod_l[0], mod_l[1], r2(norm1[l]), w_in_p, F32)
        pc_ = modproj(xc, mod_c[0], mod_c[1], r2(norm1[l]), w_in_p, F32)

        k_c, v_c, q_c = mla_prep(pc_, cs_c, mla_p, with_q=not last)
        k_l, v_l, q_l = mla_prep(pl_, cs_l, mla_p, with_q=True)
        y_a = attention(q_l, [(k_l, v_l), (k_c, v_c)])

        tabs = s5_tables(s5_lam_re[l], s5_lam_im[l], s5_log_step[l], s5_b_re[l], s5_b_im[l], s5_c_re[l], s5_c_im[l])
        yc_s5, fin = s5_scan(pc_[..., PB_S5U * GROUP_W:(PB_S5U + 1) * GROUP_W], zero_x, tabs)
        yl_s5, _ = s5_scan(pl_[..., PB_S5U * GROUP_W:(PB_S5U + 1) * GROUP_W], fin, tabs)
        glu_w, glu_b = s5_w_glu[l].astype(BF16), r2(s5_b_glu[l])
        y_b = s5_out(yl_s5, pl_, r2(s5_d[l]), glu_w, glu_b)

        la_f, lb_f = r2(jnp.log(lower[0, l])), r2(jnp.log1p(-lower[0, l]))
        la_b, lb_b = r2(jnp.log(lower[1, l])), r2(jnp.log1p(-lower[1, l]))
        oc_f, s_f = hgrn_dir(pc_, la_f, lb_f, zero_state, reverse=False)
        oc_b, s_b = hgrn_dir(pc_, la_b, lb_b, zero_state, reverse=True)
        ol_f, _ = hgrn_dir(pl_, la_f, lb_f, s_f, reverse=False)
        ol_b, _ = hgrn_dir(pl_, la_b, lb_b, s_b, reverse=True)
        y_c = hg_out(ol_f, ol_b, pl_, r2(hg_o_norm[l]))

        hy_args = (hy_conv_w[l].astype(F32), r2(hy_conv_b[l]), hy_bias[l].astype(F32))
        spec_l = hyena_spectrum(seq, hy_w1[l], hy_b1[l], hy_w2[l], hy_b2[l], hy_w3[l])
        y_d = hyena_mixer(pl_, spec_l, *hy_args)

        w_out_b = w_out[l].astype(BF16)
        w1_b, w2_b = mlp_w1[l].astype(BF16), mlp_w2[l].astype(BF16)
        x = mix_out([y_a, y_b, y_c], y_d, w_out_b, x, mod_l[2])
        h1 = modproj(x, mod_l[3], mod_l[4], r2(norm2[l]), w1_b, BF16, act="relu2", tm=1024)
        x = mlp_out(h1, w2_b, x, mod_l[5])

        if not last:
            y_ac = attention(q_c, [(k_c, v_c)])
            y_bc = s5_out(yc_s5, pc_, r2(s5_d[l]), glu_w, glu_b)
            y_cc = hg_out(oc_f, oc_b, pc_, r2(hg_o_norm[l]))
            spec_c = hyena_spectrum(n_ctx, hy_w1[l], hy_b1[l], hy_w2[l], hy_b2[l], hy_w3[l])
            y_dc = hyena_mixer(pc_, spec_c, *hy_args)
            xc = mix_out([y_ac, y_bc, y_cc], y_dc, w_out_b, xc, mod_c[2])
            h1c = modproj(xc, mod_c[3], mod_c[4], r2(norm2[l]), w1_b, BF16, act="relu2", tm=1024)
            xc = mlp_out(h1c, w2_b, xc, mod_c[5])
    return x
```

```python
import functools
import math

import numpy as np
import jax
import jax.numpy as jnp
from jax import lax
from jax.experimental import pallas as pl
from jax.experimental.pallas import tpu as pltpu

F32 = jnp.float32
BF16 = jnp.bfloat16
EPS = 1e-6

GRID_W = 64
GROUP_W = 512
N_MOD = 6
MLA_NOPE = 128
MLA_ROPE = 64
MLA_QK = MLA_NOPE + MLA_ROPE
MLA_HEADS = 4
KV_LORA = 256
MLA_SCALE = 1.0 / math.sqrt(MLA_QK)
ROPE_BASE = 10000.0
S5_G = 32
S5_N = 16
S5_P = 64
S5_T = 32
HG_HEADS = 4
HG_D = 128
HG_SUB = 16
HY_EMB = 33
HY_BANDS = 16
HY_HIDDEN = 64
HY_TARGET = 1e-2
HY_FAST = 0.3
HY_SLOW = 1.5
HY_SHIFT = 0.05

LANES = 128
VMEM_LIMIT_MB = 56

PB_S5U, PB_HGF, PB_HGB, PB_HGI, PB_CQ, PB_HGQ, PB_HGG, PB_HY, PB_KV = 0, 1, 2, 3, 4, 5, 6, 7, 10
N_PROJ = 11 * GROUP_W


def _pc(kernel, out_shape, grid, in_specs, out_specs, scratch=(), vmem_mb=VMEM_LIMIT_MB, name=None):
    return pl.pallas_call(
        kernel, out_shape=out_shape, grid=grid, in_specs=in_specs, out_specs=out_specs,
        scratch_shapes=list(scratch), name=name,
        compiler_params=pltpu.CompilerParams(
            dimension_semantics=("arbitrary",) * len(grid), vmem_limit_bytes=vmem_mb << 20))


def _tile(n, pref):
    t = min(n, pref)
    while n % t:
        t //= 2
    return t


def _silu(x):
    return x * jax.nn.sigmoid(x)


def _modvec_kernel(c_ref, w_ref, b_ref, o_ref):
    s = _silu(c_ref[...])
    o_ref[...] = jnp.dot(s.astype(BF16), w_ref[...].astype(BF16), preferred_element_type=F32) + b_ref[...]


def modvec(c8, w, b):
    d, n = w.shape
    tn = _tile(n, 1536)
    return _pc(_modvec_kernel, jax.ShapeDtypeStruct((8, n), F32), (n // tn,),
               [pl.BlockSpec((8, d), lambda j: (0, 0)),
                pl.BlockSpec((d, tn), lambda j: (0, j)),
                pl.BlockSpec((1, tn), lambda j: (0, j))],
               pl.BlockSpec((8, tn), lambda j: (0, j)), name="modvec")(c8, w, b)


def _modproj_kernel(x_ref, sh_ref, sc_ref, g_ref, w_ref, o_ref, h_ref, *, act):
    @pl.when(pl.program_id(2) == 0)
    def _():
        x = x_ref[...]
        ms = jnp.mean(x * x, axis=-1, keepdims=True)
        y = x * lax.rsqrt(ms + EPS) * g_ref[...]
        h_ref[...] = (y * (1.0 + sc_ref[...]) + sh_ref[...]).astype(BF16)

    acc = jnp.dot(h_ref[...], w_ref[...], preferred_element_type=F32)
    if act == "relu2":
        acc = jnp.square(jnp.maximum(acc, 0.0))
    o_ref[...] = acc.astype(o_ref.dtype)


def modproj(x, shift, scale, g, w, out_dtype, act=None, tm=512, tn=512):
    b, l, d = x.shape
    n = w.shape[1]
    tm, tn = _tile(l, tm), _tile(n, tn)
    per_b = shift.shape[0] == b
    mod_map = (lambda bi, i, j: (bi, 0, 0)) if per_b else (lambda bi, i, j: (0, 0, 0))
    return _pc(functools.partial(_modproj_kernel, act=act),
               jax.ShapeDtypeStruct((b, l, n), out_dtype), (b, l // tm, n // tn),
               [pl.BlockSpec((None, tm, d), lambda bi, i, j: (bi, i, 0)),
                pl.BlockSpec((None, 1, d), mod_map),
                pl.BlockSpec((None, 1, d), mod_map),
                pl.BlockSpec((1, d), lambda bi, i, j: (0, 0)),
                pl.BlockSpec((d, tn), lambda bi, i, j: (0, j))],
               pl.BlockSpec((None, tm, tn), lambda bi, i, j: (bi, i, j)),
               scratch=[pltpu.VMEM((tm, d), BF16)], name="modproj_" + (act or "lin"))(x, shift, scale, g, w)


def _mmres_kernel(*refs, n_lhs, kw):
    a_refs = refs[:n_lhs]
    w_ref, res_ref, gate_ref, o_ref = refs[n_lhs:]
    acc = None
    for i, a in enumerate(a_refs):
        d = jnp.dot(a[...], w_ref[i * kw:(i + 1) * kw, :], preferred_element_type=F32)
        acc = d if acc is None else acc + d
    o_ref[...] = res_ref[...] + gate_ref[...] * acc


def mix_out(ys, yd_hlay, w, res, gate, tm=512, tn=1024):
    b, l, kw = ys[0].shape
    n = w.shape[1]
    tm, tn = _tile(l, tm), _tile(n, tn)
    per_b = gate.shape[0] == b
    gmap = (lambda bi, i, j: (bi, 0, j)) if per_b else (lambda bi, i, j: (0, 0, j))
    y_spec = pl.BlockSpec((None, tm, kw), lambda bi, i, j: (bi, i, 0))
    yd_spec = pl.BlockSpec((None, tm, kw), lambda bi, i, j: (bi % 2, i, bi // 2))
    return _pc(functools.partial(_mmres_kernel, n_lhs=4, kw=kw),
               jax.ShapeDtypeStruct(res.shape, F32), (b, l // tm, n // tn),
               [y_spec, y_spec, y_spec, yd_spec,
                pl.BlockSpec((4 * kw, tn), lambda bi, i, j: (0, j)),
                pl.BlockSpec((None, tm, tn), lambda bi, i, j: (bi, i, j)),
                pl.BlockSpec((None, 1, tn), gmap)],
               pl.BlockSpec((None, tm, tn), lambda bi, i, j: (bi, i, j)), name="mix_out")(*ys, yd_hlay, w, res, gate)


def _mmresk_kernel(a_ref, w_ref, res_ref, gate_ref, o_ref, acc_ref):
    k = pl.program_id(3)

    @pl.when(k == 0)
    def _():
        acc_ref[...] = jnp.zeros_like(acc_ref)

    acc_ref[...] += jnp.dot(a_ref[...], w_ref[...], preferred_element_type=F32)

    @pl.when(k == pl.num_programs(3) - 1)
    def _():
        o_ref[...] = res_ref[...] + gate_ref[...] * acc_ref[...]


def mlp_out(a, w, res, gate, tm=1024, tn=1024, tk=1024):
    b, l, kk = a.shape
    n = w.shape[1]
    tm, tn, tk = _tile(l, tm), _tile(n, tn), _tile(kk, tk)
    per_b = gate.shape[0] == b
    gmap = (lambda bi, i, j, k: (bi, 0, j)) if per_b else (lambda bi, i, j, k: (0, 0, j))
    return _pc(_mmresk_kernel, jax.ShapeDtypeStruct(res.shape, F32), (b, l // tm, n // tn, kk // tk),
               [pl.BlockSpec((None, tm, tk), lambda bi, i, j, k: (bi, i, k)),
                pl.BlockSpec((tk, tn), lambda bi, i, j, k: (k, j)),
                pl.BlockSpec((None, tm, tn), lambda bi, i, j, k: (bi, i, j)),
                pl.BlockSpec((None, 1, tn), gmap)],
               pl.BlockSpec((None, tm, tn), lambda bi, i, j, k: (bi, i, j)),
               scratch=[pltpu.VMEM((tm, tn), F32)], name="mlp_out")(a, w, res, gate)


def _mla_prep_kernel(kv_ref, cq_ref, cs_ref, gkv_ref, gqa_ref, gkn_ref, gkt_ref, gqn_ref, gqt_ref,
                     wkv_ref, wq_ref, k_ref, v_ref, q_ref):
    lane = lax.broadcasted_iota(jnp.int32, (1, LANES), 1)
    low = (lane < MLA_ROPE).astype(F32)
    cs = cs_ref[...]
    kv = kv_ref[...].astype(F32)
    ckv = kv[:, :KV_LORA]
    ckvn = ckv * lax.rsqrt(jnp.mean(ckv * ckv, axis=-1, keepdims=True) + EPS) * gkv_ref[...]
    kvp = jnp.dot(ckvn.astype(BF16), wkv_ref[...], preferred_element_type=F32)
    pe2 = kv[:, KV_LORA:KV_LORA + LANES]
    ss_pe = jnp.sum(pe2 * pe2 * low, axis=-1, keepdims=True)
    pe_tab = cs * gkt_ref[...]
    for h in range(MLA_HEADS):
        knope = kvp[:, 256 * h:256 * h + 128]
        rs = lax.rsqrt((jnp.sum(knope * knope, axis=-1, keepdims=True) + ss_pe) * (1.0 / MLA_QK) + EPS)
        t = pe2 * pe_tab * rs
        t = (t + pltpu.roll(t, MLA_ROPE, axis=1)) * low
        k_ref[h, :, 0:128] = (knope * gkn_ref[...] * rs).astype(BF16)
        k_ref[h, :, 128:256] = t.astype(BF16)
        v_ref[h] = kvp[:, 256 * h + 128:256 * h + 256].astype(BF16)
    if q_ref is not None:
        cq = cq_ref[...].astype(F32)
        cqn = cq * lax.rsqrt(jnp.mean(cq * cq, axis=-1, keepdims=True) + EPS) * gqa_ref[...]
        qp = jnp.dot(cqn.astype(BF16), wq_ref[...], preferred_element_type=F32)
        q_tab = cs * gqt_ref[...]
        for h in range(MLA_HEADS):
            qn = qp[:, 256 * h:256 * h + 128]
            qe = qp[:, 256 * h + 128:256 * h + 256]
            ss = jnp.sum(qn * qn, axis=-1, keepdims=True) + jnp.sum(qe * qe * low, axis=-1, keepdims=True)
            rs = lax.rsqrt(ss * (1.0 / MLA_QK) + EPS) * MLA_SCALE
            t = qe * q_tab * rs
            t = t + pltpu.roll(t, MLA_ROPE, axis=1)
            q_ref[h, :, 0:128] = (qn * gqn_ref[...] * rs).astype(BF16)
            q_ref[h, :, 128:256] = t.astype(BF16)


def _mla_prep_kernel_noq(kv_ref, cs_ref, gkv_ref, gkn_ref, gkt_ref, wkv_ref, k_ref, v_ref):
    _mla_prep_kernel(kv_ref, None, cs_ref, gkv_ref, None, gkn_ref, gkt_ref, None, None, wkv_ref, None,
                     k_ref, v_ref, None)


def mla_prep(proj, cs, p, with_q, tm=512):
    b, l, _ = proj.shape
    tm = _tile(l, tm)
    row = lambda w: pl.BlockSpec((1, w), lambda bi, i: (0, 0))
    kv_spec = pl.BlockSpec((None, tm, GROUP_W), lambda bi, i: (bi, i, PB_KV))
    cq_spec = pl.BlockSpec((None, tm, GROUP_W), lambda bi, i: (bi, i, PB_CQ))
    cs_spec = pl.BlockSpec((tm, LANES), lambda bi, i: (i, 0))
    wkv_spec = pl.BlockSpec((KV_LORA, 1024), lambda bi, i: (0, 0))
    wq_spec = pl.BlockSpec((GROUP_W, 1024), lambda bi, i: (0, 0))
    hd = lambda w: pl.BlockSpec((None, MLA_HEADS, tm, w), lambda bi, i: (bi, 0, i, 0))
    k_sh = jax.ShapeDtypeStruct((b, MLA_HEADS, l, 256), BF16)
    v_sh = jax.ShapeDtypeStruct((b, MLA_HEADS, l, 128), BF16)
    if with_q:
        return _pc(_mla_prep_kernel, (k_sh, v_sh, k_sh), (b, l // tm),
                   [kv_spec, cq_spec, cs_spec, row(KV_LORA), row(GROUP_W), row(128), row(128), row(128), row(128),
                    wkv_spec, wq_spec],
                   (hd(256), hd(128), hd(256)), name="mla_prep")(
            proj, proj, cs, p["gkv"], p["gqa"], p["gkn"], p["gkt"], p["gqn"], p["gqt"], p["wkv"], p["wq"])
    k, v = _pc(_mla_prep_kernel_noq, (k_sh, v_sh), (b, l // tm),
               [kv_spec, cs_spec, row(KV_LORA), row(128), row(128), wkv_spec],
               (hd(256), hd(128)), name="mla_prep_kv")(proj, cs, p["gkv"], p["gkn"], p["gkt"], p["wkv"])
    return k, v, None


def _attn_kernel(*refs, n_seg):
    q = refs[0][...]
    o_ref = refs[1 + 2 * n_seg]
    scores = [lax.dot_general(q, refs[1 + 2 * i][...], (((1,), (1,)), ((), ())), preferred_element_type=F32)
              for i in range(n_seg)]
    m = scores[0].max(axis=-1, keepdims=True)
    for s in scores[1:]:
        m = jnp.maximum(m, s.max(axis=-1, keepdims=True))
    den, acc = None, None
    for i, s in enumerate(scores):
        p = jnp.exp(s - m)
        d = p.sum(axis=-1, keepdims=True)
        a = jnp.dot(p.astype(BF16), refs[2 + 2 * i][...], preferred_element_type=F32)
        den = d if den is None else den + d
        acc = a if acc is None else acc + a
    o_ref[...] = (acc / den).astype(o_ref.dtype)


def attention(q, kvs, tq=512):
    b, h, l, _ = q.shape
    tq = _tile(l, tq)
    specs = [pl.BlockSpec((None, None, tq, 256), lambda bi, hi, i: (bi, hi, i, 0))]
    args = [q]
    for k, v in kvs:
        lk = k.shape[2]
        specs += [pl.BlockSpec((None, None, lk, 256), lambda bi, hi, i: (bi, hi, 0, 0)),
                  pl.BlockSpec((None, None, lk, 128), lambda bi, hi, i: (bi, hi, 0, 0))]
        args += [k, v]
    return _pc(functools.partial(_attn_kernel, n_seg=len(kvs)),
               jax.ShapeDtypeStruct((b, l, h * 128), BF16), (b, h, l // tq), specs,
               pl.BlockSpec((None, tq, 128), lambda bi, hi, i: (bi, i, hi)), name="attention")(*args)


def s5_tables(lam_re, lam_im, log_step, b_re, b_im, c_re, c_im):
    t = S5_T
    step = jnp.exp(log_step.astype(F32))[..., None]
    re, im = lam_re.astype(F32) * step, lam_im.astype(F32) * step
    d = jnp.arange(t + 1, dtype=F32)[:, None, None, None]
    mag = jnp.exp(d * re)
    ar, ai = mag * jnp.cos(d * im), mag * jnp.sin(d * im)
    lr, li = lam_re.astype(F32), lam_im.astype(F32)
    den = lr * lr + li * li
    er, ei = ar[1] - 1.0, ai[1]
    cr, ci = (er * lr + ei * li) / den, (ei * lr - er * li) / den
    br, bi = b_re.astype(F32), b_im.astype(F32)
    bbr = cr[..., None] * br - ci[..., None] * bi
    bbi = cr[..., None] * bi + ci[..., None] * br
    ccr, cci = c_re.astype(F32), c_im.astype(F32)
    car = ccr[None] * ar[:, :, :, None, :] - cci[None] * ai[:, :, :, None, :]
    cai = ccr[None] * ai[:, :, :, None, :] + cci[None] * ar[:, :, :, None, :]
    kd = jnp.einsum("dxgnp,xgpm->xgdnm", car[:t], bbr) - jnp.einsum("dxgnp,xgpm->xgdnm", cai[:t], bbi)
    kcat = jnp.stack([kd[0], kd[1][:, ::-1]], axis=1).transpose(0, 1, 4, 2, 3).reshape(S5_G, 2, S5_N, t * S5_N)
    pw = (jnp.arange(t - 1, -1, -1), jnp.arange(t))
    w_re, w_im = [], []
    for x in range(2):
        a_r, a_i = ar[pw[x], x], ai[pw[x], x]
        w_re.append(a_r[..., None] * bbr[x][None] - a_i[..., None] * bbi[x][None])
        w_im.append(a_r[..., None] * bbi[x][None] + a_i[..., None] * bbr[x][None])
    ws = jnp.concatenate(w_re + w_im, axis=2).transpose(1, 0, 3, 2).reshape(S5_G, t * S5_N, 4 * S5_P)
    pv = (jnp.arange(1, t + 1), jnp.arange(t, 0, -1))
    vm = jnp.concatenate([car[pv[0], 0], car[pv[1], 1], -cai[pv[0], 0], -cai[pv[1], 1]], axis=-1)
    vm = vm.transpose(1, 3, 0, 2).reshape(S5_G, 4 * S5_P, t * S5_N)
    dec = jnp.stack([jnp.concatenate([ar[t, 0], ar[t, 1]], axis=-1),
                     jnp.concatenate([ai[t, 0], ai[t, 1]], axis=-1)], axis=1)
    return kcat, ws.astype(BF16), vm.astype(BF16), dec


def _s5a_kernel(u_ref, k_ref, ws_ref, y_ref, s_ref, m_ref):
    kf, kb = k_ref[0], k_ref[1]
    lane = lax.broadcasted_iota(jnp.int32, kf.shape, 1)
    for s in range(S5_T):
        lo, hi = S5_N * s, S5_N * (s + 1)
        f = kf if s == 0 else jnp.where(lane >= lo, pltpu.roll(kf, lo, axis=1), 0.0)
        bk = kb if s == S5_T - 1 else jnp.where(lane < hi, pltpu.roll(kb, hi, axis=1), 0.0)
        m_ref[lo:hi, :] = (f + bk).astype(BF16)
    u = u_ref[...]
    y_ref[...] = jnp.dot(u, m_ref[...], preferred_element_type=F32)
    s_ref[...] = jnp.dot(u, ws_ref[...], preferred_element_type=F32)


def _s5b_kernel(yi_ref, s_ref, x0_ref, dec_ref, vm_ref, y_ref, xf_ref, xa_ref, xb_ref, *, nchunk):
    ar = dec_ref[0:1, :]
    ai = dec_ref[1:2, :]
    fwd_lane = lax.broadcasted_iota(jnp.int32, (8, 128), 1) < S5_P

    def step(j, carry):
        xr, xi = carry
        rf = pl.ds(pl.multiple_of(j * 8, 8), 8)
        rb = pl.ds(pl.multiple_of((nchunk - 1 - j) * 8, 8), 8)
        x = jnp.concatenate([xr, xi], axis=1)
        xa_ref[rf, :] = x
        xb_ref[rb, :] = x
        sf, sb = s_ref[rf, :], s_ref[rb, :]
        sr = jnp.where(fwd_lane, sf[:, :128], sb[:, :128])
        si = jnp.where(fwd_lane, sf[:, 128:], sb[:, 128:])
        return ar * xr - ai * xi + sr, ar * xi + ai * xr + si

    x0 = x0_ref[...]
    xr, xi = lax.fori_loop(0, nchunk, step, (x0[:, :128], x0[:, 128:]), unroll=4)
    xf_ref[...] = jnp.concatenate([xr, xi], axis=1)
    pick = (lax.broadcasted_iota(jnp.int32, xa_ref.shape, 1) % 128) < S5_P
    xin = jnp.where(pick, xa_ref[...], xb_ref[...])
    y = yi_ref[...] + jnp.dot(xin.astype(BF16), vm_ref[...], preferred_element_type=F32)
    y_ref[...] = y.astype(y_ref.dtype)


def s5_scan(u, x0, tabs):
    kcat, ws, vm, dec = tabs
    b, l, _ = u.shape
    j = l // S5_T
    r = j * 8
    ug = u.astype(BF16).reshape(b, j, S5_T, S5_G, S5_N).transpose(3, 1, 0, 2, 4)
    ug = jnp.pad(ug, ((0, 0), (0, 0), (0, 8 - b), (0, 0), (0, 0))).reshape(S5_G, r, GROUP_W)
    g3 = lambda w: pl.BlockSpec((None, r, w), lambda g: (g, 0, 0))
    yi, s = _pc(_s5a_kernel,
                (jax.ShapeDtypeStruct((S5_G, r, GROUP_W), F32), jax.ShapeDtypeStruct((S5_G, r, 256), F32)),
                (S5_G,),
                [g3(GROUP_W), pl.BlockSpec((None, 2, S5_N, GROUP_W), lambda g: (g, 0, 0, 0)),
                 pl.BlockSpec((None, GROUP_W, 256), lambda g: (g, 0, 0))],
                (g3(GROUP_W), g3(256)),
                scratch=[pltpu.VMEM((GROUP_W, GROUP_W), BF16)], name="s5a")(ug, kcat, ws)
    y, xf = _pc(functools.partial(_s5b_kernel, nchunk=j),
                (jax.ShapeDtypeStruct((S5_G, r, GROUP_W), BF16), jax.ShapeDtypeStruct((S5_G, 8, 256), F32)),
                (S5_G,),
                [g3(GROUP_W), g3(256),
                 pl.BlockSpec((None, 8, 256), lambda g: (g, 0, 0)),
                 pl.BlockSpec((None, 2, 128), lambda g: (g, 0, 0)),
                 pl.BlockSpec((None, 256, GROUP_W), lambda g: (g, 0, 0))],
                (g3(GROUP_W), pl.BlockSpec((None, 8, 256), lambda g: (g, 0, 0))),
                scratch=[pltpu.VMEM((r, 256), F32), pltpu.VMEM((r, 256), F32)], name="s5b")(yi, s, x0, dec, vm)
    y = y.reshape(S5_G, j, 8, S5_T, S5_N)[:, :, :b].transpose(2, 1, 3, 0, 4).reshape(b, l, GROUP_W)
    return y, xf


def _s5_out_kernel(y_ref, u_ref, d_ref, w_ref, b_ref, o_ref):
    y = jax.nn.gelu(y_ref[...].astype(F32) + d_ref[...] * u_ref[...].astype(F32))
    z = jnp.dot(y.astype(BF16), w_ref[...], preferred_element_type=F32) + b_ref[...]
    o_ref[...] = (y * jax.nn.sigmoid(z)).astype(o_ref.dtype)


def s5_out(y, proj, d, w, bias, tm=512):
    b, l, _ = y.shape
    tm = _tile(l, tm)
    row = pl.BlockSpec((1, GROUP_W), lambda bi, i: (0, 0))
    return _pc(_s5_out_kernel, jax.ShapeDtypeStruct((b, l, GROUP_W), BF16), (b, l // tm),
               [pl.BlockSpec((None, tm, GROUP_W), lambda bi, i: (bi, i, 0)),
                pl.BlockSpec((None, tm, GROUP_W), lambda bi, i: (bi, i, PB_S5U)),
                row, pl.BlockSpec((GROUP_W, GROUP_W), lambda bi, i: (0, 0)), row],
               pl.BlockSpec((None, tm, GROUP_W), lambda bi, i: (bi, i, 0)), name="s5_out")(y, proj, d, w, bias)


def _hgrn_kernel(z_ref, i_ref, q_ref, la_ref, lb_ref, s0_ref, o_ref, sfin_ref, st_ref, *, reverse, nsub):
    c = HG_SUB

    @pl.when(pl.program_id(1) == 0)
    def _():
        st_ref[...] = s0_ref[...]

    hlf = c // 2
    ones = jnp.ones((HG_D, HG_D), BF16)
    row = lax.broadcasted_iota(jnp.int32, (c, HG_D), 0)
    row8 = lax.broadcasted_iota(jnp.int32, (hlf, HG_D), 0)

    def sub(jj, carry):
        j = (nsub - 1 - jj) if reverse else jj
        rows = pl.ds(pl.multiple_of(j * c, c), c)
        for h in range(HG_HEADS):
            sl = slice(HG_D * h, HG_D * (h + 1))
            z = z_ref[rows, sl].astype(F32)
            v = i_ref[rows, sl].astype(F32)
            q = _silu(q_ref[rows, sl].astype(F32))
            a = la_ref[:, sl]
            bb = lb_ref[:, sl] + (jnp.minimum(z, 0.0) - jnp.log1p(jnp.exp(-jnp.abs(z))))
            g = jnp.maximum(a, bb) + jnp.log1p(jnp.exp(-jnp.abs(a - bb)))
            k = 1.0 - jnp.exp(g)
            gc = g
            for sh in (1, 2, 4, 8):
                if reverse:
                    gc = gc + jnp.where(row < c - sh, pltpu.roll(gc, c - sh, axis=0), 0.0)
                else:
                    gc = gc + jnp.where(row >= sh, pltpu.roll(gc, sh, axis=0), 0.0)
            blocks, meta = [], []
            for s in range(c):
                for half in range(2):
                    r0 = half * hlf
                    if (r0 > s) if reverse else (r0 + hlf - 1 < s):
                        continue
                    e = gc[r0:r0 + hlf, :] - gc[s:s + 1, :]
                    if not ((r0 + hlf - 1 <= s) if reverse else (r0 >= s)):
                        keep = (row8 + r0 <= s) if reverse else (row8 + r0 >= s)
                        e = jnp.where(keep, e, -1e30)
                    blocks.append(q[r0:r0 + hlf, :] * jnp.exp(e) * k[s:s + 1, :])
                    meta.append((half, s))
            rsum = jnp.dot(jnp.concatenate(blocks, axis=0).astype(BF16), ones, preferred_element_type=F32)
            halves = [jnp.zeros((hlf, HG_D), F32), jnp.zeros((hlf, HG_D), F32)]
            for i, (half, s) in enumerate(meta):
                halves[half] = halves[half] + rsum[hlf * i:hlf * (i + 1), :] * v[s:s + 1, :]
            o = jnp.concatenate(halves, axis=0)
            st = st_ref[h]
            o = o + lax.dot_general((q * jnp.exp(gc)).astype(BF16), st.astype(BF16),
                                    (((1,), (1,)), ((), ())), preferred_element_type=F32)
            gl = gc[0:1, :] if reverse else gc[c - 1:c, :]
            kd = k * jnp.exp(gl - gc)
            ds = lax.dot_general(v.astype(BF16), kd.astype(BF16), (((0,), (0,)), ((), ())),
                                 preferred_element_type=F32)
            st_ref[h] = st * jnp.exp(gl) + ds
            o_ref[rows, sl] = o
        return carry

    lax.fori_loop(0, nsub, sub, 0, unroll=2)

    @pl.when(pl.program_id(1) == pl.num_programs(1) - 1)
    def _():
        sfin_ref[...] = st_ref[...]


def hgrn_dir(proj, la, lb1, s0, reverse, tl=256):
    b, l, _ = proj.shape
    tl = _tile(l, tl)
    nb = l // tl
    blk = (lambda i: nb - 1 - i) if reverse else (lambda i: i)
    col = lambda c: pl.BlockSpec((None, tl, GROUP_W), lambda bi, i: (bi, blk(i), c))
    row = pl.BlockSpec((1, GROUP_W), lambda bi, i: (0, 0))
    st = pl.BlockSpec((None, HG_HEADS, HG_D, HG_D), lambda bi, i: (bi, 0, 0, 0))
    return _pc(functools.partial(_hgrn_kernel, reverse=reverse, nsub=tl // HG_SUB),
               (jax.ShapeDtypeStruct((b, l, GROUP_W), F32), jax.ShapeDtypeStruct(s0.shape, F32)),
               (b, nb),
               [col(PB_HGB if reverse else PB_HGF), col(PB_HGI), col(PB_HGQ), row, row, st],
               (pl.BlockSpec((None, tl, GROUP_W), lambda bi, i: (bi, blk(i), 0)), st),
               scratch=[pltpu.VMEM((HG_HEADS, HG_D, HG_D), F32)],
               name="hgrn_bwd" if reverse else "hgrn_fwd")(proj, proj, proj, la, lb1, s0)


def _hg_out_kernel(of_ref, ob_ref, g_ref, gn_ref, y_ref):
    o = of_ref[...] + ob_ref[...]
    gate = _silu(g_ref[...].astype(F32))
    for h in range(HG_HEADS):
        sl = slice(HG_D * h, HG_D * (h + 1))
        oh = o[:, sl]
        y = oh * lax.rsqrt(jnp.mean(oh * oh, axis=-1, keepdims=True) + EPS) * gn_ref[...]
        y_ref[:, sl] = (y * gate[:, sl]).astype(y_ref.dtype)


def hg_out(of, ob, proj, gn, tm=512):
    b, l, _ = of.shape
    tm = _tile(l, tm)
    t3 = lambda c: pl.BlockSpec((None, tm, GROUP_W), lambda bi, i: (bi, i, c))
    return _pc(_hg_out_kernel, jax.ShapeDtypeStruct((b, l, GROUP_W), BF16), (b, l // tm),
               [t3(0), t3(0), t3(PB_HGG), pl.BlockSpec((1, HG_D), lambda bi, i: (0, 0))],
               t3(0), name="hg_out")(of, ob, proj, gn)


def _fft_factors(n_fft):
    n1 = {8192: 64, 512: 32, 1024: 32, 2048: 32, 4096: 64}[n_fft]
    return n1, n_fft // n1


def _dft_tables(na, nb, ka, mb, inverse):
    n = na * nb
    sgn = 1.0 if inverse else -1.0

    def stacked(phase, scale=1.0):
        ang = sgn * 2.0 * np.pi * (phase % n) / n
        fr, fi = np.cos(ang) * scale, np.sin(ang) * scale
        return np.concatenate([np.concatenate([fr, -fi], axis=-1), np.concatenate([fi, fr], axis=-1)], axis=-2)

    fa = stacked(np.outer(np.arange(na), np.arange(ka)) * nb)
    p, q, b = np.arange(na)[:, None, None], np.arange(mb)[None, :, None], np.arange(nb)[None, None, :]
    fb = stacked(b * p + b * q * na, (1.0 / n) if inverse else 1.0)
    return jnp.asarray(fa, BF16), jnp.asarray(fb, BF16)


def _fft_a_kernel(z_ref, f_ref, o_ref, *, g, na, is_complex):
    f = f_ref[...]
    for i in range(g):
        z = z_ref[:, i] if is_complex else z_ref[i]
        if is_complex:
            z = z.reshape(z.shape[0] * z.shape[1], z.shape[2])
        r = jnp.dot(f, z, preferred_element_type=F32)
        o_ref[0, i] = r[:na]
        o_ref[1, i] = r[na:]


def fft_stage_a(z, fa, g=8):
    is_complex = z.ndim == 4
    nb, ka, pc = z.shape[-3:]
    na = fa.shape[0] // 2
    g = min(g, nb)
    zspec = (pl.BlockSpec((2, g, ka, pc), lambda j: (0, j, 0, 0)) if is_complex
             else pl.BlockSpec((g, ka, pc), lambda j: (j, 0, 0)))
    return _pc(functools.partial(_fft_a_kernel, g=g, na=na, is_complex=is_complex),
               jax.ShapeDtypeStruct((2, nb, na, pc), F32), (nb // g,),
               [zspec, pl.BlockSpec(fa.shape, lambda j: (0, 0))],
               pl.BlockSpec((2, g, na, pc), lambda j: (0, j, 0, 0)), name="fft_a")(z, fa)


def _fft_b_kernel(*refs, mode, mb, kb):
    a_ref, f_ref = refs[0], refs[1]
    o_ref = refs[-1]
    for i in range(kb):
        x = a_ref[:, :, i, :]
        x = x.reshape(x.shape[0] * x.shape[1], x.shape[2]).astype(BF16)
        z = jnp.dot(f_ref[i], x, preferred_element_type=F32)
        zr, zi = z[:mb], z[mb:]
        if mode == "filter":
            h_ref = refs[2]
            hr, hi = h_ref[0, i].astype(F32), h_ref[1, i].astype(F32)
            zr, zi = zr * hr - zi * hi, zr * hi + zi * hr
        elif mode == "final":
            v_ref, g_ref, b_ref = refs[2], refs[3], refs[4]
            zr = g_ref[0, i].astype(F32) * (zr + v_ref[0, i].astype(F32) * b_ref[...])
            zi = g_ref[1, i].astype(F32) * (zi + v_ref[1, i].astype(F32) * b_ref[...])
        o_ref[0, i] = zr.astype(o_ref.dtype)
        o_ref[1, i] = zi.astype(o_ref.dtype)


def fft_stage_b(a, fb, mode, extra=(), order=0, kb=8):
    _, nb, na, pc = a.shape
    mb = fb.shape[1] // 2
    c = GROUP_W
    kb = min(kb, na)
    blk = pl.BlockSpec((2, kb, mb, c), lambda k, p: (0, k, 0, p))
    specs = [pl.BlockSpec((2, nb, kb, c), lambda k, p: (0, 0, k, p)),
             pl.BlockSpec((kb, 2 * mb, 2 * nb), lambda k, p: (k, 0, 0))]
    args = [a, fb]
    if mode == "filter":
        (h,) = extra
        specs.append(pl.BlockSpec((2, kb, mb, c), lambda k, p: (0, k, 0, order)))
        args.append(h)
    elif mode == "final":
        v, gate, bias = extra
        specs += [blk, blk, pl.BlockSpec((1, c), lambda k, p: (0, 0))]
        args += [v, gate, bias]
    return _pc(functools.partial(_fft_b_kernel, mode=mode, mb=mb, kb=kb),
               jax.ShapeDtypeStruct((2, na, mb, pc), BF16), (na // kb, pc // c), specs, blk,
               name="fft_b_" + mode)(*args)


def _hy_filter_kernel(emb_ref, w1_ref, b1_ref, w2_ref, b2_ref, w3f_ref, w3b_ref, win_ref, o_ref, h_ref, *, na):
    hp = lax.Precision.HIGHEST

    @pl.when(pl.program_id(0) == 0)
    def _():
        h = jnp.sin(jnp.dot(emb_ref[...], w1_ref[...], precision=hp, preferred_element_type=F32) + b1_ref[...])
        h_ref[...] = jnp.sin(jnp.dot(h, w2_ref[...], precision=hp, preferred_element_type=F32) + b2_ref[...])

    h = h_ref[...]
    hf = jnp.dot(h, w3f_ref[...], precision=hp, preferred_element_type=F32)
    hb = jnp.dot(h, w3b_ref[...], precision=hp, preferred_element_type=F32)
    row = lax.broadcasted_iota(jnp.int32, hf.shape, 0)
    taps = jnp.where((row & (na - 1)) < na // 2, hf, hb) * win_ref[...]
    o_ref[...] = (taps / jnp.sum(jnp.abs(taps), axis=0, keepdims=True)).astype(o_ref.dtype)


def hyena_filters(n, w1, b1, w2, b2, w3):
    na, nb = _fft_factors(2 * n)
    t = np.arange(n, dtype=np.float32)
    t_norm = t / max(n - 1, 1)
    bands = np.linspace(1e-4, HY_BANDS - 1, HY_BANDS, dtype=np.float32)
    ang = (2.0 * math.pi * t / n)[:, None] * bands[None, :]
    emb = np.concatenate([t_norm[:, None], np.cos(ang), -np.sin(ang)], axis=-1).astype(np.float32)
    emb = np.pad(emb, ((0, 0), (0, LANES - HY_EMB)))
    deltas = np.linspace(math.log(HY_TARGET) / HY_SLOW, math.log(HY_TARGET) / HY_FAST, GROUP_W, dtype=np.float32)
    win = (np.exp(-t_norm[:, None] * np.abs(deltas)[None, :]) + np.float32(HY_SHIFT)).astype(np.float32)
    back = np.concatenate([[0], np.arange(n - 2, -1, -1)])
    win_b = win[back]
    win_b[0] = 0.0
    order = (np.arange(na)[None, :] * nb + np.arange(nb)[:, None]).reshape(-1)
    emb2 = jnp.asarray(np.concatenate([emb, emb[back]], axis=0)[order])
    win2 = jnp.asarray(np.concatenate([win, win_b], axis=0)[order])
    w1p = jnp.pad(w1.astype(F32), ((0, LANES - HY_EMB), (0, 0)))
    w3r = w3.astype(F32).reshape(HY_HIDDEN, 2, 2, GROUP_W)
    w3f, w3b = w3r[:, :, 0].reshape(HY_HIDDEN, 2 * GROUP_W), w3r[:, :, 1].reshape(HY_HIDDEN, 2 * GROUP_W)
    tc = LANES
    full = lambda a: pl.BlockSpec(a.shape, lambda j: (0, 0))
    colw = pl.BlockSpec((HY_HIDDEN, tc), lambda j: (0, j))
    b1r, b2r = b1.reshape(1, -1).astype(F32), b2.reshape(1, -1).astype(F32)
    w2f = w2.astype(F32)
    return _pc(functools.partial(_hy_filter_kernel, na=na),
               jax.ShapeDtypeStruct((2 * n, 2 * GROUP_W), BF16), (2 * GROUP_W // tc,),
               [full(emb2), full(w1p), full(b1r), full(w2f), full(b2r), colw, colw,
                pl.BlockSpec((2 * n, tc), lambda j: (0, j % (GROUP_W // tc)))],
               pl.BlockSpec((2 * n, tc), lambda j: (0, j)),
               scratch=[pltpu.VMEM((2 * n, HY_HIDDEN), F32)], name="hy_filter")(
        emb2, w1p, b1r, w2f, b2r, w3f, w3b, win2)


def hyena_spectrum(n, w1, b1, w2, b2, w3):
    na, nb = _fft_factors(2 * n)
    circ = hyena_filters(n, w1, b1, w2, b2, w3).reshape(nb, na, 2 * GROUP_W)
    fa, fb = _dft_tables(na, nb, na, nb, inverse=False)
    return fft_stage_b(fft_stage_a(circ, fa[:, :na]), fb, "plain")


def _shortconv_kernel(p_ref, w_ref, b_ref, o_ref):
    p = p_ref[...].astype(F32)
    n = p.shape[0]
    row = lax.broadcasted_iota(jnp.int32, p.shape, 0)
    prev = jnp.where(row >= 1, pltpu.roll(p, 1, axis=0), 0.0)
    nxt = jnp.where(row < n - 1, pltpu.roll(p, n - 1, axis=0), 0.0)
    o_ref[...] = (prev * w_ref[0:1, :] + p * w_ref[1:2, :] + nxt * w_ref[2:3, :] + b_ref[...]).astype(o_ref.dtype)


def hyena_shortconv(proj, w, bias):
    b, n, _ = proj.shape
    return _pc(_shortconv_kernel, jax.ShapeDtypeStruct((3, 2, n, (b // 2) * GROUP_W), BF16), (b, 3),
               [pl.BlockSpec((None, n, GROUP_W), lambda bi, j: (bi, 0, PB_HY + j)),
                pl.BlockSpec((3, GROUP_W), lambda bi, j: (0, j)),
                pl.BlockSpec((1, GROUP_W), lambda bi, j: (0, j))],
               pl.BlockSpec((None, None, n, GROUP_W), lambda bi, j: (j, bi % 2, 0, bi // 2)),
               name="shortconv")(proj, w, bias)


def hyena_mixer(proj, spec, conv_w, conv_b, bias):
    b, n, _ = proj.shape
    na, nb = _fft_factors(2 * n)
    pc = (b // 2) * GROUP_W
    u = hyena_shortconv(proj, conv_w, conv_b).reshape(3, 2, na // 2, nb, pc).swapaxes(2, 3)
    fa, fb = _dft_tables(na, nb, na // 2, nb, inverse=False)
    ga, gb = _dft_tables(nb, na, nb, na // 2, inverse=True)
    z = u[0]
    for o in range(2):
        zf = fft_stage_b(fft_stage_a(z, fa), fb, "filter", (spec,), order=o)
        z = fft_stage_b(fft_stage_a(zf, ga), gb, "final", (z, u[1 + o], bias[o:o + 1]))
    return z.swapaxes(1, 2).reshape(2, n, pc)


def _rot_cols(w):
    return jnp.concatenate([-w[..., 16:32], w[..., 0:16], -w[..., 48:64], w[..., 32:48]], axis=-1)


def _rot_perm(g):
    return jnp.concatenate([g[16:32], g[0:16], g[48:64], g[32:48]])


def _prep_layer(w_in, w_uq, w_ukv, q_a_g, kv_a_g, q_g, k_g):
    d = w_in.shape[0]
    ckv, krope, rest = w_in[:, :256], w_in[:, 256:320], w_in[:, 320:]
    w_in_p = jnp.concatenate([rest, ckv, krope, _rot_cols(krope), jnp.zeros((d, 128), w_in.dtype)], axis=1)
    wq = w_uq.reshape(GROUP_W, MLA_HEADS, MLA_QK)
    wq = jnp.concatenate([wq, _rot_cols(wq[..., MLA_NOPE:])], axis=-1).reshape(GROUP_W, MLA_HEADS * 256)
    r2 = lambda a: a.reshape(1, -1).astype(F32)
    mla = dict(wkv=w_ukv.astype(BF16), wq=wq.astype(BF16), gkv=r2(kv_a_g), gqa=r2(q_a_g),
               gkn=r2(k_g[:MLA_NOPE]), gkt=r2(jnp.concatenate([k_g[MLA_NOPE:], _rot_perm(k_g[MLA_NOPE:])])),
               gqn=r2(q_g[:MLA_NOPE]), gqt=r2(jnp.concatenate([q_g[MLA_NOPE:], _rot_perm(q_g[MLA_NOPE:])])))
    return w_in_p.astype(BF16), mla


def _rope_table(n_rows):
    row = np.repeat(np.arange(n_rows, dtype=np.float32), GRID_W)
    col = np.tile(np.arange(GRID_W, dtype=np.float32), n_rows)
    n_freq = MLA_ROPE // 4
    inv_freq = (ROPE_BASE ** (-np.arange(n_freq, dtype=np.float32) / n_freq)).astype(np.float32)
    ang_r, ang_c = row[:, None] * inv_freq, col[:, None] * inv_freq
    ang = np.concatenate([ang_r, ang_r, ang_c, ang_c], axis=-1)
    return jnp.asarray(np.concatenate([np.cos(ang), np.sin(ang)], axis=-1), F32)


def kernel(x, c, ctx, c_ctx, w_mod, b_mod, norm1, norm2, w_in, w_out, mla_q_a_norm, mla_kv_a_norm, mla_w_uq, mla_w_ukv, mla_q_norm, mla_k_norm, s5_lam_re, s5_lam_im, s5_log_step, s5_b_re, s5_b_im, s5_c_re, s5_c_im, s5_d, s5_w_glu, s5_b_glu, hg_lower_bounds, hg_o_norm, hy_conv_w, hy_conv_b, hy_w1, hy_b1, hy_w2, hy_b2, hy_w3, hy_bias, mlp_w1, mlp_w2):
    bsz, seq, dm = x.shape
    n_ctx = ctx.shape[1]
    depth = w_mod.shape[0]
    assert bsz % 2 == 0 and bsz <= 8 and seq % GRID_W == 0 and S5_T * S5_N == GROUP_W
    r2 = lambda a: a.reshape(1, -1).astype(F32)

    cs_l = _rope_table(seq // GRID_W)
    cs_c = jnp.concatenate([jnp.ones((n_ctx, MLA_ROPE), F32), jnp.zeros((n_ctx, MLA_ROPE), F32)], axis=-1)
    sm = jax.nn.softmax(hg_lower_bounds.astype(F32), axis=1)
    lower = jnp.clip(jnp.cumsum(sm, axis=1) - sm[:, :1], 0.0, 1.0)
    c8 = jnp.concatenate([c.astype(F32), c_ctx.astype(F32)[None], jnp.zeros((7 - bsz, dm), F32)], axis=0)
    zero_state = jnp.zeros((bsz, HG_HEADS, HG_D, HG_D), F32)
    zero_x = jnp.zeros((S5_G, 8, 256), F32)

    xc = ctx
    for l in range(depth):
        last = l == depth - 1
        mod = modvec(c8, w_mod[l].astype(F32), r2(b_mod[l]))
        mod_l = [mod[:bsz, i * dm:(i + 1) * dm][:, None, :] for i in range(N_MOD)]
        mod_c = [mod[bsz:bsz + 1, i * dm:(i + 1) * dm][:, None, :] for i in range(N_MOD)]
        w_in_p, mla_p = _prep_layer(w_in[l], mla_w_uq[l], mla_w_ukv[l], mla_q_a_norm[l], mla_kv_a_norm[l],
                                    mla_q_norm[l], mla_k_norm[l])
        pl_ = modproj(x, mod_l[0], mod_l[1], r2(norm1[l]), w_in_p, BF16, tm=1024)
        pc_ = modproj(xc, mod_c[0], mod_c[1], r2(norm1[l]), w_in_p, BF16, tm=1024)

        k_c, v_c, q_c = mla_prep(pc_, cs_c, mla_p, with_q=not last)
        k_l, v_l, q_l = mla_prep(pl_, cs_l, mla_p, with_q=True)
        y_a = attention(q_l, [(k_l, v_l), (k_c, v_c)])

        tabs = s5_tables(s5_lam_re[l], s5_lam_im[l], s5_log_step[l], s5_b_re[l], s5_b_im[l], s5_c_re[l], s5_c_im[l])
        yc_s5, fin = s5_scan(pc_[..., PB_S5U * GROUP_W:(PB_S5U + 1) * GROUP_W], zero_x, tabs)
        yl_s5, _ = s5_scan(pl_[..., PB_S5U * GROUP_W:(PB_S5U + 1) * GROUP_W], fin, tabs)
        glu_w, glu_b = s5_w_glu[l].astype(BF16), r2(s5_b_glu[l])
        y_b = s5_out(yl_s5, pl_, r2(s5_d[l]), glu_w, glu_b)

        la_f, lb_f = r2(jnp.log(lower[0, l])), r2(jnp.log1p(-lower[0, l]))
        la_b, lb_b = r2(jnp.log(lower[1, l])), r2(jnp.log1p(-lower[1, l]))
        oc_f, s_f = hgrn_dir(pc_, la_f, lb_f, zero_state, reverse=False)
        oc_b, s_b = hgrn_dir(pc_, la_b, lb_b, zero_state, reverse=True)
        ol_f, _ = hgrn_dir(pl_, la_f, lb_f, s_f, reverse=False)
        ol_b, _ = hgrn_dir(pl_, la_b, lb_b, s_b, reverse=True)
        y_c = hg_out(ol_f, ol_b, pl_, r2(hg_o_norm[l]))

        hy_args = (hy_conv_w[l].astype(F32), r2(hy_conv_b[l]), hy_bias[l].astype(F32))
        spec_l = hyena_spectrum(seq, hy_w1[l], hy_b1[l], hy_w2[l], hy_b2[l], hy_w3[l])
        y_d = hyena_mixer(pl_, spec_l, *hy_args)

        w_out_b = w_out[l].astype(BF16)
        w1_b, w2_b = mlp_w1[l].astype(BF16), mlp_w2[l].astype(BF16)
        x = mix_out([y_a, y_b, y_c], y_d, w_out_b, x, mod_l[2])
        h1 = modproj(x, mod_l[3], mod_l[4], r2(norm2[l]), w1_b, BF16, act="relu2", tm=1024)
        x = mlp_out(h1, w2_b, x, mod_l[5])

        if not last:
            y_ac = attention(q_c, [(k_c, v_c)])
            y_bc = s5_out(yc_s5, pc_, r2(s5_d[l]), glu_w, glu_b)
            y_cc = hg_out(oc_f, oc_b, pc_, r2(hg_o_norm[l]))
            spec_c = hyena_spectrum(n_ctx, hy_w1[l], hy_b1[l], hy_w2[l], hy_b2[l], hy_w3[l])
            y_dc = hyena_mixer(pc_, spec_c, *hy_args)
            xc = mix_out([y_ac, y_bc, y_cc], y_dc, w_out_b, xc, mod_c[2])
            h1c = modproj(xc, mod_c[3], mod_c[4], r2(norm2[l]), w1_b, BF16, act="relu2", tm=1024)
            xc = mlp_out(h1c, w2_b, xc, mod_c[5])
    return x
```

```python
import functools
import math

import numpy as np
import jax
import jax.numpy as jnp
from jax import lax
from jax.experimental import pallas as pl
from jax.experimental.pallas import tpu as pltpu

F32 = jnp.float32
BF16 = jnp.bfloat16
EPS = 1e-6

GRID_W = 64
GROUP_W = 512
N_MOD = 6
MLA_NOPE = 128
MLA_ROPE = 64
MLA_QK = MLA_NOPE + MLA_ROPE
MLA_HEADS = 4
KV_LORA = 256
MLA_SCALE = 1.0 / math.sqrt(MLA_QK)
ROPE_BASE = 10000.0
S5_G = 32
S5_N = 16
S5_P = 64
S5_T = 32
HG_HEADS = 4
HG_D = 128
HG_SUB = 16
HY_EMB = 33
HY_BANDS = 16
HY_HIDDEN = 64
HY_TARGET = 1e-2
HY_FAST = 0.3
HY_SLOW = 1.5
HY_SHIFT = 0.05

LANES = 128
VMEM_LIMIT_MB = 56

PB_S5U, PB_HGF, PB_HGB, PB_HGI, PB_CQ, PB_HGQ, PB_HGG, PB_HY, PB_KV = 0, 1, 2, 3, 4, 5, 6, 7, 10
N_PROJ = 11 * GROUP_W


def _pc(kernel, out_shape, grid, in_specs, out_specs, scratch=(), vmem_mb=VMEM_LIMIT_MB, name=None):
    return pl.pallas_call(
        kernel, out_shape=out_shape, grid=grid, in_specs=in_specs, out_specs=out_specs,
        scratch_shapes=list(scratch), name=name,
        compiler_params=pltpu.CompilerParams(
            dimension_semantics=("arbitrary",) * len(grid), vmem_limit_bytes=vmem_mb << 20))


def _tile(n, pref):
    t = min(n, pref)
    while n % t:
        t //= 2
    return t


def _silu(x):
    return x * jax.nn.sigmoid(x)


def _modvec_kernel(c_ref, w_ref, b_ref, o_ref):
    s = _silu(c_ref[...])
    o_ref[...] = jnp.dot(s.astype(BF16), w_ref[...].astype(BF16), preferred_element_type=F32) + b_ref[...]


def modvec(c8, w, b):
    d, n = w.shape
    tn = _tile(n, 1536)
    return _pc(_modvec_kernel, jax.ShapeDtypeStruct((8, n), F32), (n // tn,),
               [pl.BlockSpec((8, d), lambda j: (0, 0)),
                pl.BlockSpec((d, tn), lambda j: (0, j)),
                pl.BlockSpec((1, tn), lambda j: (0, j))],
               pl.BlockSpec((8, tn), lambda j: (0, j)), name="modvec")(c8, w, b)


def _modproj_kernel(x_ref, sh_ref, sc_ref, g_ref, w_ref, o_ref, h_ref, *, act):
    @pl.when(pl.program_id(2) == 0)
    def _():
        x = x_ref[...]
        ms = jnp.mean(x * x, axis=-1, keepdims=True)
        y = x * lax.rsqrt(ms + EPS) * g_ref[...]
        h_ref[...] = (y * (1.0 + sc_ref[...]) + sh_ref[...]).astype(BF16)

    acc = jnp.dot(h_ref[...], w_ref[...], preferred_element_type=F32)
    if act == "relu2":
        acc = jnp.square(jnp.maximum(acc, 0.0))
    o_ref[...] = acc.astype(o_ref.dtype)


def modproj(x, shift, scale, g, w, out_dtype, act=None, tm=512, tn=512):
    b, l, d = x.shape
    n = w.shape[1]
    tm, tn = _tile(l, tm), _tile(n, tn)
    per_b = shift.shape[0] == b
    mod_map = (lambda bi, i, j: (bi, 0, 0)) if per_b else (lambda bi, i, j: (0, 0, 0))
    return _pc(functools.partial(_modproj_kernel, act=act),
               jax.ShapeDtypeStruct((b, l, n), out_dtype), (b, l // tm, n // tn),
               [pl.BlockSpec((None, tm, d), lambda bi, i, j: (bi, i, 0)),
                pl.BlockSpec((None, 1, d), mod_map),
                pl.BlockSpec((None, 1, d), mod_map),
                pl.BlockSpec((1, d), lambda bi, i, j: (0, 0)),
                pl.BlockSpec((d, tn), lambda bi, i, j: (0, j))],
               pl.BlockSpec((None, tm, tn), lambda bi, i, j: (bi, i, j)),
               scratch=[pltpu.VMEM((tm, d), BF16)], name="modproj_" + (act or "lin"))(x, shift, scale, g, w)


def _mmres_kernel(*refs, n_lhs, kw):
    a_refs = refs[:n_lhs]
    w_ref, res_ref, gate_ref, o_ref = refs[n_lhs:]
    acc = None
    for i, a in enumerate(a_refs):
        d = jnp.dot(a[...], w_ref[i * kw:(i + 1) * kw, :], preferred_element_type=F32)
        acc = d if acc is None else acc + d
    o_ref[...] = res_ref[...] + gate_ref[...] * acc


def mix_out(ys, yd_hlay, w, res, gate, tm=512, tn=1024):
    b, l, kw = ys[0].shape
    n = w.shape[1]
    tm, tn = _tile(l, tm), _tile(n, tn)
    per_b = gate.shape[0] == b
    gmap = (lambda bi, i, j: (bi, 0, j)) if per_b else (lambda bi, i, j: (0, 0, j))
    y_spec = pl.BlockSpec((None, tm, kw), lambda bi, i, j: (bi, i, 0))
    yd_spec = pl.BlockSpec((None, tm, kw), lambda bi, i, j: (bi % 2, i, bi // 2))
    return _pc(functools.partial(_mmres_kernel, n_lhs=4, kw=kw),
               jax.ShapeDtypeStruct(res.shape, F32), (b, l // tm, n // tn),
               [y_spec, y_spec, y_spec, yd_spec,
                pl.BlockSpec((4 * kw, tn), lambda bi, i, j: (0, j)),
                pl.BlockSpec((None, tm, tn), lambda bi, i, j: (bi, i, j)),
                pl.BlockSpec((None, 1, tn), gmap)],
               pl.BlockSpec((None, tm, tn), lambda bi, i, j: (bi, i, j)), name="mix_out")(*ys, yd_hlay, w, res, gate)


def _mmresk_kernel(a_ref, w_ref, res_ref, gate_ref, o_ref, acc_ref):
    k = pl.program_id(3)

    @pl.when(k == 0)
    def _():
        acc_ref[...] = jnp.zeros_like(acc_ref)

    acc_ref[...] += jnp.dot(a_ref[...], w_ref[...], preferred_element_type=F32)

    @pl.when(k == pl.num_programs(3) - 1)
    def _():
        o_ref[...] = res_ref[...] + gate_ref[...] * acc_ref[...]


def mlp_out(a, w, res, gate, tm=1024, tn=1024, tk=1024):
    b, l, kk = a.shape
    n = w.shape[1]
    tm, tn, tk = _tile(l, tm), _tile(n, tn), _tile(kk, tk)
    per_b = gate.shape[0] == b
    gmap = (lambda bi, i, j, k: (bi, 0, j)) if per_b else (lambda bi, i, j, k: (0, 0, j))
    return _pc(_mmresk_kernel, jax.ShapeDtypeStruct(res.shape, F32), (b, l // tm, n // tn, kk // tk),
               [pl.BlockSpec((None, tm, tk), lambda bi, i, j, k: (bi, i, k)),
                pl.BlockSpec((tk, tn), lambda bi, i, j, k: (k, j)),
                pl.BlockSpec((None, tm, tn), lambda bi, i, j, k: (bi, i, j)),
                pl.BlockSpec((None, 1, tn), gmap)],
               pl.BlockSpec((None, tm, tn), lambda bi, i, j, k: (bi, i, j)),
               scratch=[pltpu.VMEM((tm, tn), F32)], name="mlp_out")(a, w, res, gate)


def _mla_prep_kernel(kv_ref, cq_ref, cs_ref, gkv_ref, gqa_ref, gkn_ref, gkt_ref, gqn_ref, gqt_ref,
                     wkv_ref, wq_ref, k_ref, v_ref, q_ref):
    lane = lax.broadcasted_iota(jnp.int32, (1, LANES), 1)
    low = (lane < MLA_ROPE).astype(F32)
    cs = cs_ref[...]
    kv = kv_ref[...].astype(F32)
    ckv = kv[:, :KV_LORA]
    ckvn = ckv * lax.rsqrt(jnp.mean(ckv * ckv, axis=-1, keepdims=True) + EPS) * gkv_ref[...]
    kvp = jnp.dot(ckvn.astype(BF16), wkv_ref[...], preferred_element_type=F32)
    pe2 = kv[:, KV_LORA:KV_LORA + LANES]
    ss_pe = jnp.sum(pe2 * pe2 * low, axis=-1, keepdims=True)
    pe_tab = cs * gkt_ref[...]
    for h in range(MLA_HEADS):
        knope = kvp[:, 256 * h:256 * h + 128]
        rs = lax.rsqrt((jnp.sum(knope * knope, axis=-1, keepdims=True) + ss_pe) * (1.0 / MLA_QK) + EPS)
        t = pe2 * pe_tab * rs
        t = (t + pltpu.roll(t, MLA_ROPE, axis=1)) * low
        k_ref[h, :, 0:128] = (knope * gkn_ref[...] * rs).astype(BF16)
        k_ref[h, :, 128:256] = t.astype(BF16)
        v_ref[h] = kvp[:, 256 * h + 128:256 * h + 256].astype(BF16)
    if q_ref is not None:
        cq = cq_ref[...].astype(F32)
        cqn = cq * lax.rsqrt(jnp.mean(cq * cq, axis=-1, keepdims=True) + EPS) * gqa_ref[...]
        qp = jnp.dot(cqn.astype(BF16), wq_ref[...], preferred_element_type=F32)
        q_tab = cs * gqt_ref[...]
        for h in range(MLA_HEADS):
            qn = qp[:, 256 * h:256 * h + 128]
            qe = qp[:, 256 * h + 128:256 * h + 256]
            ss = jnp.sum(qn * qn, axis=-1, keepdims=True) + jnp.sum(qe * qe * low, axis=-1, keepdims=True)
            rs = lax.rsqrt(ss * (1.0 / MLA_QK) + EPS) * MLA_SCALE
            t = qe * q_tab * rs
            t = t + pltpu.roll(t, MLA_ROPE, axis=1)
            q_ref[h, :, 0:128] = (qn * gqn_ref[...] * rs).astype(BF16)
            q_ref[h, :, 128:256] = t.astype(BF16)


def _mla_prep_kernel_noq(kv_ref, cs_ref, gkv_ref, gkn_ref, gkt_ref, wkv_ref, k_ref, v_ref):
    _mla_prep_kernel(kv_ref, None, cs_ref, gkv_ref, None, gkn_ref, gkt_ref, None, None, wkv_ref, None,
                     k_ref, v_ref, None)


def mla_prep(proj, cs, p, with_q, tm=512):
    b, l, _ = proj.shape
    tm = _tile(l, tm)
    row = lambda w: pl.BlockSpec((1, w), lambda bi, i: (0, 0))
    kv_spec = pl.BlockSpec((None, tm, GROUP_W), lambda bi, i: (bi, i, PB_KV))
    cq_spec = pl.BlockSpec((None, tm, GROUP_W), lambda bi, i: (bi, i, PB_CQ))
    cs_spec = pl.BlockSpec((tm, LANES), lambda bi, i: (i, 0))
    wkv_spec = pl.BlockSpec((KV_LORA, 1024), lambda bi, i: (0, 0))
    wq_spec = pl.BlockSpec((GROUP_W, 1024), lambda bi, i: (0, 0))
    hd = lambda w: pl.BlockSpec((None, MLA_HEADS, tm, w), lambda bi, i: (bi, 0, i, 0))
    k_sh = jax.ShapeDtypeStruct((b, MLA_HEADS, l, 256), BF16)
    v_sh = jax.ShapeDtypeStruct((b, MLA_HEADS, l, 128), BF16)
    if with_q:
        return _pc(_mla_prep_kernel, (k_sh, v_sh, k_sh), (b, l // tm),
                   [kv_spec, cq_spec, cs_spec, row(KV_LORA), row(GROUP_W), row(128), row(128), row(128), row(128),
                    wkv_spec, wq_spec],
                   (hd(256), hd(128), hd(256)), name="mla_prep")(
            proj, proj, cs, p["gkv"], p["gqa"], p["gkn"], p["gkt"], p["gqn"], p["gqt"], p["wkv"], p["wq"])
    k, v = _pc(_mla_prep_kernel_noq, (k_sh, v_sh), (b, l // tm),
               [kv_spec, cs_spec, row(KV_LORA), row(128), row(128), wkv_spec],
               (hd(256), hd(128)), name="mla_prep_kv")(proj, cs, p["gkv"], p["gkn"], p["gkt"], p["wkv"])
    return k, v, None


def _attn_kernel(*refs, n_seg, n_part):
    q_ref = refs[0]
    o_ref = refs[1 + 2 * n_seg]
    rows = q_ref.shape[0] // n_part
    for part in range(n_part):
        sl = slice(part * rows, (part + 1) * rows)
        q = q_ref[sl, :]
        scores = [lax.dot_general(q, refs[1 + 2 * i][...], (((1,), (1,)), ((), ())), preferred_element_type=F32)
                  for i in range(n_seg)]
        m = scores[0].max(axis=-1, keepdims=True)
        for s in scores[1:]:
            m = jnp.maximum(m, s.max(axis=-1, keepdims=True))
        den, acc = None, None
        for i, s in enumerate(scores):
            p = jnp.exp(s - m)
            d = p.sum(axis=-1, keepdims=True)
            a = jnp.dot(p.astype(BF16), refs[2 + 2 * i][...], preferred_element_type=F32)
            den = d if den is None else den + d
            acc = a if acc is None else acc + a
        o_ref[sl, :] = (acc / den).astype(o_ref.dtype)


def attention(q, kvs, tq=1024, part_rows=256):
    b, h, l, _ = q.shape
    tq = _tile(l, tq)
    n_part = max(1, tq // part_rows)
    specs = [pl.BlockSpec((None, None, tq, 256), lambda bi, hi, i: (bi, hi, i, 0))]
    args = [q]
    for k, v in kvs:
        lk = k.shape[2]
        specs += [pl.BlockSpec((None, None, lk, 256), lambda bi, hi, i: (bi, hi, 0, 0)),
                  pl.BlockSpec((None, None, lk, 128), lambda bi, hi, i: (bi, hi, 0, 0))]
        args += [k, v]
    return _pc(functools.partial(_attn_kernel, n_seg=len(kvs), n_part=n_part),
               jax.ShapeDtypeStruct((b, l, h * 128), BF16), (b, h, l // tq), specs,
               pl.BlockSpec((None, tq, 128), lambda bi, hi, i: (bi, i, hi)), name="attention")(*args)


def s5_tables(lam_re, lam_im, log_step, b_re, b_im, c_re, c_im):
    t = S5_T
    step = jnp.exp(log_step.astype(F32))[..., None]
    re, im = lam_re.astype(F32) * step, lam_im.astype(F32) * step
    d = jnp.arange(t + 1, dtype=F32)[:, None, None, None]
    mag = jnp.exp(d * re)
    ar, ai = mag * jnp.cos(d * im), mag * jnp.sin(d * im)
    lr, li = lam_re.astype(F32), lam_im.astype(F32)
    den = lr * lr + li * li
    er, ei = ar[1] - 1.0, ai[1]
    cr, ci = (er * lr + ei * li) / den, (ei * lr - er * li) / den
    br, bi = b_re.astype(F32), b_im.astype(F32)
    bbr = cr[..., None] * br - ci[..., None] * bi
    bbi = cr[..., None] * bi + ci[..., None] * br
    ccr, cci = c_re.astype(F32), c_im.astype(F32)
    car = ccr[None] * ar[:, :, :, None, :] - cci[None] * ai[:, :, :, None, :]
    cai = ccr[None] * ai[:, :, :, None, :] + cci[None] * ar[:, :, :, None, :]
    kd = jnp.einsum("dxgnp,xgpm->xgdnm", car[:t], bbr) - jnp.einsum("dxgnp,xgpm->xgdnm", cai[:t], bbi)
    kcat = jnp.stack([kd[0], kd[1][:, ::-1]], axis=1).transpose(0, 1, 4, 2, 3).reshape(S5_G, 2, S5_N, t * S5_N)
    pw = (jnp.arange(t - 1, -1, -1), jnp.arange(t))
    w_re, w_im = [], []
    for x in range(2):
        a_r, a_i = ar[pw[x], x], ai[pw[x], x]
        w_re.append(a_r[..., None] * bbr[x][None] - a_i[..., None] * bbi[x][None])
        w_im.append(a_r[..., None] * bbi[x][None] + a_i[..., None] * bbr[x][None])
    ws = jnp.concatenate(w_re + w_im, axis=2).transpose(1, 0, 3, 2).reshape(S5_G, t * S5_N, 4 * S5_P)
    pv = (jnp.arange(1, t + 1), jnp.arange(t, 0, -1))
    vm = jnp.concatenate([car[pv[0], 0], car[pv[1], 1], -cai[pv[0], 0], -cai[pv[1], 1]], axis=-1)
    vm = vm.transpose(1, 3, 0, 2).reshape(S5_G, 4 * S5_P, t * S5_N)
    dec = jnp.stack([jnp.concatenate([ar[t, 0], ar[t, 1]], axis=-1),
                     jnp.concatenate([ai[t, 0], ai[t, 1]], axis=-1)], axis=1)
    return kcat, ws.astype(BF16), vm.astype(BF16), dec


def _s5a_kernel(u_ref, k_ref, ws_ref, y_ref, s_ref, m_ref):
    kf, kb = k_ref[0], k_ref[1]
    lane = lax.broadcasted_iota(jnp.int32, kf.shape, 1)
    for s in range(S5_T):
        lo, hi = S5_N * s, S5_N * (s + 1)
        f = kf if s == 0 else jnp.where(lane >= lo, pltpu.roll(kf, lo, axis=1), 0.0)
        bk = kb if s == S5_T - 1 else jnp.where(lane < hi, pltpu.roll(kb, hi, axis=1), 0.0)
        m_ref[lo:hi, :] = (f + bk).astype(BF16)
    u = u_ref[...]
    y_ref[...] = jnp.dot(u, m_ref[...], preferred_element_type=F32)
    s_ref[...] = jnp.dot(u, ws_ref[...], preferred_element_type=F32)


def _s5b_kernel(yi_ref, s_ref, x0_ref, dec_ref, vm_ref, y_ref, xf_ref, xa_ref, xb_ref, *, nchunk, nbatch):
    ar = dec_ref[0:1, :]
    ai = dec_ref[1:2, :]
    fwd_lane = lax.broadcasted_iota(jnp.int32, (1, 128), 1) < S5_P

    def step(j, carry):
        out = []
        for b in range(nbatch):
            xr, xi = carry[2 * b], carry[2 * b + 1]
            rf = pl.ds(b * nchunk + j, 1)
            rb = pl.ds(b * nchunk + (nchunk - 1 - j), 1)
            x = jnp.concatenate([xr, xi], axis=1)
            xa_ref[rf, :] = x
            xb_ref[rb, :] = x
            sf, sb = s_ref[rf, :], s_ref[rb, :]
            sr = jnp.where(fwd_lane, sf[:, :128], sb[:, :128])
            si = jnp.where(fwd_lane, sf[:, 128:], sb[:, 128:])
            out += [ar * xr - ai * xi + sr, ar * xi + ai * xr + si]
        return tuple(out)

    x0 = x0_ref[...]
    init = []
    for b in range(nbatch):
        init += [x0[b:b + 1, :128], x0[b:b + 1, 128:]]
    fin = lax.fori_loop(0, nchunk, step, tuple(init), unroll=2)
    xf_ref[...] = jnp.zeros_like(xf_ref)
    for b in range(nbatch):
        xf_ref[b:b + 1, :] = jnp.concatenate([fin[2 * b], fin[2 * b + 1]], axis=1)
    pick = (lax.broadcasted_iota(jnp.int32, xa_ref.shape, 1) % 128) < S5_P
    xin = jnp.where(pick, xa_ref[...], xb_ref[...])
    y = yi_ref[...] + jnp.dot(xin.astype(BF16), vm_ref[...], preferred_element_type=F32)
    y_ref[...] = y.astype(y_ref.dtype)


def s5_scan(u, x0, tabs):
    kcat, ws, vm, dec = tabs
    b, l, _ = u.shape
    j = l // S5_T
    r = b * j
    ug = u.astype(BF16).reshape(b, j, S5_T, S5_G, S5_N).transpose(3, 0, 1, 2, 4).reshape(S5_G, r, GROUP_W)
    g3 = lambda w: pl.BlockSpec((None, r, w), lambda g: (g, 0, 0))
    yi, s = _pc(_s5a_kernel,
                (jax.ShapeDtypeStruct((S5_G, r, GROUP_W), F32), jax.ShapeDtypeStruct((S5_G, r, 256), F32)),
                (S5_G,),
                [g3(GROUP_W), pl.BlockSpec((None, 2, S5_N, GROUP_W), lambda g: (g, 0, 0, 0)),
                 pl.BlockSpec((None, GROUP_W, 256), lambda g: (g, 0, 0))],
                (g3(GROUP_W), g3(256)),
                scratch=[pltpu.VMEM((GROUP_W, GROUP_W), BF16)], name="s5a")(ug, kcat, ws)
    y, xf = _pc(functools.partial(_s5b_kernel, nchunk=j, nbatch=b),
                (jax.ShapeDtypeStruct((S5_G, r, GROUP_W), BF16), jax.ShapeDtypeStruct((S5_G, 8, 256), F32)),
                (S5_G,),
                [g3(GROUP_W), g3(256),
                 pl.BlockSpec((None, 8, 256), lambda g: (g, 0, 0)),
                 pl.BlockSpec((None, 2, 128), lambda g: (g, 0, 0)),
                 pl.BlockSpec((None, 256, GROUP_W), lambda g: (g, 0, 0))],
                (g3(GROUP_W), pl.BlockSpec((None, 8, 256), lambda g: (g, 0, 0))),
                scratch=[pltpu.VMEM((r, 256), F32), pltpu.VMEM((r, 256), F32)], name="s5b")(yi, s, x0, dec, vm)
    y = y.reshape(S5_G, b, j, S5_T, S5_N).transpose(1, 2, 3, 0, 4).reshape(b, l, GROUP_W)
    return y, xf


def _s5_out_kernel(y_ref, u_ref, d_ref, w_ref, b_ref, o_ref):
    y = jax.nn.gelu(y_ref[...].astype(F32) + d_ref[...] * u_ref[...].astype(F32))
    z = jnp.dot(y.astype(BF16), w_ref[...], preferred_element_type=F32) + b_ref[...]
    o_ref[...] = (y * jax.nn.sigmoid(z)).astype(o_ref.dtype)


def s5_out(y, proj, d, w, bias, tm=512):
    b, l, _ = y.shape
    tm = _tile(l, tm)
    row = pl.BlockSpec((1, GROUP_W), lambda bi, i: (0, 0))
    return _pc(_s5_out_kernel, jax.ShapeDtypeStruct((b, l, GROUP_W), BF16), (b, l // tm),
               [pl.BlockSpec((None, tm, GROUP_W), lambda bi, i: (bi, i, 0)),
                pl.BlockSpec((None, tm, GROUP_W), lambda bi, i: (bi, i, PB_S5U)),
                row, pl.BlockSpec((GROUP_W, GROUP_W), lambda bi, i: (0, 0)), row],
               pl.BlockSpec((None, tm, GROUP_W), lambda bi, i: (bi, i, 0)), name="s5_out")(y, proj, d, w, bias)


def _hgrn_kernel(z_ref, i_ref, q_ref, la_ref, lb_ref, s0_ref, o_ref, sfin_ref, st_ref, *, reverse, nsub):
    c = HG_SUB

    @pl.when(pl.program_id(1) == 0)
    def _():
        st_ref[...] = s0_ref[...]

    hlf = c // 2
    ones = jnp.ones((HG_D, HG_D), BF16)
    row = lax.broadcasted_iota(jnp.int32, (c, HG_D), 0)
    row8 = lax.broadcasted_iota(jnp.int32, (hlf, HG_D), 0)

    def sub(jj, carry):
        j = (nsub - 1 - jj) if reverse else jj
        rows = pl.ds(pl.multiple_of(j * c, c), c)
        for h in range(HG_HEADS):
            sl = slice(HG_D * h, HG_D * (h + 1))
            z = z_ref[rows, sl].astype(F32)
            v = i_ref[rows, sl].astype(F32)
            q = _silu(q_ref[rows, sl].astype(F32))
            a = la_ref[:, sl]
            bb = lb_ref[:, sl] + (jnp.minimum(z, 0.0) - jnp.log1p(jnp.exp(-jnp.abs(z))))
            g = jnp.maximum(a, bb) + jnp.log1p(jnp.exp(-jnp.abs(a - bb)))
            k = 1.0 - jnp.exp(g)
            gc = g
            for sh in (1, 2, 4, 8):
                if reverse:
                    gc = gc + jnp.where(row < c - sh, pltpu.roll(gc, c - sh, axis=0), 0.0)
                else:
                    gc = gc + jnp.where(row >= sh, pltpu.roll(gc, sh, axis=0), 0.0)
            blocks, meta = [], []
            for s in range(c):
                for half in range(2):
                    r0 = half * hlf
                    if (r0 > s) if reverse else (r0 + hlf - 1 < s):
                        continue
                    e = gc[r0:r0 + hlf, :] - gc[s:s + 1, :]
                    if not ((r0 + hlf - 1 <= s) if reverse else (r0 >= s)):
                        keep = (row8 + r0 <= s) if reverse else (row8 + r0 >= s)
                        e = jnp.where(keep, e, -1e30)
                    blocks.append(q[r0:r0 + hlf, :] * jnp.exp(e) * k[s:s + 1, :])
                    meta.append((half, s))
            rsum = jnp.dot(jnp.concatenate(blocks, axis=0).astype(BF16), ones, preferred_element_type=F32)
            halves = [jnp.zeros((hlf, HG_D), F32), jnp.zeros((hlf, HG_D), F32)]
            for i, (half, s) in enumerate(meta):
                halves[half] = halves[half] + rsum[hlf * i:hlf * (i + 1), :] * v[s:s + 1, :]
            o = jnp.concatenate(halves, axis=0)
            st = st_ref[h]
            o = o + lax.dot_general((q * jnp.exp(gc)).astype(BF16), st.astype(BF16),
                                    (((1,), (1,)), ((), ())), preferred_element_type=F32)
            gl = gc[0:1, :] if reverse else gc[c - 1:c, :]
            kd = k * jnp.exp(gl - gc)
            ds = lax.dot_general(v.astype(BF16), kd.astype(BF16), (((0,), (0,)), ((), ())),
                                 preferred_element_type=F32)
            st_ref[h] = st * jnp.exp(gl) + ds
            o_ref[rows, sl] = o
        return carry

    lax.fori_loop(0, nsub, sub, 0, unroll=4)

    @pl.when(pl.program_id(1) == pl.num_programs(1) - 1)
    def _():
        sfin_ref[...] = st_ref[...]


def hgrn_dir(proj, la, lb1, s0, reverse, tl=256):
    b, l, _ = proj.shape
    tl = _tile(l, tl)
    nb = l // tl
    blk = (lambda i: nb - 1 - i) if reverse else (lambda i: i)
    col = lambda c: pl.BlockSpec((None, tl, GROUP_W), lambda bi, i: (bi, blk(i), c))
    row = pl.BlockSpec((1, GROUP_W), lambda bi, i: (0, 0))
    st = pl.BlockSpec((None, HG_HEADS, HG_D, HG_D), lambda bi, i: (bi, 0, 0, 0))
    return _pc(functools.partial(_hgrn_kernel, reverse=reverse, nsub=tl // HG_SUB),
               (jax.ShapeDtypeStruct((b, l, GROUP_W), F32), jax.ShapeDtypeStruct(s0.shape, F32)),
               (b, nb),
               [col(PB_HGB if reverse else PB_HGF), col(PB_HGI), col(PB_HGQ), row, row, st],
               (pl.BlockSpec((None, tl, GROUP_W), lambda bi, i: (bi, blk(i), 0)), st),
               scratch=[pltpu.VMEM((HG_HEADS, HG_D, HG_D), F32)],
               name="hgrn_bwd" if reverse else "hgrn_fwd")(proj, proj, proj, la, lb1, s0)


def _hg_out_kernel(of_ref, ob_ref, g_ref, gn_ref, y_ref):
    o = of_ref[...] + ob_ref[...]
    gate = _silu(g_ref[...].astype(F32))
    for h in range(HG_HEADS):
        sl = slice(HG_D * h, HG_D * (h + 1))
        oh = o[:, sl]
        y = oh * lax.rsqrt(jnp.mean(oh * oh, axis=-1, keepdims=True) + EPS) * gn_ref[...]
        y_ref[:, sl] = (y * gate[:, sl]).astype(y_ref.dtype)


def hg_out(of, ob, proj, gn, tm=512):
    b, l, _ = of.shape
    tm = _tile(l, tm)
    t3 = lambda c: pl.BlockSpec((None, tm, GROUP_W), lambda bi, i: (bi, i, c))
    return _pc(_hg_out_kernel, jax.ShapeDtypeStruct((b, l, GROUP_W), BF16), (b, l // tm),
               [t3(0), t3(0), t3(PB_HGG), pl.BlockSpec((1, HG_D), lambda bi, i: (0, 0))],
               t3(0), name="hg_out")(of, ob, proj, gn)


def _fft_factors(n_fft):
    n1 = {8192: 64, 512: 32, 1024: 32, 2048: 32, 4096: 64}[n_fft]
    return n1, n_fft // n1


def _dft_tables(na, nb, ka, mb, inverse):
    n = na * nb
    sgn = 1.0 if inverse else -1.0

    def stacked(phase, scale=1.0):
        ang = sgn * 2.0 * np.pi * (phase % n) / n
        fr, fi = np.cos(ang) * scale, np.sin(ang) * scale
        return np.concatenate([np.concatenate([fr, -fi], axis=-1), np.concatenate([fi, fr], axis=-1)], axis=-2)

    fa = stacked(np.outer(np.arange(na), np.arange(ka)) * nb)
    p, q, b = np.arange(na)[:, None, None], np.arange(mb)[None, :, None], np.arange(nb)[None, None, :]
    fb = stacked(b * p + b * q * na, (1.0 / n) if inverse else 1.0)
    return jnp.asarray(fa, BF16), jnp.asarray(fb, BF16)


def _fft_a_kernel(z_ref, f_ref, o_ref, *, g, na, is_complex):
    f = f_ref[...]
    for i in range(g):
        z = z_ref[:, i] if is_complex else z_ref[i]
        if is_complex:
            z = z.reshape(z.shape[0] * z.shape[1], z.shape[2])
        r = jnp.dot(f, z, preferred_element_type=F32)
        o_ref[0, i] = r[:na]
        o_ref[1, i] = r[na:]


def fft_stage_a(z, fa, g=8):
    is_complex = z.ndim == 4
    nb, ka, pc = z.shape[-3:]
    na = fa.shape[0] // 2
    g = min(g, nb)
    zspec = (pl.BlockSpec((2, g, ka, pc), lambda j: (0, j, 0, 0)) if is_complex
             else pl.BlockSpec((g, ka, pc), lambda j: (j, 0, 0)))
    return _pc(functools.partial(_fft_a_kernel, g=g, na=na, is_complex=is_complex),
               jax.ShapeDtypeStruct((2, nb, na, pc), F32), (nb // g,),
               [zspec, pl.BlockSpec(fa.shape, lambda j: (0, 0))],
               pl.BlockSpec((2, g, na, pc), lambda j: (0, j, 0, 0)), name="fft_a")(z, fa)


def _fft_b_kernel(*refs, mode, mb, kb):
    a_ref, f_ref = refs[0], refs[1]
    o_ref = refs[-1]
    nb = a_ref.shape[1]
    for i in range(kb):
        f = f_ref[i].astype(F32)
        z = (jnp.dot(f[:, :nb], a_ref[0, :, i, :], preferred_element_type=F32)
             + jnp.dot(f[:, nb:], a_ref[1, :, i, :], preferred_element_type=F32))
        zr, zi = z[:mb], z[mb:]
        if mode == "filter":
            h_ref = refs[2]
            hr, hi = h_ref[0, i].astype(F32), h_ref[1, i].astype(F32)
            zr, zi = zr * hr - zi * hi, zr * hi + zi * hr
        elif mode == "final":
            v_ref, g_ref, b_ref = refs[2], refs[3], refs[4]
            zr = g_ref[0, i].astype(F32) * (zr + v_ref[0, i].astype(F32) * b_ref[...])
            zi = g_ref[1, i].astype(F32) * (zi + v_ref[1, i].astype(F32) * b_ref[...])
        o_ref[0, i] = zr.astype(o_ref.dtype)
        o_ref[1, i] = zi.astype(o_ref.dtype)


def fft_stage_b(a, fb, mode, extra=(), order=0, kb=8):
    _, nb, na, pc = a.shape
    mb = fb.shape[1] // 2
    c = GROUP_W
    kb = min(kb, na)
    blk = pl.BlockSpec((2, kb, mb, c), lambda k, p: (0, k, 0, p))
    specs = [pl.BlockSpec((2, nb, kb, c), lambda k, p: (0, 0, k, p)),
             pl.BlockSpec((kb, 2 * mb, 2 * nb), lambda k, p: (k, 0, 0))]
    args = [a, fb]
    if mode == "filter":
        (h,) = extra
        specs.append(pl.BlockSpec((2, kb, mb, c), lambda k, p: (0, k, 0, order)))
        args.append(h)
    elif mode == "final":
        v, gate, bias = extra
        specs += [blk, blk, pl.BlockSpec((1, c), lambda k, p: (0, 0))]
        args += [v, gate, bias]
    return _pc(functools.partial(_fft_b_kernel, mode=mode, mb=mb, kb=kb),
               jax.ShapeDtypeStruct((2, na, mb, pc), BF16), (na // kb, pc // c), specs, blk,
               name="fft_b_" + mode)(*args)


def _hy_filter_kernel(emb_ref, w1_ref, b1_ref, w2_ref, b2_ref, w3f_ref, w3b_ref, win_ref, o_ref, h_ref, *, na):
    hp = lax.Precision.HIGHEST

    @pl.when(pl.program_id(0) == 0)
    def _():
        h = jnp.sin(jnp.dot(emb_ref[...], w1_ref[...], precision=hp, preferred_element_type=F32) + b1_ref[...])
        h_ref[...] = jnp.sin(jnp.dot(h, w2_ref[...], precision=hp, preferred_element_type=F32) + b2_ref[...])

    h = h_ref[...]
    hf = jnp.dot(h, w3f_ref[...], precision=hp, preferred_element_type=F32)
    hb = jnp.dot(h, w3b_ref[...], precision=hp, preferred_element_type=F32)
    row = lax.broadcasted_iota(jnp.int32, hf.shape, 0)
    taps = jnp.where((row & (na - 1)) < na // 2, hf, hb) * win_ref[...]
    o_ref[...] = (taps / jnp.sum(jnp.abs(taps), axis=0, keepdims=True)).astype(o_ref.dtype)


def hyena_filters(n, w1, b1, w2, b2, w3):
    na, nb = _fft_factors(2 * n)
    t = np.arange(n, dtype=np.float32)
    t_norm = t / max(n - 1, 1)
    bands = np.linspace(1e-4, HY_BANDS - 1, HY_BANDS, dtype=np.float32)
    ang = (2.0 * math.pi * t / n)[:, None] * bands[None, :]
    emb = np.concatenate([t_norm[:, None], np.cos(ang), -np.sin(ang)], axis=-1).astype(np.float32)
    emb = np.pad(emb, ((0, 0), (0, LANES - HY_EMB)))
    deltas = np.linspace(math.log(HY_TARGET) / HY_SLOW, math.log(HY_TARGET) / HY_FAST, GROUP_W, dtype=np.float32)
    win = (np.exp(-t_norm[:, None] * np.abs(deltas)[None, :]) + np.float32(HY_SHIFT)).astype(np.float32)
    back = np.concatenate([[0], np.arange(n - 2, -1, -1)])
    win_b = win[back]
    win_b[0] = 0.0
    order = (np.arange(na)[None, :] * nb + np.arange(nb)[:, None]).reshape(-1)
    emb2 = jnp.asarray(np.concatenate([emb, emb[back]], axis=0)[order])
    win2 = jnp.asarray(np.concatenate([win, win_b], axis=0)[order])
    w1p = jnp.pad(w1.astype(F32), ((0, LANES - HY_EMB), (0, 0)))
    w3r = w3.astype(F32).reshape(HY_HIDDEN, 2, 2, GROUP_W)
    w3f, w3b = w3r[:, :, 0].reshape(HY_HIDDEN, 2 * GROUP_W), w3r[:, :, 1].reshape(HY_HIDDEN, 2 * GROUP_W)
    tc = LANES
    full = lambda a: pl.BlockSpec(a.shape, lambda j: (0, 0))
    colw = pl.BlockSpec((HY_HIDDEN, tc), lambda j: (0, j))
    b1r, b2r = b1.reshape(1, -1).astype(F32), b2.reshape(1, -1).astype(F32)
    w2f = w2.astype(F32)
    return _pc(functools.partial(_hy_filter_kernel, na=na),
               jax.ShapeDtypeStruct((2 * n, 2 * GROUP_W), BF16), (2 * GROUP_W // tc,),
               [full(emb2), full(w1p), full(b1r), full(w2f), full(b2r), colw, colw,
                pl.BlockSpec((2 * n, tc), lambda j: (0, j % (GROUP_W // tc)))],
               pl.BlockSpec((2 * n, tc), lambda j: (0, j)),
               scratch=[pltpu.VMEM((2 * n, HY_HIDDEN), F32)], name="hy_filter")(
        emb2, w1p, b1r, w2f, b2r, w3f, w3b, win2)


def hyena_spectrum(n, w1, b1, w2, b2, w3):
    na, nb = _fft_factors(2 * n)
    circ = hyena_filters(n, w1, b1, w2, b2, w3).reshape(nb, na, 2 * GROUP_W)
    fa, fb = _dft_tables(na, nb, na, nb, inverse=False)
    return fft_stage_b(fft_stage_a(circ, fa[:, :na]), fb, "plain")


def _shortconv_kernel(p_ref, w_ref, b_ref, o_ref):
    p = p_ref[...].astype(F32)
    n = p.shape[0]
    row = lax.broadcasted_iota(jnp.int32, p.shape, 0)
    prev = jnp.where(row >= 1, pltpu.roll(p, 1, axis=0), 0.0)
    nxt = jnp.where(row < n - 1, pltpu.roll(p, n - 1, axis=0), 0.0)
    o_ref[...] = (prev * w_ref[0:1, :] + p * w_ref[1:2, :] + nxt * w_ref[2:3, :] + b_ref[...]).astype(o_ref.dtype)


def hyena_shortconv(proj, w, bias):
    b, n, _ = proj.shape
    return _pc(_shortconv_kernel, jax.ShapeDtypeStruct((3, 2, n, (b // 2) * GROUP_W), BF16), (b, 3),
               [pl.BlockSpec((None, n, GROUP_W), lambda bi, j: (bi, 0, PB_HY + j)),
                pl.BlockSpec((3, GROUP_W), lambda bi, j: (0, j)),
                pl.BlockSpec((1, GROUP_W), lambda bi, j: (0, j))],
               pl.BlockSpec((None, None, n, GROUP_W), lambda bi, j: (j, bi % 2, 0, bi // 2)),
               name="shortconv")(proj, w, bias)


def hyena_mixer(proj, spec, conv_w, conv_b, bias):
    b, n, _ = proj.shape
    na, nb = _fft_factors(2 * n)
    pc = (b // 2) * GROUP_W
    u = hyena_shortconv(proj, conv_w, conv_b).reshape(3, 2, na // 2, nb, pc).swapaxes(2, 3)
    fa, fb = _dft_tables(na, nb, na // 2, nb, inverse=False)
    ga, gb = _dft_tables(nb, na, nb, na // 2, inverse=True)
    z = u[0]
    for o in range(2):
        zf = fft_stage_b(fft_stage_a(z, fa), fb, "filter", (spec,), order=o)
        z = fft_stage_b(fft_stage_a(zf, ga), gb, "final", (z, u[1 + o], bias[o:o + 1]))
    return z.swapaxes(1, 2).reshape(2, n, pc)


def _rot_cols(w):
    return jnp.concatenate([-w[..., 16:32], w[..., 0:16], -w[..., 48:64], w[..., 32:48]], axis=-1)


def _rot_perm(g):
    return jnp.concatenate([g[16:32], g[0:16], g[48:64], g[32:48]])


def _prep_layer(w_in, w_uq, w_ukv, q_a_g, kv_a_g, q_g, k_g):
    d = w_in.shape[0]
    ckv, krope, rest = w_in[:, :256], w_in[:, 256:320], w_in[:, 320:]
    w_in_p = jnp.concatenate([rest, ckv, krope, _rot_cols(krope), jnp.zeros((d, 128), w_in.dtype)], axis=1)
    wq = w_uq.reshape(GROUP_W, MLA_HEADS, MLA_QK)
    wq = jnp.concatenate([wq, _rot_cols(wq[..., MLA_NOPE:])], axis=-1).reshape(GROUP_W, MLA_HEADS * 256)
    r2 = lambda a: a.reshape(1, -1).astype(F32)
    mla = dict(wkv=w_ukv.astype(BF16), wq=wq.astype(BF16), gkv=r2(kv_a_g), gqa=r2(q_a_g),
               gkn=r2(k_g[:MLA_NOPE]), gkt=r2(jnp.concatenate([k_g[MLA_NOPE:], _rot_perm(k_g[MLA_NOPE:])])),
               gqn=r2(q_g[:MLA_NOPE]), gqt=r2(jnp.concatenate([q_g[MLA_NOPE:], _rot_perm(q_g[MLA_NOPE:])])))
    return w_in_p.astype(BF16), mla


def _rope_table(n_rows):
    row = np.repeat(np.arange(n_rows, dtype=np.float32), GRID_W)
    col = np.tile(np.arange(GRID_W, dtype=np.float32), n_rows)
    n_freq = MLA_ROPE // 4
    inv_freq = (ROPE_BASE ** (-np.arange(n_freq, dtype=np.float32) / n_freq)).astype(np.float32)
    ang_r, ang_c = row[:, None] * inv_freq, col[:, None] * inv_freq
    ang = np.concatenate([ang_r, ang_r, ang_c, ang_c], axis=-1)
    return jnp.asarray(np.concatenate([np.cos(ang), np.sin(ang)], axis=-1), F32)


def kernel(x, c, ctx, c_ctx, w_mod, b_mod, norm1, norm2, w_in, w_out, mla_q_a_norm, mla_kv_a_norm, mla_w_uq, mla_w_ukv, mla_q_norm, mla_k_norm, s5_lam_re, s5_lam_im, s5_log_step, s5_b_re, s5_b_im, s5_c_re, s5_c_im, s5_d, s5_w_glu, s5_b_glu, hg_lower_bounds, hg_o_norm, hy_conv_w, hy_conv_b, hy_w1, hy_b1, hy_w2, hy_b2, hy_w3, hy_bias, mlp_w1, mlp_w2):
    bsz, seq, dm = x.shape
    n_ctx = ctx.shape[1]
    depth = w_mod.shape[0]
    assert bsz % 2 == 0 and bsz <= 8 and seq % GRID_W == 0 and S5_T * S5_N == GROUP_W
    r2 = lambda a: a.reshape(1, -1).astype(F32)

    cs_l = _rope_table(seq // GRID_W)
    cs_c = jnp.concatenate([jnp.ones((n_ctx, MLA_ROPE), F32), jnp.zeros((n_ctx, MLA_ROPE), F32)], axis=-1)
    sm = jax.nn.softmax(hg_lower_bounds.astype(F32), axis=1)
    lower = jnp.clip(jnp.cumsum(sm, axis=1) - sm[:, :1], 0.0, 1.0)
    c8 = jnp.concatenate([c.astype(F32), c_ctx.astype(F32)[None], jnp.zeros((7 - bsz, dm), F32)], axis=0)
    zero_state = jnp.zeros((bsz, HG_HEADS, HG_D, HG_D), F32)
    zero_x = jnp.zeros((S5_G, 8, 256), F32)

    xc = ctx
    for l in range(depth):
        last = l == depth - 1
        mod = modvec(c8, w_mod[l].astype(F32), r2(b_mod[l]))
        mod_l = [mod[:bsz, i * dm:(i + 1) * dm][:, None, :] for i in range(N_MOD)]
        mod_c = [mod[bsz:bsz + 1, i * dm:(i + 1) * dm][:, None, :] for i in range(N_MOD)]
        w_in_p, mla_p = _prep_layer(w_in[l], mla_w_uq[l], mla_w_ukv[l], mla_q_a_norm[l], mla_kv_a_norm[l],
                                    mla_q_norm[l], mla_k_norm[l])
        pl_ = modproj(x, mod_l[0], mod_l[1], r2(norm1[l]), w_in_p, BF16, tm=1024)
        pc_ = modproj(xc, mod_c[0], mod_c[1], r2(norm1[l]), w_in_p, BF16, tm=1024)

        k_c, v_c, q_c = mla_prep(pc_, cs_c, mla_p, with_q=not last)
        k_l, v_l, q_l = mla_prep(pl_, cs_l, mla_p, with_q=True)
        y_a = attention(q_l, [(k_l, v_l), (k_c, v_c)])

        tabs = s5_tables(s5_lam_re[l], s5_lam_im[l], s5_log_step[l], s5_b_re[l], s5_b_im[l], s5_c_re[l], s5_c_im[l])
        yc_s5, fin = s5_scan(pc_[..., PB_S5U * GROUP_W:(PB_S5U + 1) * GROUP_W], zero_x, tabs)
        yl_s5, _ = s5_scan(pl_[..., PB_S5U * GROUP_W:(PB_S5U + 1) * GROUP_W], fin, tabs)
        glu_w, glu_b = s5_w_glu[l].astype(BF16), r2(s5_b_glu[l])
        y_b = s5_out(yl_s5, pl_, r2(s5_d[l]), glu_w, glu_b)

        la_f, lb_f = r2(jnp.log(lower[0, l])), r2(jnp.log1p(-lower[0, l]))
        la_b, lb_b = r2(jnp.log(lower[1, l])), r2(jnp.log1p(-lower[1, l]))
        oc_f, s_f = hgrn_dir(pc_, la_f, lb_f, zero_state, reverse=False)
        oc_b, s_b = hgrn_dir(pc_, la_b, lb_b, zero_state, reverse=True)
        ol_f, _ = hgrn_dir(pl_, la_f, lb_f, s_f, reverse=False)
        ol_b, _ = hgrn_dir(pl_, la_b, lb_b, s_b, reverse=True)
        y_c = hg_out(ol_f, ol_b, pl_, r2(hg_o_norm[l]))

        hy_args = (hy_conv_w[l].astype(F32), r2(hy_conv_b[l]), hy_bias[l].astype(F32))
        spec_l = hyena_spectrum(seq, hy_w1[l], hy_b1[l], hy_w2[l], hy_b2[l], hy_w3[l])
        y_d = hyena_mixer(pl_, spec_l, *hy_args)

        w_out_b = w_out[l].astype(BF16)
        w1_b, w2_b = mlp_w1[l].astype(BF16), mlp_w2[l].astype(BF16)
        x = mix_out([y_a, y_b, y_c], y_d, w_out_b, x, mod_l[2])
        h1 = modproj(x, mod_l[3], mod_l[4], r2(norm2[l]), w1_b, BF16, act="relu2", tm=1024)
        x = mlp_out(h1, w2_b, x, mod_l[5])

        if not last:
            y_ac = attention(q_c, [(k_c, v_c)])
            y_bc = s5_out(yc_s5, pc_, r2(s5_d[l]), glu_w, glu_b)
            y_cc = hg_out(oc_f, oc_b, pc_, r2(hg_o_norm[l]))
            spec_c = hyena_spectrum(n_ctx, hy_w1[l], hy_b1[l], hy_w2[l], hy_b2[l], hy_w3[l])
            y_dc = hyena_mixer(pc_, spec_c, *hy_args)
            xc = mix_out([y_ac, y_bc, y_cc], y_dc, w_out_b, xc, mod_c[2])
            h1c = modproj(xc, mod_c[3], mod_c[4], r2(norm2[l]), w1_b, BF16, act="relu2", tm=1024)
            xc = mlp_out(h1c, w2_b, xc, mod_c[5])
    return x
```

```python
import functools
import math

import numpy as np
import jax
import jax.numpy as jnp
from jax import lax
from jax.experimental import pallas as pl
from jax.experimental.pallas import tpu as pltpu

F32 = jnp.float32
BF16 = jnp.bfloat16
EPS = 1e-6

GRID_W = 64
GROUP_W = 512
N_MOD = 6
MLA_NOPE = 128
MLA_ROPE = 64
MLA_QK = MLA_NOPE + MLA_ROPE
MLA_HEADS = 4
KV_LORA = 256
MLA_SCALE = 1.0 / math.sqrt(MLA_QK)
LOG2E = 1.0 / math.log(2.0)
ROPE_BASE = 10000.0
S5_G = 32
S5_N = 16
S5_P = 64
S5_T = 32
HG_HEADS = 4
HG_D = 128
HG_SUB = 16
HY_EMB = 33
HY_BANDS = 16
HY_HIDDEN = 64
HY_TARGET = 1e-2
HY_FAST = 0.3
HY_SLOW = 1.5
HY_SHIFT = 0.05

LANES = 128
VMEM_LIMIT_MB = 56

PB_S5U, PB_HGF, PB_HGB, PB_HGI, PB_CQ, PB_HGQ, PB_HGG, PB_HY, PB_KV = 0, 1, 2, 3, 4, 5, 6, 7, 10
N_PROJ = 11 * GROUP_W


def _pc(kernel, out_shape, grid, in_specs, out_specs, scratch=(), vmem_mb=VMEM_LIMIT_MB, name=None):
    return pl.pallas_call(
        kernel, out_shape=out_shape, grid=grid, in_specs=in_specs, out_specs=out_specs,
        scratch_shapes=list(scratch), name=name,
        compiler_params=pltpu.CompilerParams(
            dimension_semantics=("arbitrary",) * len(grid), vmem_limit_bytes=vmem_mb << 20))


def _tile(n, pref):
    t = min(n, pref)
    while n % t:
        t //= 2
    return t


def _silu(x):
    return x * jax.nn.sigmoid(x)


def _modvec_kernel(c_ref, w_ref, b_ref, o_ref):
    s = _silu(c_ref[...])
    o_ref[...] = jnp.dot(s.astype(BF16), w_ref[...].astype(BF16), preferred_element_type=F32) + b_ref[...]


def modvec(c8, w, b):
    d, n = w.shape
    tn = _tile(n, 1536)
    return _pc(_modvec_kernel, jax.ShapeDtypeStruct((8, n), F32), (n // tn,),
               [pl.BlockSpec((8, d), lambda j: (0, 0)),
                pl.BlockSpec((d, tn), lambda j: (0, j)),
                pl.BlockSpec((1, tn), lambda j: (0, j))],
               pl.BlockSpec((8, tn), lambda j: (0, j)), name="modvec")(c8, w, b)


def _modproj_kernel(x_ref, sh_ref, sc_ref, g_ref, w_ref, o_ref, h_ref, *, act):
    @pl.when(pl.program_id(2) == 0)
    def _():
        x = x_ref[...]
        ms = jnp.mean(x * x, axis=-1, keepdims=True)
        y = x * lax.rsqrt(ms + EPS) * g_ref[...]
        h_ref[...] = (y * (1.0 + sc_ref[...]) + sh_ref[...]).astype(BF16)

    acc = jnp.dot(h_ref[...], w_ref[...], preferred_element_type=F32)
    if act == "relu2":
        acc = jnp.square(jnp.maximum(acc, 0.0))
    o_ref[...] = acc.astype(o_ref.dtype)


def modproj(x, shift, scale, g, w, out_dtype, act=None, tm=512, tn=512):
    b, l, d = x.shape
    n = w.shape[1]
    tm, tn = _tile(l, tm), _tile(n, tn)
    per_b = shift.shape[0] == b
    mod_map = (lambda bi, i, j: (bi, 0, 0)) if per_b else (lambda bi, i, j: (0, 0, 0))
    return _pc(functools.partial(_modproj_kernel, act=act),
               jax.ShapeDtypeStruct((b, l, n), out_dtype), (b, l // tm, n // tn),
               [pl.BlockSpec((None, tm, d), lambda bi, i, j: (bi, i, 0)),
                pl.BlockSpec((None, 1, d), mod_map),
                pl.BlockSpec((None, 1, d), mod_map),
                pl.BlockSpec((1, d), lambda bi, i, j: (0, 0)),
                pl.BlockSpec((d, tn), lambda bi, i, j: (0, j))],
               pl.BlockSpec((None, tm, tn), lambda bi, i, j: (bi, i, j)),
               scratch=[pltpu.VMEM((tm, d), BF16)], name="modproj_" + (act or "lin"))(x, shift, scale, g, w)


def _mmres_kernel(*refs, n_lhs, kw):
    a_refs = refs[:n_lhs]
    w_ref, res_ref, gate_ref, o_ref = refs[n_lhs:]
    acc = None
    for i, a in enumerate(a_refs):
        d = jnp.dot(a[...], w_ref[i * kw:(i + 1) * kw, :], preferred_element_type=F32)
        acc = d if acc is None else acc + d
    o_ref[...] = res_ref[...] + gate_ref[...] * acc


def mix_out(ys, yd_hlay, w, res, gate, tm=1024, tn=1024):
    b, l, kw = ys[0].shape
    n = w.shape[1]
    tm, tn = _tile(l, tm), _tile(n, tn)
    per_b = gate.shape[0] == b
    gmap = (lambda bi, i, j: (bi, 0, j)) if per_b else (lambda bi, i, j: (0, 0, j))
    y_spec = pl.BlockSpec((None, tm, kw), lambda bi, i, j: (bi, i, 0))
    yd_spec = pl.BlockSpec((None, tm, kw), lambda bi, i, j: (bi % 2, i, bi // 2))
    return _pc(functools.partial(_mmres_kernel, n_lhs=4, kw=kw),
               jax.ShapeDtypeStruct(res.shape, F32), (b, l // tm, n // tn),
               [y_spec, y_spec, y_spec, yd_spec,
                pl.BlockSpec((4 * kw, tn), lambda bi, i, j: (0, j)),
                pl.BlockSpec((None, tm, tn), lambda bi, i, j: (bi, i, j)),
                pl.BlockSpec((None, 1, tn), gmap)],
               pl.BlockSpec((None, tm, tn), lambda bi, i, j: (bi, i, j)), name="mix_out")(*ys, yd_hlay, w, res, gate)


def _mmresk_kernel(a_ref, w_ref, res_ref, gate_ref, o_ref, acc_ref):
    k = pl.program_id(3)

    @pl.when(k == 0)
    def _():
        acc_ref[...] = jnp.zeros_like(acc_ref)

    acc_ref[...] += jnp.dot(a_ref[...], w_ref[...], preferred_element_type=F32)

    @pl.when(k == pl.num_programs(3) - 1)
    def _():
        o_ref[...] = res_ref[...] + gate_ref[...] * acc_ref[...]


def mlp_out(a, w, res, gate, tm=1024, tn=1024, tk=2048):
    b, l, kk = a.shape
    n = w.shape[1]
    tm, tn, tk = _tile(l, tm), _tile(n, tn), _tile(kk, tk)
    per_b = gate.shape[0] == b
    gmap = (lambda bi, i, j, k: (bi, 0, j)) if per_b else (lambda bi, i, j, k: (0, 0, j))
    return _pc(_mmresk_kernel, jax.ShapeDtypeStruct(res.shape, F32), (b, l // tm, n // tn, kk // tk),
               [pl.BlockSpec((None, tm, tk), lambda bi, i, j, k: (bi, i, k)),
                pl.BlockSpec((tk, tn), lambda bi, i, j, k: (k, j)),
                pl.BlockSpec((None, tm, tn), lambda bi, i, j, k: (bi, i, j)),
                pl.BlockSpec((None, 1, tn), gmap)],
               pl.BlockSpec((None, tm, tn), lambda bi, i, j, k: (bi, i, j)),
               scratch=[pltpu.VMEM((tm, tn), F32)], name="mlp_out")(a, w, res, gate)


def _mla_prep_kernel(kv_ref, cq_ref, cs_ref, gkv_ref, gqa_ref, gkn_ref, gkt_ref, gqn_ref, gqt_ref,
                     wkv_ref, wq_ref, k_ref, v_ref, q_ref):
    lane = lax.broadcasted_iota(jnp.int32, (1, LANES), 1)
    low = (lane < MLA_ROPE).astype(F32)
    cs = cs_ref[...]
    kv = kv_ref[...].astype(F32)
    ckv = kv[:, :KV_LORA]
    ckvn = ckv * lax.rsqrt(jnp.mean(ckv * ckv, axis=-1, keepdims=True) + EPS) * gkv_ref[...]
    kvp = jnp.dot(ckvn.astype(BF16), wkv_ref[...], preferred_element_type=F32)
    pe2 = kv[:, KV_LORA:KV_LORA + LANES]
    ss_pe = jnp.sum(pe2 * pe2 * low, axis=-1, keepdims=True)
    pe_tab = cs * gkt_ref[...]
    for h in range(MLA_HEADS):
        knope = kvp[:, 256 * h:256 * h + 128]
        rs = lax.rsqrt((jnp.sum(knope * knope, axis=-1, keepdims=True) + ss_pe) * (1.0 / MLA_QK) + EPS)
        t = pe2 * pe_tab * rs
        t = (t + pltpu.roll(t, MLA_ROPE, axis=1)) * low
        k_ref[h, :, 0:128] = (knope * gkn_ref[...] * rs).astype(BF16)
        k_ref[h, :, 128:256] = t.astype(BF16)
        v_ref[h] = kvp[:, 256 * h + 128:256 * h + 256].astype(BF16)
    if q_ref is not None:
        cq = cq_ref[...].astype(F32)
        cqn = cq * lax.rsqrt(jnp.mean(cq * cq, axis=-1, keepdims=True) + EPS) * gqa_ref[...]
        qp = jnp.dot(cqn.astype(BF16), wq_ref[...], preferred_element_type=F32)
        q_tab = cs * gqt_ref[...]
        for h in range(MLA_HEADS):
            qn = qp[:, 256 * h:256 * h + 128]
            qe = qp[:, 256 * h + 128:256 * h + 256]
            ss = jnp.sum(qn * qn, axis=-1, keepdims=True) + jnp.sum(qe * qe * low, axis=-1, keepdims=True)
            rs = lax.rsqrt(ss * (1.0 / MLA_QK) + EPS) * (MLA_SCALE * LOG2E)
            t = qe * q_tab * rs
            t = t + pltpu.roll(t, MLA_ROPE, axis=1)
            q_ref[h, :, 0:128] = (qn * gqn_ref[...] * rs).astype(BF16)
            q_ref[h, :, 128:256] = t.astype(BF16)


def _mla_prep_kernel_noq(kv_ref, cs_ref, gkv_ref, gkn_ref, gkt_ref, wkv_ref, k_ref, v_ref):
    _mla_prep_kernel(kv_ref, None, cs_ref, gkv_ref, None, gkn_ref, gkt_ref, None, None, wkv_ref, None,
                     k_ref, v_ref, None)


def mla_prep(proj, cs, p, with_q, tm=512):
    b, l, _ = proj.shape
    tm = _tile(l, tm)
    row = lambda w: pl.BlockSpec((1, w), lambda bi, i: (0, 0))
    kv_spec = pl.BlockSpec((None, tm, GROUP_W), lambda bi, i: (bi, i, PB_KV))
    cq_spec = pl.BlockSpec((None, tm, GROUP_W), lambda bi, i: (bi, i, PB_CQ))
    cs_spec = pl.BlockSpec((tm, LANES), lambda bi, i: (i, 0))
    wkv_spec = pl.BlockSpec((KV_LORA, 1024), lambda bi, i: (0, 0))
    wq_spec = pl.BlockSpec((GROUP_W, 1024), lambda bi, i: (0, 0))
    hd = lambda w: pl.BlockSpec((None, MLA_HEADS, tm, w), lambda bi, i: (bi, 0, i, 0))
    k_sh = jax.ShapeDtypeStruct((b, MLA_HEADS, l, 256), BF16)
    v_sh = jax.ShapeDtypeStruct((b, MLA_HEADS, l, 128), BF16)
    if with_q:
        return _pc(_mla_prep_kernel, (k_sh, v_sh, k_sh), (b, l // tm),
                   [kv_spec, cq_spec, cs_spec, row(KV_LORA), row(GROUP_W), row(128), row(128), row(128), row(128),
                    wkv_spec, wq_spec],
                   (hd(256), hd(128), hd(256)), name="mla_prep")(
            proj, proj, cs, p["gkv"], p["gqa"], p["gkn"], p["gkt"], p["gqn"], p["gqt"], p["wkv"], p["wq"])
    k, v = _pc(_mla_prep_kernel_noq, (k_sh, v_sh), (b, l // tm),
               [kv_spec, cs_spec, row(KV_LORA), row(128), row(128), wkv_spec],
               (hd(256), hd(128)), name="mla_prep_kv")(proj, cs, p["gkv"], p["gkn"], p["gkt"], p["wkv"])
    return k, v, None


def _attn_kernel(*refs, n_seg, n_part):
    q_ref = refs[0]
    o_ref = refs[1 + 2 * n_seg]
    rows = q_ref.shape[0] // n_part
    for part in range(n_part):
        sl = slice(part * rows, (part + 1) * rows)
        q = q_ref[sl, :]
        scores = [lax.dot_general(q, refs[1 + 2 * i][...], (((1,), (1,)), ((), ())), preferred_element_type=F32)
                  for i in range(n_seg)]
        m = scores[0].max(axis=-1, keepdims=True)
        for s in scores[1:]:
            m = jnp.maximum(m, s.max(axis=-1, keepdims=True))
        den, acc = None, None
        for i, s in enumerate(scores):
            p = jnp.exp2(s - m)
            d = p.sum(axis=-1, keepdims=True)
            a = jnp.dot(p.astype(BF16), refs[2 + 2 * i][...], preferred_element_type=F32)
            den = d if den is None else den + d
            acc = a if acc is None else acc + a
        o_ref[sl, :] = (acc / den).astype(o_ref.dtype)


def attention(q, kvs, tq=1024, part_rows=256):
    b, h, l, _ = q.shape
    tq = _tile(l, tq)
    n_part = max(1, tq // part_rows)
    specs = [pl.BlockSpec((None, None, tq, 256), lambda bi, hi, i: (bi, hi, i, 0))]
    args = [q]
    for k, v in kvs:
        lk = k.shape[2]
        specs += [pl.BlockSpec((None, None, lk, 256), lambda bi, hi, i: (bi, hi, 0, 0)),
                  pl.BlockSpec((None, None, lk, 128), lambda bi, hi, i: (bi, hi, 0, 0))]
        args += [k, v]
    return _pc(functools.partial(_attn_kernel, n_seg=len(kvs), n_part=n_part),
               jax.ShapeDtypeStruct((b, l, h * 128), BF16), (b, h, l // tq), specs,
               pl.BlockSpec((None, tq, 128), lambda bi, hi, i: (bi, i, hi)), name="attention")(*args)


def s5_tables(lam_re, lam_im, log_step, b_re, b_im, c_re, c_im):
    t = S5_T
    step = jnp.exp(log_step.astype(F32))[..., None]
    re, im = lam_re.astype(F32) * step, lam_im.astype(F32) * step
    d = jnp.arange(t + 1, dtype=F32)[:, None, None, None]
    mag = jnp.exp(d * re)
    ar, ai = mag * jnp.cos(d * im), mag * jnp.sin(d * im)
    lr, li = lam_re.astype(F32), lam_im.astype(F32)
    den = lr * lr + li * li
    er, ei = ar[1] - 1.0, ai[1]
    cr, ci = (er * lr + ei * li) / den, (ei * lr - er * li) / den
    br, bi = b_re.astype(F32), b_im.astype(F32)
    bbr = cr[..., None] * br - ci[..., None] * bi
    bbi = cr[..., None] * bi + ci[..., None] * br
    ccr, cci = c_re.astype(F32), c_im.astype(F32)
    car = ccr[None] * ar[:, :, :, None, :] - cci[None] * ai[:, :, :, None, :]
    cai = ccr[None] * ai[:, :, :, None, :] + cci[None] * ar[:, :, :, None, :]
    kd = jnp.einsum("dxgnp,xgpm->xgdnm", car[:t], bbr) - jnp.einsum("dxgnp,xgpm->xgdnm", cai[:t], bbi)
    kcat = jnp.stack([kd[0], kd[1][:, ::-1]], axis=1).transpose(0, 1, 4, 2, 3).reshape(S5_G, 2, S5_N, t * S5_N)
    pw = (jnp.arange(t - 1, -1, -1), jnp.arange(t))
    w_re, w_im = [], []
    for x in range(2):
        a_r, a_i = ar[pw[x], x], ai[pw[x], x]
        w_re.append(a_r[..., None] * bbr[x][None] - a_i[..., None] * bbi[x][None])
        w_im.append(a_r[..., None] * bbi[x][None] + a_i[..., None] * bbr[x][None])
    ws = jnp.concatenate(w_re + w_im, axis=2).transpose(1, 0, 3, 2).reshape(S5_G, t * S5_N, 4 * S5_P)
    pv = (jnp.arange(1, t + 1), jnp.arange(t, 0, -1))
    vm = jnp.concatenate([car[pv[0], 0], car[pv[1], 1], -cai[pv[0], 0], -cai[pv[1], 1]], axis=-1)
    vm = vm.transpose(1, 3, 0, 2).reshape(S5_G, 4 * S5_P, t * S5_N)
    dec = jnp.stack([jnp.concatenate([ar[t, 0], ar[t, 1]], axis=-1),
                     jnp.concatenate([ai[t, 0], ai[t, 1]], axis=-1)], axis=1)
    return kcat, ws.astype(BF16), vm.astype(BF16), dec


def _s5a_kernel(u_ref, k_ref, ws_ref, y_ref, s_ref, m_ref):
    kf, kb = k_ref[0], k_ref[1]
    lane = lax.broadcasted_iota(jnp.int32, kf.shape, 1)
    for s in range(S5_T):
        lo, hi = S5_N * s, S5_N * (s + 1)
        f = kf if s == 0 else jnp.where(lane >= lo, pltpu.roll(kf, lo, axis=1), 0.0)
        bk = kb if s == S5_T - 1 else jnp.where(lane < hi, pltpu.roll(kb, hi, axis=1), 0.0)
        m_ref[lo:hi, :] = (f + bk).astype(BF16)
    u = u_ref[...]
    y_ref[...] = jnp.dot(u, m_ref[...], preferred_element_type=F32)
    s_ref[...] = jnp.dot(u, ws_ref[...], preferred_element_type=F32)


def _s5b_kernel(yi_ref, s_ref, x0_ref, dec_ref, vm_ref, y_ref, xf_ref, xa_ref, xb_ref, *, nchunk, nbatch):
    cps = 8 // nbatch
    nslab = nchunk // cps
    ar = dec_ref[0:1, :]
    ai = dec_ref[1:2, :]
    fwd_lane = lax.broadcasted_iota(jnp.int32, (nbatch, 128), 1) < S5_P

    def step(jj, carry):
        xr, xi = carry
        rf = pl.ds(pl.multiple_of(jj * 8, 8), 8)
        rb = pl.ds(pl.multiple_of((nslab - 1 - jj) * 8, 8), 8)
        sf, sb = s_ref[rf, :], s_ref[rb, :]
        seen = []
        for i in range(cps):
            lo, hi = i * nbatch, (i + 1) * nbatch
            ml, mh = (cps - 1 - i) * nbatch, (cps - i) * nbatch
            seen.append(jnp.concatenate([xr, xi], axis=1))
            sr = jnp.where(fwd_lane, sf[lo:hi, :128], sb[ml:mh, :128])
            si = jnp.where(fwd_lane, sf[lo:hi, 128:], sb[ml:mh, 128:])
            xr, xi = ar * xr - ai * xi + sr, ar * xi + ai * xr + si
        xa_ref[rf, :] = jnp.concatenate(seen, axis=0)
        xb_ref[rb, :] = jnp.concatenate(seen[::-1], axis=0)
        return xr, xi

    x0 = x0_ref[0:nbatch, :]
    xr, xi = lax.fori_loop(0, nslab, step, (x0[:, :128], x0[:, 128:]), unroll=2)
    xf_ref[...] = jnp.zeros_like(xf_ref)
    xf_ref[0:nbatch, :] = jnp.concatenate([xr, xi], axis=1)
    pick = (lax.broadcasted_iota(jnp.int32, xa_ref.shape, 1) % 128) < S5_P
    xin = jnp.where(pick, xa_ref[...], xb_ref[...])
    y = yi_ref[...] + jnp.dot(xin.astype(BF16), vm_ref[...], preferred_element_type=F32)
    y_ref[...] = y.astype(y_ref.dtype)


def s5_scan(u, x0, tabs):
    kcat, ws, vm, dec = tabs
    b, l, _ = u.shape
    j = l // S5_T
    r = b * j
    assert 8 % b == 0 and j % (8 // b) == 0
    ug = u.astype(BF16).reshape(b, j, S5_T, S5_G, S5_N).transpose(3, 1, 0, 2, 4).reshape(S5_G, r, GROUP_W)
    g3 = lambda w: pl.BlockSpec((None, r, w), lambda g: (g, 0, 0))
    yi, s = _pc(_s5a_kernel,
                (jax.ShapeDtypeStruct((S5_G, r, GROUP_W), F32), jax.ShapeDtypeStruct((S5_G, r, 256), F32)),
                (S5_G,),
                [g3(GROUP_W), pl.BlockSpec((None, 2, S5_N, GROUP_W), lambda g: (g, 0, 0, 0)),
                 pl.BlockSpec((None, GROUP_W, 256), lambda g: (g, 0, 0))],
                (g3(GROUP_W), g3(256)),
                scratch=[pltpu.VMEM((GROUP_W, GROUP_W), BF16)], name="s5a")(ug, kcat, ws)
    y, xf = _pc(functools.partial(_s5b_kernel, nchunk=j, nbatch=b),
                (jax.ShapeDtypeStruct((S5_G, r, GROUP_W), BF16), jax.ShapeDtypeStruct((S5_G, 8, 256), F32)),
                (S5_G,),
                [g3(GROUP_W), g3(256),
                 pl.BlockSpec((None, 8, 256), lambda g: (g, 0, 0)),
                 pl.BlockSpec((None, 2, 128), lambda g: (g, 0, 0)),
                 pl.BlockSpec((None, 256, GROUP_W), lambda g: (g, 0, 0))],
                (g3(GROUP_W), pl.BlockSpec((None, 8, 256), lambda g: (g, 0, 0))),
                scratch=[pltpu.VMEM((r, 256), F32), pltpu.VMEM((r, 256), F32)], name="s5b")(yi, s, x0, dec, vm)
    y = y.reshape(S5_G, j, b, S5_T, S5_N).transpose(2, 1, 3, 0, 4).reshape(b, l, GROUP_W)
    return y, xf


def _s5_out_kernel(y_ref, u_ref, d_ref, w_ref, b_ref, o_ref):
    y = jax.nn.gelu(y_ref[...].astype(F32) + d_ref[...] * u_ref[...].astype(F32))
    z = jnp.dot(y.astype(BF16), w_ref[...], preferred_element_type=F32) + b_ref[...]
    o_ref[...] = (y * jax.nn.sigmoid(z)).astype(o_ref.dtype)


def s5_out(y, proj, d, w, bias, tm=512):
    b, l, _ = y.shape
    tm = _tile(l, tm)
    row = pl.BlockSpec((1, GROUP_W), lambda bi, i: (0, 0))
    return _pc(_s5_out_kernel, jax.ShapeDtypeStruct((b, l, GROUP_W), BF16), (b, l // tm),
               [pl.BlockSpec((None, tm, GROUP_W), lambda bi, i: (bi, i, 0)),
                pl.BlockSpec((None, tm, GROUP_W), lambda bi, i: (bi, i, PB_S5U)),
                row, pl.BlockSpec((GROUP_W, GROUP_W), lambda bi, i: (0, 0)), row],
               pl.BlockSpec((None, tm, GROUP_W), lambda bi, i: (bi, i, 0)), name="s5_out")(y, proj, d, w, bias)


def _hgrn_kernel(z_ref, i_ref, q_ref, la_ref, lb_ref, s0_ref, o_ref, sfin_ref, st_ref, *, reverse, nsub):
    c = HG_SUB

    @pl.when(pl.program_id(1) == 0)
    def _():
        st_ref[...] = s0_ref[...]

    hlf = c // 2
    ones = jnp.ones((HG_D, HG_D), BF16)
    row = lax.broadcasted_iota(jnp.int32, (c, HG_D), 0)
    row8 = lax.broadcasted_iota(jnp.int32, (hlf, HG_D), 0)

    def sub(jj, carry):
        j = (nsub - 1 - jj) if reverse else jj
        rows = pl.ds(pl.multiple_of(j * c, c), c)
        for h in range(HG_HEADS):
            sl = slice(HG_D * h, HG_D * (h + 1))
            z = z_ref[rows, sl].astype(F32)
            v = i_ref[rows, sl].astype(F32)
            q = _silu(q_ref[rows, sl].astype(F32))
            a = la_ref[:, sl]
            bb = lb_ref[:, sl] + (jnp.minimum(z, 0.0) - jnp.log1p(jnp.exp(-jnp.abs(z))))
            g = jnp.maximum(a, bb) + jnp.log1p(jnp.exp(-jnp.abs(a - bb)))
            k = 1.0 - jnp.exp(g)
            gc = g
            for sh in (1, 2, 4, 8):
                if reverse:
                    gc = gc + jnp.where(row < c - sh, pltpu.roll(gc, c - sh, axis=0), 0.0)
                else:
                    gc = gc + jnp.where(row >= sh, pltpu.roll(gc, sh, axis=0), 0.0)
            gc = gc * LOG2E
            gk = gc - jnp.log(jnp.maximum(k, 0.0)) * LOG2E
            blocks, meta = [], []
            for s in range(c):
                for half in range(2):
                    r0 = half * hlf
                    if (r0 > s) if reverse else (r0 + hlf - 1 < s):
                        continue
                    e = gc[r0:r0 + hlf, :] - gk[s:s + 1, :]
                    if not ((r0 + hlf - 1 <= s) if reverse else (r0 >= s)):
                        keep = (row8 + r0 <= s) if reverse else (row8 + r0 >= s)
                        e = jnp.where(keep, e, -1e30)
                    blocks.append(q[r0:r0 + hlf, :] * jnp.exp2(e))
                    meta.append((half, s))
            rsum = jnp.dot(jnp.concatenate(blocks, axis=0).astype(BF16), ones, preferred_element_type=F32)
            halves = [jnp.zeros((hlf, HG_D), F32), jnp.zeros((hlf, HG_D), F32)]
            for i, (half, s) in enumerate(meta):
                halves[half] = halves[half] + rsum[hlf * i:hlf * (i + 1), :] * v[s:s + 1, :]
            o = jnp.concatenate(halves, axis=0)
            st = st_ref[h]
            o = o + lax.dot_general((q * jnp.exp2(gc)).astype(BF16), st.astype(BF16),
                                    (((1,), (1,)), ((), ())), preferred_element_type=F32)
            gl = gc[0:1, :] if reverse else gc[c - 1:c, :]
            kd = k * jnp.exp2(gl - gc)
            ds = lax.dot_general(v.astype(BF16), kd.astype(BF16), (((0,), (0,)), ((), ())),
                                 preferred_element_type=F32)
            st_ref[h] = st * jnp.exp2(gl) + ds
            o_ref[rows, sl] = o
        return carry

    lax.fori_loop(0, nsub, sub, 0, unroll=4)

    @pl.when(pl.program_id(1) == pl.num_programs(1) - 1)
    def _():
        sfin_ref[...] = st_ref[...]


def hgrn_dir(proj, la, lb1, s0, reverse, tl=256):
    b, l, _ = proj.shape
    tl = _tile(l, tl)
    nb = l // tl
    blk = (lambda i: nb - 1 - i) if reverse else (lambda i: i)
    col = lambda c: pl.BlockSpec((None, tl, GROUP_W), lambda bi, i: (bi, blk(i), c))
    row = pl.BlockSpec((1, GROUP_W), lambda bi, i: (0, 0))
    st = pl.BlockSpec((None, HG_HEADS, HG_D, HG_D), lambda bi, i: (bi, 0, 0, 0))
    return _pc(functools.partial(_hgrn_kernel, reverse=reverse, nsub=tl // HG_SUB),
               (jax.ShapeDtypeStruct((b, l, GROUP_W), F32), jax.ShapeDtypeStruct(s0.shape, F32)),
               (b, nb),
               [col(PB_HGB if reverse else PB_HGF), col(PB_HGI), col(PB_HGQ), row, row, st],
               (pl.BlockSpec((None, tl, GROUP_W), lambda bi, i: (bi, blk(i), 0)), st),
               scratch=[pltpu.VMEM((HG_HEADS, HG_D, HG_D), F32)],
               name="hgrn_bwd" if reverse else "hgrn_fwd")(proj, proj, proj, la, lb1, s0)


def _hg_out_kernel(of_ref, ob_ref, g_ref, gn_ref, y_ref):
    o = of_ref[...] + ob_ref[...]
    gate = _silu(g_ref[...].astype(F32))
    for h in range(HG_HEADS):
        sl = slice(HG_D * h, HG_D * (h + 1))
        oh = o[:, sl]
        y = oh * lax.rsqrt(jnp.mean(oh * oh, axis=-1, keepdims=True) + EPS) * gn_ref[...]
        y_ref[:, sl] = (y * gate[:, sl]).astype(y_ref.dtype)


def hg_out(of, ob, proj, gn, tm=512):
    b, l, _ = of.shape
    tm = _tile(l, tm)
    t3 = lambda c: pl.BlockSpec((None, tm, GROUP_W), lambda bi, i: (bi, i, c))
    return _pc(_hg_out_kernel, jax.ShapeDtypeStruct((b, l, GROUP_W), BF16), (b, l // tm),
               [t3(0), t3(0), t3(PB_HGG), pl.BlockSpec((1, HG_D), lambda bi, i: (0, 0))],
               t3(0), name="hg_out")(of, ob, proj, gn)


def _fft_factors(n_fft):
    n1 = {8192: 64, 512: 32, 1024: 32, 2048: 32, 4096: 64}[n_fft]
    return n1, n_fft // n1


def _dft_tables(na, nb, ka, mb, inverse):
    n = na * nb
    sgn = 1.0 if inverse else -1.0

    def stacked(phase, scale=1.0):
        ang = sgn * 2.0 * np.pi * (phase % n) / n
        fr, fi = np.cos(ang) * scale, np.sin(ang) * scale
        return np.concatenate([np.concatenate([fr, -fi], axis=-1), np.concatenate([fi, fr], axis=-1)], axis=-2)

    fa = stacked(np.outer(np.arange(na), np.arange(ka)) * nb)
    p, q, b = np.arange(na)[:, None, None], np.arange(mb)[None, :, None], np.arange(nb)[None, None, :]
    fb = stacked(b * p + b * q * na, (1.0 / n) if inverse else 1.0)
    return jnp.asarray(fa, BF16), jnp.asarray(fb, BF16)


def _fft_a_kernel(z_ref, f_ref, o_ref, *, g, na, is_complex):
    f = f_ref[...]
    for i in range(g):
        z = z_ref[:, i] if is_complex else z_ref[i]
        if is_complex:
            z = z.reshape(z.shape[0] * z.shape[1], z.shape[2])
        r = jnp.dot(f, z, preferred_element_type=F32)
        o_ref[0, i] = r[:na]
        o_ref[1, i] = r[na:]


def fft_stage_a(z, fa, g=8):
    is_complex = z.ndim == 4
    nb, ka, pc = z.shape[-3:]
    na = fa.shape[0] // 2
    g = min(g, nb)
    zspec = (pl.BlockSpec((2, g, ka, pc), lambda j: (0, j, 0, 0)) if is_complex
             else pl.BlockSpec((g, ka, pc), lambda j: (j, 0, 0)))
    return _pc(functools.partial(_fft_a_kernel, g=g, na=na, is_complex=is_complex),
               jax.ShapeDtypeStruct((2, nb, na, pc), F32), (nb // g,),
               [zspec, pl.BlockSpec(fa.shape, lambda j: (0, 0))],
               pl.BlockSpec((2, g, na, pc), lambda j: (0, j, 0, 0)), name="fft_a")(z, fa)


def _fft_b_kernel(*refs, mode, mb, kb):
    a_ref, f_ref = refs[0], refs[1]
    o_ref = refs[-1]
    nb = a_ref.shape[1]
    for i in range(kb):
        f = f_ref[i].astype(F32)
        z = (jnp.dot(f[:, :nb], a_ref[0, :, i, :], preferred_element_type=F32)
             + jnp.dot(f[:, nb:], a_ref[1, :, i, :], preferred_element_type=F32))
        zr, zi = z[:mb], z[mb:]
        if mode == "filter":
            h_ref = refs[2]
            hr, hi = h_ref[0, i].astype(F32), h_ref[1, i].astype(F32)
            zr, zi = zr * hr - zi * hi, zr * hi + zi * hr
        elif mode == "final":
            v_ref, g_ref, b_ref = refs[2], refs[3], refs[4]
            zr = g_ref[0, i].astype(F32) * (zr + v_ref[0, i].astype(F32) * b_ref[...])
            zi = g_ref[1, i].astype(F32) * (zi + v_ref[1, i].astype(F32) * b_ref[...])
        o_ref[0, i] = zr.astype(o_ref.dtype)
        o_ref[1, i] = zi.astype(o_ref.dtype)


def fft_stage_b(a, fb, mode, extra=(), order=0, kb=8):
    _, nb, na, pc = a.shape
    mb = fb.shape[1] // 2
    c = GROUP_W
    kb = min(kb, na)
    blk = pl.BlockSpec((2, kb, mb, c), lambda k, p: (0, k, 0, p))
    specs = [pl.BlockSpec((2, nb, kb, c), lambda k, p: (0, 0, k, p)),
             pl.BlockSpec((kb, 2 * mb, 2 * nb), lambda k, p: (k, 0, 0))]
    args = [a, fb]
    if mode == "filter":
        (h,) = extra
        specs.append(pl.BlockSpec((2, kb, mb, c), lambda k, p: (0, k, 0, order)))
        args.append(h)
    elif mode == "final":
        v, gate, bias = extra
        specs += [blk, blk, pl.BlockSpec((1, c), lambda k, p: (0, 0))]
        args += [v, gate, bias]
    return _pc(functools.partial(_fft_b_kernel, mode=mode, mb=mb, kb=kb),
               jax.ShapeDtypeStruct((2, na, mb, pc), BF16), (na // kb, pc // c), specs, blk,
               name="fft_b_" + mode)(*args)


def _hy_filter_kernel(emb_ref, w1_ref, b1_ref, w2_ref, b2_ref, w3_ref, win_ref, o_ref, h_ref, *, na):
    hp = lax.Precision.HIGHEST

    @pl.when(pl.program_id(0) == 0)
    def _():
        h = jnp.sin(jnp.dot(emb_ref[...], w1_ref[...], precision=hp, preferred_element_type=F32) + b1_ref[...])
        h = jnp.sin(jnp.dot(h, w2_ref[...], precision=hp, preferred_element_type=F32) + b2_ref[...])
        row = lax.broadcasted_iota(jnp.int32, h.shape, 0)
        causal = (row & (na - 1)) < na // 2
        h_ref[...] = jnp.concatenate([jnp.where(causal, h, 0.0), jnp.where(causal, 0.0, h)], axis=1).astype(BF16)

    taps = jnp.dot(h_ref[...], w3_ref[...], preferred_element_type=F32) * win_ref[...]
    o_ref[...] = (taps / jnp.sum(jnp.abs(taps), axis=0, keepdims=True)).astype(o_ref.dtype)


def hyena_filters(n, w1, b1, w2, b2, w3):
    na, nb = _fft_factors(2 * n)
    t = np.arange(n, dtype=np.float32)
    t_norm = t / max(n - 1, 1)
    bands = np.linspace(1e-4, HY_BANDS - 1, HY_BANDS, dtype=np.float32)
    ang = (2.0 * math.pi * t / n)[:, None] * bands[None, :]
    emb = np.concatenate([t_norm[:, None], np.cos(ang), -np.sin(ang)], axis=-1).astype(np.float32)
    emb = np.pad(emb, ((0, 0), (0, LANES - HY_EMB)))
    deltas = np.linspace(math.log(HY_TARGET) / HY_SLOW, math.log(HY_TARGET) / HY_FAST, GROUP_W, dtype=np.float32)
    win = (np.exp(-t_norm[:, None] * np.abs(deltas)[None, :]) + np.float32(HY_SHIFT)).astype(np.float32)
    back = np.concatenate([[0], np.arange(n - 2, -1, -1)])
    win_b = win[back]
    win_b[0] = 0.0
    order = (np.arange(na)[None, :] * nb + np.arange(nb)[:, None]).reshape(-1)
    emb2 = jnp.asarray(np.concatenate([emb, emb[back]], axis=0)[order])
    win2 = jnp.asarray(np.concatenate([win, win_b], axis=0)[order])
    w1p = jnp.pad(w1.astype(F32), ((0, LANES - HY_EMB), (0, 0)))
    w3r = w3.astype(F32).reshape(HY_HIDDEN, 2, 2, GROUP_W)
    w3c = jnp.concatenate([w3r[:, :, 0].reshape(HY_HIDDEN, 2 * GROUP_W),
                           w3r[:, :, 1].reshape(HY_HIDDEN, 2 * GROUP_W)], axis=0).astype(BF16)
    tc = LANES
    full = lambda a: pl.BlockSpec(a.shape, lambda j: (0, 0))
    b1r, b2r = b1.reshape(1, -1).astype(F32), b2.reshape(1, -1).astype(F32)
    w2f = w2.astype(F32)
    return _pc(functools.partial(_hy_filter_kernel, na=na),
               jax.ShapeDtypeStruct((2 * n, 2 * GROUP_W), BF16), (2 * GROUP_W // tc,),
               [full(emb2), full(w1p), full(b1r), full(w2f), full(b2r),
                pl.BlockSpec((2 * HY_HIDDEN, tc), lambda j: (0, j)),
                pl.BlockSpec((2 * n, tc), lambda j: (0, j % (GROUP_W // tc)))],
               pl.BlockSpec((2 * n, tc), lambda j: (0, j)),
               scratch=[pltpu.VMEM((2 * n, 2 * HY_HIDDEN), BF16)], name="hy_filter")(
        emb2, w1p, b1r, w2f, b2r, w3c, win2)


def hyena_spectrum(n, w1, b1, w2, b2, w3):
    na, nb = _fft_factors(2 * n)
    circ = hyena_filters(n, w1, b1, w2, b2, w3).reshape(nb, na, 2 * GROUP_W)
    fa, fb = _dft_tables(na, nb, na, nb, inverse=False)
    return fft_stage_b(fft_stage_a(circ, fa[:, :na]), fb, "plain")


def _shortconv_kernel(p_ref, w_ref, b_ref, o_ref):
    p = p_ref[...].astype(F32)
    n = p.shape[0]
    row = lax.broadcasted_iota(jnp.int32, p.shape, 0)
    prev = jnp.where(row >= 1, pltpu.roll(p, 1, axis=0), 0.0)
    nxt = jnp.where(row < n - 1, pltpu.roll(p, n - 1, axis=0), 0.0)
    o_ref[...] = (prev * w_ref[0:1, :] + p * w_ref[1:2, :] + nxt * w_ref[2:3, :] + b_ref[...]).astype(o_ref.dtype)


def hyena_shortconv(proj, w, bias):
    b, n, _ = proj.shape
    return _pc(_shortconv_kernel, jax.ShapeDtypeStruct((3, 2, n, (b // 2) * GROUP_W), BF16), (b, 3),
               [pl.BlockSpec((None, n, GROUP_W), lambda bi, j: (bi, 0, PB_HY + j)),
                pl.BlockSpec((3, GROUP_W), lambda bi, j: (0, j)),
                pl.BlockSpec((1, GROUP_W), lambda bi, j: (0, j))],
               pl.BlockSpec((None, None, n, GROUP_W), lambda bi, j: (j, bi % 2, 0, bi // 2)),
               name="shortconv")(proj, w, bias)


def hyena_mixer(proj, spec, conv_w, conv_b, bias):
    b, n, _ = proj.shape
    na, nb = _fft_factors(2 * n)
    pc = (b // 2) * GROUP_W
    u = hyena_shortconv(proj, conv_w, conv_b).reshape(3, 2, na // 2, nb, pc).swapaxes(2, 3)
    fa, fb = _dft_tables(na, nb, na // 2, nb, inverse=False)
    ga, gb = _dft_tables(nb, na, nb, na // 2, inverse=True)
    z = u[0]
    for o in range(2):
        zf = fft_stage_b(fft_stage_a(z, fa), fb, "filter", (spec,), order=o)
        z = fft_stage_b(fft_stage_a(zf, ga), gb, "final", (z, u[1 + o], bias[o:o + 1]))
    return z.swapaxes(1, 2).reshape(2, n, pc)


def _rot_cols(w):
    return jnp.concatenate([-w[..., 16:32], w[..., 0:16], -w[..., 48:64], w[..., 32:48]], axis=-1)


def _rot_perm(g):
    return jnp.concatenate([g[16:32], g[0:16], g[48:64], g[32:48]])


def _prep_layer(w_in, w_uq, w_ukv, q_a_g, kv_a_g, q_g, k_g):
    d = w_in.shape[0]
    ckv, krope, rest = w_in[:, :256], w_in[:, 256:320], w_in[:, 320:]
    w_in_p = jnp.concatenate([rest, ckv, krope, _rot_cols(krope), jnp.zeros((d, 128), w_in.dtype)], axis=1)
    wq = w_uq.reshape(GROUP_W, MLA_HEADS, MLA_QK)
    wq = jnp.concatenate([wq, _rot_cols(wq[..., MLA_NOPE:])], axis=-1).reshape(GROUP_W, MLA_HEADS * 256)
    r2 = lambda a: a.reshape(1, -1).astype(F32)
    mla = dict(wkv=w_ukv.astype(BF16), wq=wq.astype(BF16), gkv=r2(kv_a_g), gqa=r2(q_a_g),
               gkn=r2(k_g[:MLA_NOPE]), gkt=r2(jnp.concatenate([k_g[MLA_NOPE:], _rot_perm(k_g[MLA_NOPE:])])),
               gqn=r2(q_g[:MLA_NOPE]), gqt=r2(jnp.concatenate([q_g[MLA_NOPE:], _rot_perm(q_g[MLA_NOPE:])])))
    return w_in_p.astype(BF16), mla


def _rope_table(n_rows):
    row = np.repeat(np.arange(n_rows, dtype=np.float32), GRID_W)
    col = np.tile(np.arange(GRID_W, dtype=np.float32), n_rows)
    n_freq = MLA_ROPE // 4
    inv_freq = (ROPE_BASE ** (-np.arange(n_freq, dtype=np.float32) / n_freq)).astype(np.float32)
    ang_r, ang_c = row[:, None] * inv_freq, col[:, None] * inv_freq
    ang = np.concatenate([ang_r, ang_r, ang_c, ang_c], axis=-1)
    return jnp.asarray(np.concatenate([np.cos(ang), np.sin(ang)], axis=-1), F32)


def kernel(x, c, ctx, c_ctx, w_mod, b_mod, norm1, norm2, w_in, w_out, mla_q_a_norm, mla_kv_a_norm, mla_w_uq, mla_w_ukv, mla_q_norm, mla_k_norm, s5_lam_re, s5_lam_im, s5_log_step, s5_b_re, s5_b_im, s5_c_re, s5_c_im, s5_d, s5_w_glu, s5_b_glu, hg_lower_bounds, hg_o_norm, hy_conv_w, hy_conv_b, hy_w1, hy_b1, hy_w2, hy_b2, hy_w3, hy_bias, mlp_w1, mlp_w2):
    bsz, seq, dm = x.shape
    n_ctx = ctx.shape[1]
    depth = w_mod.shape[0]
    assert bsz % 2 == 0 and bsz <= 8 and seq % GRID_W == 0 and S5_T * S5_N == GROUP_W
    r2 = lambda a: a.reshape(1, -1).astype(F32)

    cs_l = _rope_table(seq // GRID_W)
    cs_c = jnp.concatenate([jnp.ones((n_ctx, MLA_ROPE), F32), jnp.zeros((n_ctx, MLA_ROPE), F32)], axis=-1)
    sm = jax.nn.softmax(hg_lower_bounds.astype(F32), axis=1)
    lower = jnp.clip(jnp.cumsum(sm, axis=1) - sm[:, :1], 0.0, 1.0)
    c8 = jnp.concatenate([c.astype(F32), c_ctx.astype(F32)[None], jnp.zeros((7 - bsz, dm), F32)], axis=0)
    zero_state = jnp.zeros((bsz, HG_HEADS, HG_D, HG_D), F32)
    zero_x = jnp.zeros((S5_G, 8, 256), F32)

    xc = ctx
    for l in range(depth):
        last = l == depth - 1
        mod = modvec(c8, w_mod[l].astype(F32), r2(b_mod[l]))
        mod_l = [mod[:bsz, i * dm:(i + 1) * dm][:, None, :] for i in range(N_MOD)]
        mod_c = [mod[bsz:bsz + 1, i * dm:(i + 1) * dm][:, None, :] for i in range(N_MOD)]
        w_in_p, mla_p = _prep_layer(w_in[l], mla_w_uq[l], mla_w_ukv[l], mla_q_a_norm[l], mla_kv_a_norm[l],
                                    mla_q_norm[l], mla_k_norm[l])
        pl_ = modproj(x, mod_l[0], mod_l[1], r2(norm1[l]), w_in_p, BF16, tm=1024)
        pc_ = modproj(xc, mod_c[0], mod_c[1], r2(norm1[l]), w_in_p, BF16, tm=1024)

        k_c, v_c, q_c = mla_prep(pc_, cs_c, mla_p, with_q=not last)
        k_l, v_l, q_l = mla_prep(pl_, cs_l, mla_p, with_q=True)
        y_a = attention(q_l, [(k_l, v_l), (k_c, v_c)])

        tabs = s5_tables(s5_lam_re[l], s5_lam_im[l], s5_log_step[l], s5_b_re[l], s5_b_im[l], s5_c_re[l], s5_c_im[l])
        yc_s5, fin = s5_scan(pc_[..., PB_S5U * GROUP_W:(PB_S5U + 1) * GROUP_W], zero_x, tabs)
        yl_s5, _ = s5_scan(pl_[..., PB_S5U * GROUP_W:(PB_S5U + 1) * GROUP_W], fin, tabs)
        glu_w, glu_b = s5_w_glu[l].astype(BF16), r2(s5_b_glu[l])
        y_b = s5_out(yl_s5, pl_, r2(s5_d[l]), glu_w, glu_b)

        la_f, lb_f = r2(jnp.log(lower[0, l])), r2(jnp.log1p(-lower[0, l]))
        la_b, lb_b = r2(jnp.log(lower[1, l])), r2(jnp.log1p(-lower[1, l]))
        oc_f, s_f = hgrn_dir(pc_, la_f, lb_f, zero_state, reverse=False)
        oc_b, s_b = hgrn_dir(pc_, la_b, lb_b, zero_state, reverse=True)
        ol_f, _ = hgrn_dir(pl_, la_f, lb_f, s_f, reverse=False)
        ol_b, _ = hgrn_dir(pl_, la_b, lb_b, s_b, reverse=True)
        y_c = hg_out(ol_f, ol_b, pl_, r2(hg_o_norm[l]))

        hy_args = (hy_conv_w[l].astype(F32), r2(hy_conv_b[l]), hy_bias[l].astype(F32))
        spec_l = hyena_spectrum(seq, hy_w1[l], hy_b1[l], hy_w2[l], hy_b2[l], hy_w3[l])
        y_d = hyena_mixer(pl_, spec_l, *hy_args)

        w_out_b = w_out[l].astype(BF16)
        w1_b, w2_b = mlp_w1[l].astype(BF16), mlp_w2[l].astype(BF16)
        x = mix_out([y_a, y_b, y_c], y_d, w_out_b, x, mod_l[2])
        h1 = modproj(x, mod_l[3], mod_l[4], r2(norm2[l]), w1_b, BF16, act="relu2", tm=1024)
        x = mlp_out(h1, w2_b, x, mod_l[5])

        if not last:
            y_ac = attention(q_c, [(k_c, v_c)])
            y_bc = s5_out(yc_s5, pc_, r2(s5_d[l]), glu_w, glu_b)
            y_cc = hg_out(oc_f, oc_b, pc_, r2(hg_o_norm[l]))
            spec_c = hyena_spectrum(n_ctx, hy_w1[l], hy_b1[l], hy_w2[l], hy_b2[l], hy_w3[l])
            y_dc = hyena_mixer(pc_, spec_c, *hy_args)
            xc = mix_out([y_ac, y_bc, y_cc], y_dc, w_out_b, xc, mod_c[2])
            h1c = modproj(xc, mod_c[3], mod_c[4], r2(norm2[l]), w1_b, BF16, act="relu2", tm=1024)
            xc = mlp_out(h1c, w2_b, xc, mod_c[5])
    return x
```

```python
import functools
import math

import numpy as np
import jax
import jax.numpy as jnp
from jax import lax
from jax.experimental import pallas as pl
from jax.experimental.pallas import tpu as pltpu

F32 = jnp.float32
BF16 = jnp.bfloat16
EPS = 1e-6

GRID_W = 64
GROUP_W = 512
N_MOD = 6
MLA_NOPE = 128
MLA_ROPE = 64
MLA_QK = MLA_NOPE + MLA_ROPE
MLA_HEADS = 4
KV_LORA = 256
MLA_SCALE = 1.0 / math.sqrt(MLA_QK)
LOG2E = 1.0 / math.log(2.0)
ROPE_BASE = 10000.0
S5_G = 32
S5_N = 16
S5_P = 64
S5_T = 32
HG_HEADS = 4
HG_D = 128
HG_SUB = 16
HY_EMB = 33
HY_BANDS = 16
HY_HIDDEN = 64
HY_TARGET = 1e-2
HY_FAST = 0.3
HY_SLOW = 1.5
HY_SHIFT = 0.05

LANES = 128
VMEM_LIMIT_MB = 56

PB_S5U, PB_HGF, PB_HGB, PB_HGI, PB_CQ, PB_HGQ, PB_HGG, PB_HY, PB_KV = 0, 1, 2, 3, 4, 5, 6, 7, 10
N_PROJ = 11 * GROUP_W


def _pc(kernel, out_shape, grid, in_specs, out_specs, scratch=(), vmem_mb=VMEM_LIMIT_MB, name=None):
    return pl.pallas_call(
        kernel, out_shape=out_shape, grid=grid, in_specs=in_specs, out_specs=out_specs,
        scratch_shapes=list(scratch), name=name,
        compiler_params=pltpu.CompilerParams(
            dimension_semantics=("arbitrary",) * len(grid), vmem_limit_bytes=vmem_mb << 20))


def _tile(n, pref):
    t = min(n, pref)
    while n % t:
        t //= 2
    return t


def _silu(x):
    return x * jax.nn.sigmoid(x)


def _modvec_kernel(c_ref, w_ref, b_ref, o_ref):
    s = _silu(c_ref[...])
    o_ref[...] = jnp.dot(s.astype(BF16), w_ref[...].astype(BF16), preferred_element_type=F32) + b_ref[...]


def modvec(c8, w, b):
    d, n = w.shape
    tn = _tile(n, 1536)
    return _pc(_modvec_kernel, jax.ShapeDtypeStruct((8, n), F32), (n // tn,),
               [pl.BlockSpec((8, d), lambda j: (0, 0)),
                pl.BlockSpec((d, tn), lambda j: (0, j)),
                pl.BlockSpec((1, tn), lambda j: (0, j))],
               pl.BlockSpec((8, tn), lambda j: (0, j)), name="modvec")(c8, w, b)


def _modproj_kernel(x_ref, sh_ref, sc_ref, g_ref, w_ref, o_ref, h_ref, *, act):
    @pl.when(pl.program_id(2) == 0)
    def _():
        x = x_ref[...]
        ms = jnp.mean(x * x, axis=-1, keepdims=True)
        y = x * lax.rsqrt(ms + EPS) * g_ref[...]
        h_ref[...] = (y * (1.0 + sc_ref[...]) + sh_ref[...]).astype(BF16)

    acc = jnp.dot(h_ref[...], w_ref[...], preferred_element_type=F32)
    if act == "relu2":
        acc = jnp.square(jnp.maximum(acc, 0.0))
    o_ref[...] = acc.astype(o_ref.dtype)


def modproj(x, shift, scale, g, w, out_dtype, act=None, tm=512, tn=512):
    b, l, d = x.shape
    n = w.shape[1]
    tm, tn = _tile(l, tm), _tile(n, tn)
    per_b = shift.shape[0] == b
    mod_map = (lambda bi, i, j: (bi, 0, 0)) if per_b else (lambda bi, i, j: (0, 0, 0))
    return _pc(functools.partial(_modproj_kernel, act=act),
               jax.ShapeDtypeStruct((b, l, n), out_dtype), (b, l // tm, n // tn),
               [pl.BlockSpec((None, tm, d), lambda bi, i, j: (bi, i, 0)),
                pl.BlockSpec((None, 1, d), mod_map),
                pl.BlockSpec((None, 1, d), mod_map),
                pl.BlockSpec((1, d), lambda bi, i, j: (0, 0)),
                pl.BlockSpec((d, tn), lambda bi, i, j: (0, j))],
               pl.BlockSpec((None, tm, tn), lambda bi, i, j: (bi, i, j)),
               scratch=[pltpu.VMEM((tm, d), BF16)], name="modproj_" + (act or "lin"))(x, shift, scale, g, w)


def _mmres_kernel(*refs, n_lhs, kw):
    a_refs = refs[:n_lhs]
    w_ref, res_ref, gate_ref, o_ref = refs[n_lhs:]
    acc = None
    for i, a in enumerate(a_refs):
        d = jnp.dot(a[...], w_ref[i * kw:(i + 1) * kw, :], preferred_element_type=F32)
        acc = d if acc is None else acc + d
    o_ref[...] = res_ref[...] + gate_ref[...] * acc


def mix_out(ys, yd_hlay, w, res, gate, tm=1024, tn=1024):
    b, l, kw = ys[0].shape
    n = w.shape[1]
    tm, tn = _tile(l, tm), _tile(n, tn)
    per_b = gate.shape[0] == b
    gmap = (lambda bi, i, j: (bi, 0, j)) if per_b else (lambda bi, i, j: (0, 0, j))
    y_spec = pl.BlockSpec((None, tm, kw), lambda bi, i, j: (bi, i, 0))
    yd_spec = pl.BlockSpec((None, tm, kw), lambda bi, i, j: (bi % 2, i, bi // 2))
    return _pc(functools.partial(_mmres_kernel, n_lhs=4, kw=kw),
               jax.ShapeDtypeStruct(res.shape, F32), (b, l // tm, n // tn),
               [y_spec, y_spec, y_spec, yd_spec,
                pl.BlockSpec((4 * kw, tn), lambda bi, i, j: (0, j)),
                pl.BlockSpec((None, tm, tn), lambda bi, i, j: (bi, i, j)),
                pl.BlockSpec((None, 1, tn), gmap)],
               pl.BlockSpec((None, tm, tn), lambda bi, i, j: (bi, i, j)), name="mix_out")(*ys, yd_hlay, w, res, gate)


def _mmresk_kernel(a_ref, w_ref, res_ref, gate_ref, o_ref, acc_ref):
    k = pl.program_id(3)

    @pl.when(k == 0)
    def _():
        acc_ref[...] = jnp.zeros_like(acc_ref)

    acc_ref[...] += jnp.dot(a_ref[...], w_ref[...], preferred_element_type=F32)

    @pl.when(k == pl.num_programs(3) - 1)
    def _():
        o_ref[...] = res_ref[...] + gate_ref[...] * acc_ref[...]


def mlp_out(a, w, res, gate, tm=1024, tn=1024, tk=2048):
    b, l, kk = a.shape
    n = w.shape[1]
    tm, tn, tk = _tile(l, tm), _tile(n, tn), _tile(kk, tk)
    per_b = gate.shape[0] == b
    gmap = (lambda bi, i, j, k: (bi, 0, j)) if per_b else (lambda bi, i, j, k: (0, 0, j))
    return _pc(_mmresk_kernel, jax.ShapeDtypeStruct(res.shape, F32), (b, l // tm, n // tn, kk // tk),
               [pl.BlockSpec((None, tm, tk), lambda bi, i, j, k: (bi, i, k)),
                pl.BlockSpec((tk, tn), lambda bi, i, j, k: (k, j)),
                pl.BlockSpec((None, tm, tn), lambda bi, i, j, k: (bi, i, j)),
                pl.BlockSpec((None, 1, tn), gmap)],
               pl.BlockSpec((None, tm, tn), lambda bi, i, j, k: (bi, i, j)),
               scratch=[pltpu.VMEM((tm, tn), F32)], name="mlp_out")(a, w, res, gate)


def _mla_prep_kernel(kv_ref, cq_ref, cs_ref, gkv_ref, gqa_ref, gkn_ref, gkt_ref, gqn_ref, gqt_ref,
                     wkv_ref, wq_ref, k_ref, v_ref, q_ref):
    lane = lax.broadcasted_iota(jnp.int32, (1, LANES), 1)
    low = (lane < MLA_ROPE).astype(F32)
    cs = cs_ref[...]
    kv = kv_ref[...].astype(F32)
    ckv = kv[:, :KV_LORA]
    ckvn = ckv * lax.rsqrt(jnp.mean(ckv * ckv, axis=-1, keepdims=True) + EPS) * gkv_ref[...]
    kvp = jnp.dot(ckvn.astype(BF16), wkv_ref[...], preferred_element_type=F32)
    pe2 = kv[:, KV_LORA:KV_LORA + LANES]
    ss_pe = jnp.sum(pe2 * pe2 * low, axis=-1, keepdims=True)
    pe_tab = cs * gkt_ref[...]
    for h in range(MLA_HEADS):
        knope = kvp[:, 256 * h:256 * h + 128]
        rs = lax.rsqrt((jnp.sum(knope * knope, axis=-1, keepdims=True) + ss_pe) * (1.0 / MLA_QK) + EPS)
        t = pe2 * pe_tab * rs
        t = (t + pltpu.roll(t, MLA_ROPE, axis=1)) * low
        k_ref[h, :, 0:128] = (knope * gkn_ref[...] * rs).astype(BF16)
        k_ref[h, :, 128:256] = t.astype(BF16)
        v_ref[h] = kvp[:, 256 * h + 128:256 * h + 256].astype(BF16)
    if q_ref is not None:
        cq = cq_ref[...].astype(F32)
        cqn = cq * lax.rsqrt(jnp.mean(cq * cq, axis=-1, keepdims=True) + EPS) * gqa_ref[...]
        qp = jnp.dot(cqn.astype(BF16), wq_ref[...], preferred_element_type=F32)
        q_tab = cs * gqt_ref[...]
        for h in range(MLA_HEADS):
            qn = qp[:, 256 * h:256 * h + 128]
            qe = qp[:, 256 * h + 128:256 * h + 256]
            ss = jnp.sum(qn * qn, axis=-1, keepdims=True) + jnp.sum(qe * qe * low, axis=-1, keepdims=True)
            rs = lax.rsqrt(ss * (1.0 / MLA_QK) + EPS) * (MLA_SCALE * LOG2E)
            t = qe * q_tab * rs
            t = t + pltpu.roll(t, MLA_ROPE, axis=1)
            q_ref[h, :, 0:128] = (qn * gqn_ref[...] * rs).astype(BF16)
            q_ref[h, :, 128:256] = t.astype(BF16)


def _mla_prep_kernel_noq(kv_ref, cs_ref, gkv_ref, gkn_ref, gkt_ref, wkv_ref, k_ref, v_ref):
    _mla_prep_kernel(kv_ref, None, cs_ref, gkv_ref, None, gkn_ref, gkt_ref, None, None, wkv_ref, None,
                     k_ref, v_ref, None)


def mla_prep(proj, cs, p, with_q, tm=512):
    b, l, _ = proj.shape
    tm = _tile(l, tm)
    row = lambda w: pl.BlockSpec((1, w), lambda bi, i: (0, 0))
    kv_spec = pl.BlockSpec((None, tm, GROUP_W), lambda bi, i: (bi, i, PB_KV))
    cq_spec = pl.BlockSpec((None, tm, GROUP_W), lambda bi, i: (bi, i, PB_CQ))
    cs_spec = pl.BlockSpec((tm, LANES), lambda bi, i: (i, 0))
    wkv_spec = pl.BlockSpec((KV_LORA, 1024), lambda bi, i: (0, 0))
    wq_spec = pl.BlockSpec((GROUP_W, 1024), lambda bi, i: (0, 0))
    hd = lambda w: pl.BlockSpec((None, MLA_HEADS, tm, w), lambda bi, i: (bi, 0, i, 0))
    k_sh = jax.ShapeDtypeStruct((b, MLA_HEADS, l, 256), BF16)
    v_sh = jax.ShapeDtypeStruct((b, MLA_HEADS, l, 128), BF16)
    if with_q:
        return _pc(_mla_prep_kernel, (k_sh, v_sh, k_sh), (b, l // tm),
                   [kv_spec, cq_spec, cs_spec, row(KV_LORA), row(GROUP_W), row(128), row(128), row(128), row(128),
                    wkv_spec, wq_spec],
                   (hd(256), hd(128), hd(256)), name="mla_prep")(
            proj, proj, cs, p["gkv"], p["gqa"], p["gkn"], p["gkt"], p["gqn"], p["gqt"], p["wkv"], p["wq"])
    k, v = _pc(_mla_prep_kernel_noq, (k_sh, v_sh), (b, l // tm),
               [kv_spec, cs_spec, row(KV_LORA), row(128), row(128), wkv_spec],
               (hd(256), hd(128)), name="mla_prep_kv")(proj, cs, p["gkv"], p["gkn"], p["gkt"], p["wkv"])
    return k, v, None


def _attn_kernel(*refs, n_seg, n_part, kv_chunk):
    q_ref = refs[0]
    o_ref = refs[1 + 2 * n_seg]
    rows = q_ref.shape[0] // n_part
    for part in range(n_part):
        sl = slice(part * rows, (part + 1) * rows)
        q = q_ref[sl, :]
        m = jnp.full((rows, 1), -jnp.inf, F32)
        den = jnp.zeros((rows, 1), F32)
        acc = jnp.zeros((rows, o_ref.shape[1]), F32)
        for i in range(n_seg):
            k_ref, v_ref = refs[1 + 2 * i], refs[2 + 2 * i]
            ch = min(kv_chunk, k_ref.shape[0])
            for c0 in range(0, k_ref.shape[0], ch):
                s = lax.dot_general(q, k_ref[c0:c0 + ch, :], (((1,), (1,)), ((), ())), preferred_element_type=F32)
                m_new = jnp.maximum(m, s.max(axis=-1, keepdims=True))
                alpha = jnp.exp2(m - m_new)
                p = jnp.exp2(s - m_new)
                den = alpha * den + p.sum(axis=-1, keepdims=True)
                acc = alpha * acc + jnp.dot(p.astype(BF16), v_ref[c0:c0 + ch, :], preferred_element_type=F32)
                m = m_new
        o_ref[sl, :] = (acc / den).astype(o_ref.dtype)


def attention(q, kvs, tq=1024, part_rows=512, kv_chunk=4096):
    b, h, l, _ = q.shape
    tq = _tile(l, tq)
    n_part = max(1, tq // part_rows)
    assert all(k.shape[2] % min(kv_chunk, k.shape[2]) == 0 for k, _ in kvs)
    specs = [pl.BlockSpec((None, None, tq, 256), lambda bi, hi, i: (bi, hi, i, 0))]
    args = [q]
    for k, v in kvs:
        lk = k.shape[2]
        specs += [pl.BlockSpec((None, None, lk, 256), lambda bi, hi, i: (bi, hi, 0, 0)),
                  pl.BlockSpec((None, None, lk, 128), lambda bi, hi, i: (bi, hi, 0, 0))]
        args += [k, v]
    return _pc(functools.partial(_attn_kernel, n_seg=len(kvs), n_part=n_part, kv_chunk=kv_chunk),
               jax.ShapeDtypeStruct((b, l, h * 128), BF16), (b, h, l // tq), specs,
               pl.BlockSpec((None, tq, 128), lambda bi, hi, i: (bi, i, hi)), name="attention")(*args)


def s5_tables(lam_re, lam_im, log_step, b_re, b_im, c_re, c_im):
    t = S5_T
    step = jnp.exp(log_step.astype(F32))[..., None]
    re, im = lam_re.astype(F32) * step, lam_im.astype(F32) * step
    d = jnp.arange(t + 1, dtype=F32)[:, None, None, None]
    mag = jnp.exp(d * re)
    ar, ai = mag * jnp.cos(d * im), mag * jnp.sin(d * im)
    lr, li = lam_re.astype(F32), lam_im.astype(F32)
    den = lr * lr + li * li
    er, ei = ar[1] - 1.0, ai[1]
    cr, ci = (er * lr + ei * li) / den, (ei * lr - er * li) / den
    br, bi = b_re.astype(F32), b_im.astype(F32)
    bbr = cr[..., None] * br - ci[..., None] * bi
    bbi = cr[..., None] * bi + ci[..., None] * br
    ccr, cci = c_re.astype(F32), c_im.astype(F32)
    car = ccr[None] * ar[:, :, :, None, :] - cci[None] * ai[:, :, :, None, :]
    cai = ccr[None] * ai[:, :, :, None, :] + cci[None] * ar[:, :, :, None, :]
    kd = jnp.einsum("dxgnp,xgpm->xgdnm", car[:t], bbr) - jnp.einsum("dxgnp,xgpm->xgdnm", cai[:t], bbi)
    kcat = jnp.stack([kd[0], kd[1][:, ::-1]], axis=1).transpose(0, 1, 4, 2, 3).reshape(S5_G, 2, S5_N, t * S5_N)
    pw = (jnp.arange(t - 1, -1, -1), jnp.arange(t))
    w_re, w_im = [], []
    for x in range(2):
        a_r, a_i = ar[pw[x], x], ai[pw[x], x]
        w_re.append(a_r[..., None] * bbr[x][None] - a_i[..., None] * bbi[x][None])
        w_im.append(a_r[..., None] * bbi[x][None] + a_i[..., None] * bbr[x][None])
    ws = jnp.concatenate(w_re + w_im, axis=2).transpose(1, 0, 3, 2).reshape(S5_G, t * S5_N, 4 * S5_P)
    pv = (jnp.arange(1, t + 1), jnp.arange(t, 0, -1))
    vm = jnp.concatenate([car[pv[0], 0], car[pv[1], 1], -cai[pv[0], 0], -cai[pv[1], 1]], axis=-1)
    vm = vm.transpose(1, 3, 0, 2).reshape(S5_G, 4 * S5_P, t * S5_N)
    dec = jnp.stack([jnp.concatenate([ar[t, 0], ar[t, 1]], axis=-1),
                     jnp.concatenate([ai[t, 0], ai[t, 1]], axis=-1)], axis=1)
    return kcat, ws.astype(BF16), vm.astype(BF16), dec


def _s5a_kernel(u_ref, k_ref, ws_ref, y_ref, s_ref, m_ref):
    kf, kb = k_ref[0], k_ref[1]
    lane = lax.broadcasted_iota(jnp.int32, kf.shape, 1)
    for s in range(S5_T):
        lo, hi = S5_N * s, S5_N * (s + 1)
        f = kf if s == 0 else jnp.where(lane >= lo, pltpu.roll(kf, lo, axis=1), 0.0)
        bk = kb if s == S5_T - 1 else jnp.where(lane < hi, pltpu.roll(kb, hi, axis=1), 0.0)
        m_ref[lo:hi, :] = (f + bk).astype(BF16)
    u = u_ref[...]
    y_ref[...] = jnp.dot(u, m_ref[...], preferred_element_type=F32)
    s_ref[...] = jnp.dot(u, ws_ref[...], preferred_element_type=F32)


def _s5b_kernel(yi_ref, s_ref, x0_ref, dec_ref, vm_ref, y_ref, xf_ref, xa_ref, xb_ref, *, nchunk, nbatch):
    cps = 8 // nbatch
    nslab = nchunk // cps
    ar = dec_ref[0:1, :]
    ai = dec_ref[1:2, :]
    fwd_lane = lax.broadcasted_iota(jnp.int32, (nbatch, 128), 1) < S5_P

    def step(jj, carry):
        xr, xi = carry
        rf = pl.ds(pl.multiple_of(jj * 8, 8), 8)
        rb = pl.ds(pl.multiple_of((nslab - 1 - jj) * 8, 8), 8)
        sf, sb = s_ref[rf, :], s_ref[rb, :]
        seen = []
        for i in range(cps):
            lo, hi = i * nbatch, (i + 1) * nbatch
            ml, mh = (cps - 1 - i) * nbatch, (cps - i) * nbatch
            seen.append(jnp.concatenate([xr, xi], axis=1))
            sr = jnp.where(fwd_lane, sf[lo:hi, :128], sb[ml:mh, :128])
            si = jnp.where(fwd_lane, sf[lo:hi, 128:], sb[ml:mh, 128:])
            xr, xi = ar * xr - ai * xi + sr, ar * xi + ai * xr + si
        xa_ref[rf, :] = jnp.concatenate(seen, axis=0)
        xb_ref[rb, :] = jnp.concatenate(seen[::-1], axis=0)
        return xr, xi

    x0 = x0_ref[0:nbatch, :]
    xr, xi = lax.fori_loop(0, nslab, step, (x0[:, :128], x0[:, 128:]), unroll=2)
    xf_ref[...] = jnp.zeros_like(xf_ref)
    xf_ref[0:nbatch, :] = jnp.concatenate([xr, xi], axis=1)
    pick = (lax.broadcasted_iota(jnp.int32, xa_ref.shape, 1) % 128) < S5_P
    xin = jnp.where(pick, xa_ref[...], xb_ref[...])
    y = yi_ref[...] + jnp.dot(xin.astype(BF16), vm_ref[...], preferred_element_type=F32)
    y_ref[...] = y.astype(y_ref.dtype)


def s5_scan(u, x0, tabs):
    kcat, ws, vm, dec = tabs
    b, l, _ = u.shape
    j = l // S5_T
    r = b * j
    assert 8 % b == 0 and j % (8 // b) == 0
    ug = u.astype(BF16).reshape(b, j, S5_T, S5_G, S5_N).transpose(3, 1, 0, 2, 4).reshape(S5_G, r, GROUP_W)
    g3 = lambda w: pl.BlockSpec((None, r, w), lambda g: (g, 0, 0))
    yi, s = _pc(_s5a_kernel,
                (jax.ShapeDtypeStruct((S5_G, r, GROUP_W), F32), jax.ShapeDtypeStruct((S5_G, r, 256), F32)),
                (S5_G,),
                [g3(GROUP_W), pl.BlockSpec((None, 2, S5_N, GROUP_W), lambda g: (g, 0, 0, 0)),
                 pl.BlockSpec((None, GROUP_W, 256), lambda g: (g, 0, 0))],
                (g3(GROUP_W), g3(256)),
                scratch=[pltpu.VMEM((GROUP_W, GROUP_W), BF16)], name="s5a")(ug, kcat, ws)
    y, xf = _pc(functools.partial(_s5b_kernel, nchunk=j, nbatch=b),
                (jax.ShapeDtypeStruct((S5_G, r, GROUP_W), BF16), jax.ShapeDtypeStruct((S5_G, 8, 256), F32)),
                (S5_G,),
                [g3(GROUP_W), g3(256),
                 pl.BlockSpec((None, 8, 256), lambda g: (g, 0, 0)),
                 pl.BlockSpec((None, 2, 128), lambda g: (g, 0, 0)),
                 pl.BlockSpec((None, 256, GROUP_W), lambda g: (g, 0, 0))],
                (g3(GROUP_W), pl.BlockSpec((None, 8, 256), lambda g: (g, 0, 0))),
                scratch=[pltpu.VMEM((r, 256), F32), pltpu.VMEM((r, 256), F32)], name="s5b")(yi, s, x0, dec, vm)
    y = y.reshape(S5_G, j, b, S5_T, S5_N).transpose(2, 1, 3, 0, 4).reshape(b, l, GROUP_W)
    return y, xf


def _s5_out_kernel(y_ref, u_ref, d_ref, w_ref, b_ref, o_ref):
    y = jax.nn.gelu(y_ref[...].astype(F32) + d_ref[...] * u_ref[...].astype(F32))
    z = jnp.dot(y.astype(BF16), w_ref[...], preferred_element_type=F32) + b_ref[...]
    o_ref[...] = (y * jax.nn.sigmoid(z)).astype(o_ref.dtype)


def s5_out(y, proj, d, w, bias, tm=512):
    b, l, _ = y.shape
    tm = _tile(l, tm)
    row = pl.BlockSpec((1, GROUP_W), lambda bi, i: (0, 0))
    return _pc(_s5_out_kernel, jax.ShapeDtypeStruct((b, l, GROUP_W), BF16), (b, l // tm),
               [pl.BlockSpec((None, tm, GROUP_W), lambda bi, i: (bi, i, 0)),
                pl.BlockSpec((None, tm, GROUP_W), lambda bi, i: (bi, i, PB_S5U)),
                row, pl.BlockSpec((GROUP_W, GROUP_W), lambda bi, i: (0, 0)), row],
               pl.BlockSpec((None, tm, GROUP_W), lambda bi, i: (bi, i, 0)), name="s5_out")(y, proj, d, w, bias)


def _hgrn_kernel(z_ref, i_ref, q_ref, la_ref, lb_ref, s0_ref, o_ref, sfin_ref, st_ref, *, reverse, nsub):
    c = HG_SUB

    @pl.when(pl.program_id(1) == 0)
    def _():
        st_ref[...] = s0_ref[...]

    hlf = c // 2
    ones = jnp.ones((HG_D, HG_D), BF16)
    row = lax.broadcasted_iota(jnp.int32, (c, HG_D), 0)
    row8 = lax.broadcasted_iota(jnp.int32, (hlf, HG_D), 0)

    def sub(jj, carry):
        j = (nsub - 1 - jj) if reverse else jj
        rows = pl.ds(pl.multiple_of(j * c, c), c)
        for h in range(HG_HEADS):
            sl = slice(HG_D * h, HG_D * (h + 1))
            z = z_ref[rows, sl].astype(F32)
            v = i_ref[rows, sl].astype(F32)
            q = _silu(q_ref[rows, sl].astype(F32))
            a = la_ref[:, sl]
            bb = lb_ref[:, sl] + (jnp.minimum(z, 0.0) - jnp.log1p(jnp.exp(-jnp.abs(z))))
            g = jnp.maximum(a, bb) + jnp.log1p(jnp.exp(-jnp.abs(a - bb)))
            k = 1.0 - jnp.exp(g)
            gc = g
            for sh in (1, 2, 4, 8):
                if reverse:
                    gc = gc + jnp.where(row < c - sh, pltpu.roll(gc, c - sh, axis=0), 0.0)
                else:
                    gc = gc + jnp.where(row >= sh, pltpu.roll(gc, sh, axis=0), 0.0)
            gc = gc * LOG2E
            gk = gc - jnp.log(jnp.maximum(k, 0.0)) * LOG2E
            blocks, meta = [], []
            for s in range(c):
                for half in range(2):
                    r0 = half * hlf
                    if (r0 > s) if reverse else (r0 + hlf - 1 < s):
                        continue
                    e = gc[r0:r0 + hlf, :] - gk[s:s + 1, :]
                    if not ((r0 + hlf - 1 <= s) if reverse else (r0 >= s)):
                        keep = (row8 + r0 <= s) if reverse else (row8 + r0 >= s)
                        e = jnp.where(keep, e, -1e30)
                    blocks.append(q[r0:r0 + hlf, :] * jnp.exp2(e))
                    meta.append((half, s))
            rsum = jnp.dot(jnp.concatenate(blocks, axis=0).astype(BF16), ones, preferred_element_type=F32)
            halves = [jnp.zeros((hlf, HG_D), F32), jnp.zeros((hlf, HG_D), F32)]
            for i, (half, s) in enumerate(meta):
                halves[half] = halves[half] + rsum[hlf * i:hlf * (i + 1), :] * v[s:s + 1, :]
            o = jnp.concatenate(halves, axis=0)
            st = st_ref[h]
            o = o + lax.dot_general((q * jnp.exp2(gc)).astype(BF16), st.astype(BF16),
                                    (((1,), (1,)), ((), ())), preferred_element_type=F32)
            gl = gc[0:1, :] if reverse else gc[c - 1:c, :]
            kd = k * jnp.exp2(gl - gc)
            ds = lax.dot_general(v.astype(BF16), kd.astype(BF16), (((0,), (0,)), ((), ())),
                                 preferred_element_type=F32)
            st_ref[h] = st * jnp.exp2(gl) + ds
            o_ref[rows, sl] = o.astype(o_ref.dtype)
        return carry

    lax.fori_loop(0, nsub, sub, 0, unroll=4)

    @pl.when(pl.program_id(1) == pl.num_programs(1) - 1)
    def _():
        sfin_ref[...] = st_ref[...]


def hgrn_dir(proj, la, lb1, s0, reverse, tl=256):
    b, l, _ = proj.shape
    tl = _tile(l, tl)
    nb = l // tl
    blk = (lambda i: nb - 1 - i) if reverse else (lambda i: i)
    col = lambda c: pl.BlockSpec((None, tl, GROUP_W), lambda bi, i: (bi, blk(i), c))
    row = pl.BlockSpec((1, GROUP_W), lambda bi, i: (0, 0))
    st = pl.BlockSpec((None, HG_HEADS, HG_D, HG_D), lambda bi, i: (bi, 0, 0, 0))
    return _pc(functools.partial(_hgrn_kernel, reverse=reverse, nsub=tl // HG_SUB),
               (jax.ShapeDtypeStruct((b, l, GROUP_W), BF16), jax.ShapeDtypeStruct(s0.shape, F32)),
               (b, nb),
               [col(PB_HGB if reverse else PB_HGF), col(PB_HGI), col(PB_HGQ), row, row, st],
               (pl.BlockSpec((None, tl, GROUP_W), lambda bi, i: (bi, blk(i), 0)), st),
               scratch=[pltpu.VMEM((HG_HEADS, HG_D, HG_D), F32)],
               name="hgrn_bwd" if reverse else "hgrn_fwd")(proj, proj, proj, la, lb1, s0)


def _hg_out_kernel(of_ref, ob_ref, g_ref, gn_ref, y_ref):
    o = of_ref[...].astype(F32) + ob_ref[...].astype(F32)
    gate = _silu(g_ref[...].astype(F32))
    for h in range(HG_HEADS):
        sl = slice(HG_D * h, HG_D * (h + 1))
        oh = o[:, sl]
        y = oh * lax.rsqrt(jnp.mean(oh * oh, axis=-1, keepdims=True) + EPS) * gn_ref[...]
        y_ref[:, sl] = (y * gate[:, sl]).astype(y_ref.dtype)


def hg_out(of, ob, proj, gn, tm=512):
    b, l, _ = of.shape
    tm = _tile(l, tm)
    t3 = lambda c: pl.BlockSpec((None, tm, GROUP_W), lambda bi, i: (bi, i, c))
    return _pc(_hg_out_kernel, jax.ShapeDtypeStruct((b, l, GROUP_W), BF16), (b, l // tm),
               [t3(0), t3(0), t3(PB_HGG), pl.BlockSpec((1, HG_D), lambda bi, i: (0, 0))],
               t3(0), name="hg_out")(of, ob, proj, gn)


def _fft_factors(n_fft):
    n1 = {8192: 64, 512: 32, 1024: 32, 2048: 32, 4096: 64}[n_fft]
    return n1, n_fft // n1


def _dft_tables(na, nb, ka, mb, inverse):
    n = na * nb
    sgn = 1.0 if inverse else -1.0

    def stacked(phase, scale=1.0):
        ang = sgn * 2.0 * np.pi * (phase % n) / n
        fr, fi = np.cos(ang) * scale, np.sin(ang) * scale
        return np.concatenate([np.concatenate([fr, -fi], axis=-1), np.concatenate([fi, fr], axis=-1)], axis=-2)

    fa = stacked(np.outer(np.arange(na), np.arange(ka)) * nb)
    p, q, b = np.arange(na)[:, None, None], np.arange(mb)[None, :, None], np.arange(nb)[None, None, :]
    fb = stacked(b * p + b * q * na, (1.0 / n) if inverse else 1.0)
    return jnp.asarray(fa, BF16), jnp.asarray(fb, BF16)


def _fft_a_kernel(z_ref, f_ref, o_ref, *, g, na, is_complex):
    f = f_ref[...]
    for i in range(g):
        z = z_ref[:, i] if is_complex else z_ref[i]
        if is_complex:
            z = z.reshape(z.shape[0] * z.shape[1], z.shape[2])
        r = jnp.dot(f, z, preferred_element_type=F32)
        o_ref[0, i] = r[:na]
        o_ref[1, i] = r[na:]


def fft_stage_a(z, fa, g=8):
    is_complex = z.ndim == 4
    nb, ka, pc = z.shape[-3:]
    na = fa.shape[0] // 2
    g = min(g, nb)
    zspec = (pl.BlockSpec((2, g, ka, pc), lambda j: (0, j, 0, 0)) if is_complex
             else pl.BlockSpec((g, ka, pc), lambda j: (j, 0, 0)))
    return _pc(functools.partial(_fft_a_kernel, g=g, na=na, is_complex=is_complex),
               jax.ShapeDtypeStruct((2, nb, na, pc), F32), (nb // g,),
               [zspec, pl.BlockSpec(fa.shape, lambda j: (0, 0))],
               pl.BlockSpec((2, g, na, pc), lambda j: (0, j, 0, 0)), name="fft_a")(z, fa)


def _fft_b_kernel(*refs, mode, mb, kb):
    a_ref, f_ref = refs[0], refs[1]
    o_ref = refs[-1]
    nb = a_ref.shape[1]
    for i in range(kb):
        f = f_ref[i].astype(F32)
        z = (jnp.dot(f[:, :nb], a_ref[0, :, i, :], preferred_element_type=F32)
             + jnp.dot(f[:, nb:], a_ref[1, :, i, :], preferred_element_type=F32))
        zr, zi = z[:mb], z[mb:]
        if mode == "filter":
            h_ref = refs[2]
            hr, hi = h_ref[0, i].astype(F32), h_ref[1, i].astype(F32)
            zr, zi = zr * hr - zi * hi, zr * hi + zi * hr
        elif mode == "final":
            v_ref, g_ref, b_ref = refs[2], refs[3], refs[4]
            zr = g_ref[0, i].astype(F32) * (zr + v_ref[0, i].astype(F32) * b_ref[...])
            zi = g_ref[1, i].astype(F32) * (zi + v_ref[1, i].astype(F32) * b_ref[...])
        o_ref[0, i] = zr.astype(o_ref.dtype)
        o_ref[1, i] = zi.astype(o_ref.dtype)


def fft_stage_b(a, fb, mode, extra=(), order=0, kb=8):
    _, nb, na, pc = a.shape
    mb = fb.shape[1] // 2
    c = GROUP_W
    kb = min(kb, na)
    blk = pl.BlockSpec((2, kb, mb, c), lambda k, p: (0, k, 0, p))
    specs = [pl.BlockSpec((2, nb, kb, c), lambda k, p: (0, 0, k, p)),
             pl.BlockSpec((kb, 2 * mb, 2 * nb), lambda k, p: (k, 0, 0))]
    args = [a, fb]
    if mode == "filter":
        (h,) = extra
        specs.append(pl.BlockSpec((2, kb, mb, c), lambda k, p: (0, k, 0, order)))
        args.append(h)
    elif mode == "final":
        v, gate, bias = extra
        specs += [blk, blk, pl.BlockSpec((1, c), lambda k, p: (0, 0))]
        args += [v, gate, bias]
    return _pc(functools.partial(_fft_b_kernel, mode=mode, mb=mb, kb=kb),
               jax.ShapeDtypeStruct((2, na, mb, pc), BF16), (na // kb, pc // c), specs, blk,
               name="fft_b_" + mode)(*args)


def _hy_filter_kernel(emb_ref, w1_ref, b1_ref, w2_ref, b2_ref, w3_ref, win_ref, o_ref, h_ref, *, na):
    hp = lax.Precision.HIGHEST

    @pl.when(pl.program_id(0) == 0)
    def _():
        h = jnp.sin(jnp.dot(emb_ref[...], w1_ref[...], precision=hp, preferred_element_type=F32) + b1_ref[...])
        h = jnp.sin(jnp.dot(h, w2_ref[...], precision=hp, preferred_element_type=F32) + b2_ref[...])
        row = lax.broadcasted_iota(jnp.int32, h.shape, 0)
        causal = (row & (na - 1)) < na // 2
        h_ref[...] = jnp.concatenate([jnp.where(causal, h, 0.0), jnp.where(causal, 0.0, h)], axis=1).astype(BF16)

    taps = jnp.dot(h_ref[...], w3_ref[...], preferred_element_type=F32) * win_ref[...]
    o_ref[...] = (taps / jnp.sum(jnp.abs(taps), axis=0, keepdims=True)).astype(o_ref.dtype)


def hyena_filters(n, w1, b1, w2, b2, w3):
    na, nb = _fft_factors(2 * n)
    t = np.arange(n, dtype=np.float32)
    t_norm = t / max(n - 1, 1)
    bands = np.linspace(1e-4, HY_BANDS - 1, HY_BANDS, dtype=np.float32)
    ang = (2.0 * math.pi * t / n)[:, None] * bands[None, :]
    emb = np.concatenate([t_norm[:, None], np.cos(ang), -np.sin(ang)], axis=-1).astype(np.float32)
    emb = np.pad(emb, ((0, 0), (0, LANES - HY_EMB)))
    deltas = np.linspace(math.log(HY_TARGET) / HY_SLOW, math.log(HY_TARGET) / HY_FAST, GROUP_W, dtype=np.float32)
    win = (np.exp(-t_norm[:, None] * np.abs(deltas)[None, :]) + np.float32(HY_SHIFT)).astype(np.float32)
    back = np.concatenate([[0], np.arange(n - 2, -1, -1)])
    win_b = win[back]
    win_b[0] = 0.0
    order = (np.arange(na)[None, :] * nb + np.arange(nb)[:, None]).reshape(-1)
    emb2 = jnp.asarray(np.concatenate([emb, emb[back]], axis=0)[order])
    win2 = jnp.asarray(np.concatenate([win, win_b], axis=0)[order])
    w1p = jnp.pad(w1.astype(F32), ((0, LANES - HY_EMB), (0, 0)))
    w3r = w3.astype(F32).reshape(HY_HIDDEN, 2, 2, GROUP_W)
    w3c = jnp.concatenate([w3r[:, :, 0].reshape(HY_HIDDEN, 2 * GROUP_W),
                           w3r[:, :, 1].reshape(HY_HIDDEN, 2 * GROUP_W)], axis=0).astype(BF16)
    tc = LANES
    full = lambda a: pl.BlockSpec(a.shape, lambda j: (0, 0))
    b1r, b2r = b1.reshape(1, -1).astype(F32), b2.reshape(1, -1).astype(F32)
    w2f = w2.astype(F32)
    return _pc(functools.partial(_hy_filter_kernel, na=na),
               jax.ShapeDtypeStruct((2 * n, 2 * GROUP_W), BF16), (2 * GROUP_W // tc,),
               [full(emb2), full(w1p), full(b1r), full(w2f), full(b2r),
                pl.BlockSpec((2 * HY_HIDDEN, tc), lambda j: (0, j)),
                pl.BlockSpec((2 * n, tc), lambda j: (0, j % (GROUP_W // tc)))],
               pl.BlockSpec((2 * n, tc), lambda j: (0, j)),
               scratch=[pltpu.VMEM((2 * n, 2 * HY_HIDDEN), BF16)], name="hy_filter")(
        emb2, w1p, b1r, w2f, b2r, w3c, win2)


def hyena_spectrum(n, w1, b1, w2, b2, w3):
    na, nb = _fft_factors(2 * n)
    circ = hyena_filters(n, w1, b1, w2, b2, w3).reshape(nb, na, 2 * GROUP_W)
    fa, fb = _dft_tables(na, nb, na, nb, inverse=False)
    return fft_stage_b(fft_stage_a(circ, fa[:, :na]), fb, "plain")


def _shortconv_kernel(p_ref, w_ref, b_ref, o_ref):
    p = p_ref[...].astype(F32)
    n = p.shape[0]
    row = lax.broadcasted_iota(jnp.int32, p.shape, 0)
    prev = jnp.where(row >= 1, pltpu.roll(p, 1, axis=0), 0.0)
    nxt = jnp.where(row < n - 1, pltpu.roll(p, n - 1, axis=0), 0.0)
    o_ref[...] = (prev * w_ref[0:1, :] + p * w_ref[1:2, :] + nxt * w_ref[2:3, :] + b_ref[...]).astype(o_ref.dtype)


def hyena_shortconv(proj, w, bias):
    b, n, _ = proj.shape
    return _pc(_shortconv_kernel, jax.ShapeDtypeStruct((3, 2, n, (b // 2) * GROUP_W), BF16), (b, 3),
               [pl.BlockSpec((None, n, GROUP_W), lambda bi, j: (bi, 0, PB_HY + j)),
                pl.BlockSpec((3, GROUP_W), lambda bi, j: (0, j)),
                pl.BlockSpec((1, GROUP_W), lambda bi, j: (0, j))],
               pl.BlockSpec((None, None, n, GROUP_W), lambda bi, j: (j, bi % 2, 0, bi // 2)),
               name="shortconv")(proj, w, bias)


def hyena_mixer(proj, spec, conv_w, conv_b, bias):
    b, n, _ = proj.shape
    na, nb = _fft_factors(2 * n)
    pc = (b // 2) * GROUP_W
    u = hyena_shortconv(proj, conv_w, conv_b).reshape(3, 2, na // 2, nb, pc).swapaxes(2, 3)
    fa, fb = _dft_tables(na, nb, na // 2, nb, inverse=False)
    ga, gb = _dft_tables(nb, na, nb, na // 2, inverse=True)
    z = u[0]
    for o in range(2):
        zf = fft_stage_b(fft_stage_a(z, fa), fb, "filter", (spec,), order=o)
        z = fft_stage_b(fft_stage_a(zf, ga), gb, "final", (z, u[1 + o], bias[o:o + 1]))
    return z.swapaxes(1, 2).reshape(2, n, pc)


def _rot_cols(w):
    return jnp.concatenate([-w[..., 16:32], w[..., 0:16], -w[..., 48:64], w[..., 32:48]], axis=-1)


def _rot_perm(g):
    return jnp.concatenate([g[16:32], g[0:16], g[48:64], g[32:48]])


def _prep_layer(w_in, w_uq, w_ukv, q_a_g, kv_a_g, q_g, k_g):
    d = w_in.shape[0]
    ckv, krope, rest = w_in[:, :256], w_in[:, 256:320], w_in[:, 320:]
    w_in_p = jnp.concatenate([rest, ckv, krope, _rot_cols(krope), jnp.zeros((d, 128), w_in.dtype)], axis=1)
    wq = w_uq.reshape(GROUP_W, MLA_HEADS, MLA_QK)
    wq = jnp.concatenate([wq, _rot_cols(wq[..., MLA_NOPE:])], axis=-1).reshape(GROUP_W, MLA_HEADS * 256)
    r2 = lambda a: a.reshape(1, -1).astype(F32)
    mla = dict(wkv=w_ukv.astype(BF16), wq=wq.astype(BF16), gkv=r2(kv_a_g), gqa=r2(q_a_g),
               gkn=r2(k_g[:MLA_NOPE]), gkt=r2(jnp.concatenate([k_g[MLA_NOPE:], _rot_perm(k_g[MLA_NOPE:])])),
               gqn=r2(q_g[:MLA_NOPE]), gqt=r2(jnp.concatenate([q_g[MLA_NOPE:], _rot_perm(q_g[MLA_NOPE:])])))
    return w_in_p.astype(BF16), mla


def _rope_table(n_rows):
    row = np.repeat(np.arange(n_rows, dtype=np.float32), GRID_W)
    col = np.tile(np.arange(GRID_W, dtype=np.float32), n_rows)
    n_freq = MLA_ROPE // 4
    inv_freq = (ROPE_BASE ** (-np.arange(n_freq, dtype=np.float32) / n_freq)).astype(np.float32)
    ang_r, ang_c = row[:, None] * inv_freq, col[:, None] * inv_freq
    ang = np.concatenate([ang_r, ang_r, ang_c, ang_c], axis=-1)
    return jnp.asarray(np.concatenate([np.cos(ang), np.sin(ang)], axis=-1), F32)


def kernel(x, c, ctx, c_ctx, w_mod, b_mod, norm1, norm2, w_in, w_out, mla_q_a_norm, mla_kv_a_norm, mla_w_uq, mla_w_ukv, mla_q_norm, mla_k_norm, s5_lam_re, s5_lam_im, s5_log_step, s5_b_re, s5_b_im, s5_c_re, s5_c_im, s5_d, s5_w_glu, s5_b_glu, hg_lower_bounds, hg_o_norm, hy_conv_w, hy_conv_b, hy_w1, hy_b1, hy_w2, hy_b2, hy_w3, hy_bias, mlp_w1, mlp_w2):
    bsz, seq, dm = x.shape
    n_ctx = ctx.shape[1]
    depth = w_mod.shape[0]
    assert bsz % 2 == 0 and bsz <= 8 and seq % GRID_W == 0 and S5_T * S5_N == GROUP_W
    r2 = lambda a: a.reshape(1, -1).astype(F32)

    cs_l = _rope_table(seq // GRID_W)
    cs_c = jnp.concatenate([jnp.ones((n_ctx, MLA_ROPE), F32), jnp.zeros((n_ctx, MLA_ROPE), F32)], axis=-1)
    sm = jax.nn.softmax(hg_lower_bounds.astype(F32), axis=1)
    lower = jnp.clip(jnp.cumsum(sm, axis=1) - sm[:, :1], 0.0, 1.0)
    c8 = jnp.concatenate([c.astype(F32), c_ctx.astype(F32)[None], jnp.zeros((7 - bsz, dm), F32)], axis=0)
    zero_state = jnp.zeros((bsz, HG_HEADS, HG_D, HG_D), F32)
    zero_x = jnp.zeros((S5_G, 8, 256), F32)

    xc = ctx
    for l in range(depth):
        last = l == depth - 1
        mod = modvec(c8, w_mod[l].astype(F32), r2(b_mod[l]))
        mod_l = [mod[:bsz, i * dm:(i + 1) * dm][:, None, :] for i in range(N_MOD)]
        mod_c = [mod[bsz:bsz + 1, i * dm:(i + 1) * dm][:, None, :] for i in range(N_MOD)]
        w_in_p, mla_p = _prep_layer(w_in[l], mla_w_uq[l], mla_w_ukv[l], mla_q_a_norm[l], mla_kv_a_norm[l],
                                    mla_q_norm[l], mla_k_norm[l])
        pl_ = modproj(x, mod_l[0], mod_l[1], r2(norm1[l]), w_in_p, BF16, tm=1024)
        pc_ = modproj(xc, mod_c[0], mod_c[1], r2(norm1[l]), w_in_p, BF16, tm=1024)

        k_c, v_c, q_c = mla_prep(pc_, cs_c, mla_p, with_q=not last)
        k_l, v_l, q_l = mla_prep(pl_, cs_l, mla_p, with_q=True)
        y_a = attention(q_l, [(k_l, v_l), (k_c, v_c)])

        tabs = s5_tables(s5_lam_re[l], s5_lam_im[l], s5_log_step[l], s5_b_re[l], s5_b_im[l], s5_c_re[l], s5_c_im[l])
        yc_s5, fin = s5_scan(pc_[..., PB_S5U * GROUP_W:(PB_S5U + 1) * GROUP_W], zero_x, tabs)
        yl_s5, _ = s5_scan(pl_[..., PB_S5U * GROUP_W:(PB_S5U + 1) * GROUP_W], fin, tabs)
        glu_w, glu_b = s5_w_glu[l].astype(BF16), r2(s5_b_glu[l])
        y_b = s5_out(yl_s5, pl_, r2(s5_d[l]), glu_w, glu_b)

        la_f, lb_f = r2(jnp.log(lower[0, l])), r2(jnp.log1p(-lower[0, l]))
        la_b, lb_b = r2(jnp.log(lower[1, l])), r2(jnp.log1p(-lower[1, l]))
        oc_f, s_f = hgrn_dir(pc_, la_f, lb_f, zero_state, reverse=False)
        oc_b, s_b = hgrn_dir(pc_, la_b, lb_b, zero_state, reverse=True)
        ol_f, _ = hgrn_dir(pl_, la_f, lb_f, s_f, reverse=False)
        ol_b, _ = hgrn_dir(pl_, la_b, lb_b, s_b, reverse=True)
        y_c = hg_out(ol_f, ol_b, pl_, r2(hg_o_norm[l]))

        hy_args = (hy_conv_w[l].astype(F32), r2(hy_conv_b[l]), hy_bias[l].astype(F32))
        spec_l = hyena_spectrum(seq, hy_w1[l], hy_b1[l], hy_w2[l], hy_b2[l], hy_w3[l])
        y_d = hyena_mixer(pl_, spec_l, *hy_args)

        w_out_b = w_out[l].astype(BF16)
        w1_b, w2_b = mlp_w1[l].astype(BF16), mlp_w2[l].astype(BF16)
        x = mix_out([y_a, y_b, y_c], y_d, w_out_b, x, mod_l[2])
        h1 = modproj(x, mod_l[3], mod_l[4], r2(norm2[l]), w1_b, BF16, act="relu2", tm=1024, tn=1024)
        x = mlp_out(h1, w2_b, x, mod_l[5])

        if not last:
            y_ac = attention(q_c, [(k_c, v_c)])
            y_bc = s5_out(yc_s5, pc_, r2(s5_d[l]), glu_w, glu_b)
            y_cc = hg_out(oc_f, oc_b, pc_, r2(hg_o_norm[l]))
            spec_c = hyena_spectrum(n_ctx, hy_w1[l], hy_b1[l], hy_w2[l], hy_b2[l], hy_w3[l])
            y_dc = hyena_mixer(pc_, spec_c, *hy_args)
            xc = mix_out([y_ac, y_bc, y_cc], y_dc, w_out_b, xc, mod_c[2])
            h1c = modproj(xc, mod_c[3], mod_c[4], r2(norm2[l]), w1_b, BF16, act="relu2", tm=1024)
            xc = mlp_out(h1c, w2_b, xc, mod_c[5])
    return x
```

```python
import functools
import math

import numpy as np
import jax
import jax.numpy as jnp
from jax import lax
from jax.experimental import pallas as pl
from jax.experimental.pallas import tpu as pltpu

F32 = jnp.float32
BF16 = jnp.bfloat16
EPS = 1e-6

GRID_W = 64
GROUP_W = 512
N_MOD = 6
MLA_NOPE = 128
MLA_ROPE = 64
MLA_QK = MLA_NOPE + MLA_ROPE
MLA_HEADS = 4
KV_LORA = 256
MLA_SCALE = 1.0 / math.sqrt(MLA_QK)
LOG2E = 1.0 / math.log(2.0)
ROPE_BASE = 10000.0
S5_G = 32
S5_N = 16
S5_P = 64
S5_T = 32
HG_HEADS = 4
HG_D = 128
HG_SUB = 16
HY_EMB = 33
HY_BANDS = 16
HY_HIDDEN = 64
HY_TARGET = 1e-2
HY_FAST = 0.3
HY_SLOW = 1.5
HY_SHIFT = 0.05

LANES = 128
VMEM_LIMIT_MB = 56

PB_S5U, PB_HGF, PB_HGB, PB_HGI, PB_CQ, PB_HGQ, PB_HGG, PB_HY, PB_KV = 0, 1, 2, 3, 4, 5, 6, 7, 10
N_PROJ = 11 * GROUP_W


def _pc(kernel, out_shape, grid, in_specs, out_specs, scratch=(), vmem_mb=VMEM_LIMIT_MB, name=None):
    return pl.pallas_call(
        kernel, out_shape=out_shape, grid=grid, in_specs=in_specs, out_specs=out_specs,
        scratch_shapes=list(scratch), name=name,
        compiler_params=pltpu.CompilerParams(
            dimension_semantics=("arbitrary",) * len(grid), vmem_limit_bytes=vmem_mb << 20))


def _tile(n, pref):
    t = min(n, pref)
    while n % t:
        t //= 2
    return t


def _silu(x):
    return x * jax.nn.sigmoid(x)


def _modvec_kernel(c_ref, w_ref, b_ref, o_ref):
    s = _silu(c_ref[...])
    o_ref[...] = jnp.dot(s.astype(BF16), w_ref[...].astype(BF16), preferred_element_type=F32) + b_ref[...]


def modvec(c8, w, b):
    d, n = w.shape
    tn = _tile(n, 1536)
    return _pc(_modvec_kernel, jax.ShapeDtypeStruct((8, n), F32), (n // tn,),
               [pl.BlockSpec((8, d), lambda j: (0, 0)),
                pl.BlockSpec((d, tn), lambda j: (0, j)),
                pl.BlockSpec((1, tn), lambda j: (0, j))],
               pl.BlockSpec((8, tn), lambda j: (0, j)), name="modvec")(c8, w, b)


def _modproj_kernel(x_ref, sh_ref, sc_ref, g_ref, w_ref, o_ref, h_ref, *, act):
    @pl.when(pl.program_id(2) == 0)
    def _():
        x = x_ref[...]
        ms = jnp.mean(x * x, axis=-1, keepdims=True)
        y = x * lax.rsqrt(ms + EPS) * g_ref[...]
        h_ref[...] = (y * (1.0 + sc_ref[...]) + sh_ref[...]).astype(BF16)

    acc = jnp.dot(h_ref[...], w_ref[...], preferred_element_type=F32)
    if act == "relu2":
        acc = jnp.square(jnp.maximum(acc, 0.0))
    o_ref[...] = acc.astype(o_ref.dtype)


def modproj(x, shift, scale, g, w, out_dtype, act=None, tm=512, tn=512):
    b, l, d = x.shape
    n = w.shape[1]
    tm, tn = _tile(l, tm), _tile(n, tn)
    per_b = shift.shape[0] == b
    mod_map = (lambda bi, i, j: (bi, 0, 0)) if per_b else (lambda bi, i, j: (0, 0, 0))
    return _pc(functools.partial(_modproj_kernel, act=act),
               jax.ShapeDtypeStruct((b, l, n), out_dtype), (b, l // tm, n // tn),
               [pl.BlockSpec((None, tm, d), lambda bi, i, j: (bi, i, 0)),
                pl.BlockSpec((None, 1, d), mod_map),
                pl.BlockSpec((None, 1, d), mod_map),
                pl.BlockSpec((1, d), lambda bi, i, j: (0, 0)),
                pl.BlockSpec((d, tn), lambda bi, i, j: (0, j))],
               pl.BlockSpec((None, tm, tn), lambda bi, i, j: (bi, i, j)),
               scratch=[pltpu.VMEM((tm, d), BF16)], name="modproj_" + (act or "lin"))(x, shift, scale, g, w)


def _mmres_kernel(*refs, n_lhs, kw):
    a_refs = refs[:n_lhs]
    w_ref, res_ref, gate_ref, o_ref = refs[n_lhs:]
    acc = None
    for i, a in enumerate(a_refs):
        d = jnp.dot(a[...], w_ref[i * kw:(i + 1) * kw, :], preferred_element_type=F32)
        acc = d if acc is None else acc + d
    o_ref[...] = res_ref[...] + gate_ref[...] * acc


def mix_out(ys, yd_hlay, w, res, gate, tm=1024, tn=1024):
    b, l, kw = ys[0].shape
    n = w.shape[1]
    tm, tn = _tile(l, tm), _tile(n, tn)
    per_b = gate.shape[0] == b
    gmap = (lambda bi, i, j: (bi, 0, j)) if per_b else (lambda bi, i, j: (0, 0, j))
    y_spec = pl.BlockSpec((None, tm, kw), lambda bi, i, j: (bi, i, 0))
    yd_spec = pl.BlockSpec((None, tm, kw), lambda bi, i, j: (bi % 2, i, bi // 2))
    return _pc(functools.partial(_mmres_kernel, n_lhs=4, kw=kw),
               jax.ShapeDtypeStruct(res.shape, F32), (b, l // tm, n // tn),
               [y_spec, y_spec, y_spec, yd_spec,
                pl.BlockSpec((4 * kw, tn), lambda bi, i, j: (0, j)),
                pl.BlockSpec((None, tm, tn), lambda bi, i, j: (bi, i, j)),
                pl.BlockSpec((None, 1, tn), gmap)],
               pl.BlockSpec((None, tm, tn), lambda bi, i, j: (bi, i, j)), name="mix_out")(*ys, yd_hlay, w, res, gate)


def _mmresk_kernel(a_ref, w_ref, res_ref, gate_ref, o_ref, acc_ref):
    k = pl.program_id(3)

    @pl.when(k == 0)
    def _():
        acc_ref[...] = jnp.zeros_like(acc_ref)

    acc_ref[...] += jnp.dot(a_ref[...], w_ref[...], preferred_element_type=F32)

    @pl.when(k == pl.num_programs(3) - 1)
    def _():
        o_ref[...] = res_ref[...] + gate_ref[...] * acc_ref[...]


def mlp_out(a, w, res, gate, tm=1024, tn=1024, tk=2048):
    b, l, kk = a.shape
    n = w.shape[1]
    tm, tn, tk = _tile(l, tm), _tile(n, tn), _tile(kk, tk)
    per_b = gate.shape[0] == b
    gmap = (lambda bi, i, j, k: (bi, 0, j)) if per_b else (lambda bi, i, j, k: (0, 0, j))
    return _pc(_mmresk_kernel, jax.ShapeDtypeStruct(res.shape, F32), (b, l // tm, n // tn, kk // tk),
               [pl.BlockSpec((None, tm, tk), lambda bi, i, j, k: (bi, i, k)),
                pl.BlockSpec((tk, tn), lambda bi, i, j, k: (k, j)),
                pl.BlockSpec((None, tm, tn), lambda bi, i, j, k: (bi, i, j)),
                pl.BlockSpec((None, 1, tn), gmap)],
               pl.BlockSpec((None, tm, tn), lambda bi, i, j, k: (bi, i, j)),
               scratch=[pltpu.VMEM((tm, tn), F32)], name="mlp_out")(a, w, res, gate)


def _mla_prep_kernel(kv_ref, cq_ref, cs_ref, gkv_ref, gqa_ref, gkn_ref, gkt_ref, gqn_ref, gqt_ref,
                     wkv_ref, wq_ref, k_ref, v_ref, q_ref):
    lane = lax.broadcasted_iota(jnp.int32, (1, LANES), 1)
    low = (lane < MLA_ROPE).astype(F32)
    cs = cs_ref[...]
    kv = kv_ref[...].astype(F32)
    ckv = kv[:, :KV_LORA]
    ckvn = ckv * lax.rsqrt(jnp.mean(ckv * ckv, axis=-1, keepdims=True) + EPS) * gkv_ref[...]
    kvp = jnp.dot(ckvn.astype(BF16), wkv_ref[...], preferred_element_type=F32)
    pe2 = kv[:, KV_LORA:KV_LORA + LANES]
    ss_pe = jnp.sum(pe2 * pe2 * low, axis=-1, keepdims=True)
    pe_tab = cs * gkt_ref[...]
    for h in range(MLA_HEADS):
        knope = kvp[:, 256 * h:256 * h + 128]
        rs = lax.rsqrt((jnp.sum(knope * knope, axis=-1, keepdims=True) + ss_pe) * (1.0 / MLA_QK) + EPS)
        t = pe2 * pe_tab * rs
        t = (t + pltpu.roll(t, MLA_ROPE, axis=1)) * low
        k_ref[h, :, 0:128] = (knope * gkn_ref[...] * rs).astype(BF16)
        k_ref[h, :, 128:256] = t.astype(BF16)
        v_ref[h] = kvp[:, 256 * h + 128:256 * h + 256].astype(BF16)
    if q_ref is not None:
        cq = cq_ref[...].astype(F32)
        cqn = cq * lax.rsqrt(jnp.mean(cq * cq, axis=-1, keepdims=True) + EPS) * gqa_ref[...]
        qp = jnp.dot(cqn.astype(BF16), wq_ref[...], preferred_element_type=F32)
        q_tab = cs * gqt_ref[...]
        for h in range(MLA_HEADS):
            qn = qp[:, 256 * h:256 * h + 128]
            qe = qp[:, 256 * h + 128:256 * h + 256]
            ss = jnp.sum(qn * qn + qe * qe * low, axis=-1, keepdims=True)
            rs = lax.rsqrt(ss * (1.0 / MLA_QK) + EPS) * (MLA_SCALE * LOG2E)
            t = qe * q_tab * rs
            t = t + pltpu.roll(t, MLA_ROPE, axis=1)
            q_ref[h, :, 0:128] = (qn * gqn_ref[...] * rs).astype(BF16)
            q_ref[h, :, 128:256] = t.astype(BF16)


def _mla_prep_kernel_noq(kv_ref, cs_ref, gkv_ref, gkn_ref, gkt_ref, wkv_ref, k_ref, v_ref):
    _mla_prep_kernel(kv_ref, None, cs_ref, gkv_ref, None, gkn_ref, gkt_ref, None, None, wkv_ref, None,
                     k_ref, v_ref, None)


def mla_prep(proj, cs, p, with_q, tm=512):
    b, l, _ = proj.shape
    tm = _tile(l, tm)
    row = lambda w: pl.BlockSpec((1, w), lambda bi, i: (0, 0))
    kv_spec = pl.BlockSpec((None, tm, GROUP_W), lambda bi, i: (bi, i, PB_KV))
    cq_spec = pl.BlockSpec((None, tm, GROUP_W), lambda bi, i: (bi, i, PB_CQ))
    cs_spec = pl.BlockSpec((tm, LANES), lambda bi, i: (i, 0))
    wkv_spec = pl.BlockSpec((KV_LORA, 1024), lambda bi, i: (0, 0))
    wq_spec = pl.BlockSpec((GROUP_W, 1024), lambda bi, i: (0, 0))
    hd = lambda w: pl.BlockSpec((None, MLA_HEADS, tm, w), lambda bi, i: (bi, 0, i, 0))
    k_sh = jax.ShapeDtypeStruct((b, MLA_HEADS, l, 256), BF16)
    v_sh = jax.ShapeDtypeStruct((b, MLA_HEADS, l, 128), BF16)
    if with_q:
        return _pc(_mla_prep_kernel, (k_sh, v_sh, k_sh), (b, l // tm),
                   [kv_spec, cq_spec, cs_spec, row(KV_LORA), row(GROUP_W), row(128), row(128), row(128), row(128),
                    wkv_spec, wq_spec],
                   (hd(256), hd(128), hd(256)), name="mla_prep")(
            proj, proj, cs, p["gkv"], p["gqa"], p["gkn"], p["gkt"], p["gqn"], p["gqt"], p["wkv"], p["wq"])
    k, v = _pc(_mla_prep_kernel_noq, (k_sh, v_sh), (b, l // tm),
               [kv_spec, cs_spec, row(KV_LORA), row(128), row(128), wkv_spec],
               (hd(256), hd(128)), name="mla_prep_kv")(proj, cs, p["gkv"], p["gkn"], p["gkt"], p["wkv"])
    return k, v, None


def _attn_kernel(*refs, n_seg, n_part, kv_chunk):
    q_ref = refs[0]
    o_ref = refs[1 + 2 * n_seg]
    rows = q_ref.shape[0] // n_part
    for part in range(n_part):
        sl = slice(part * rows, (part + 1) * rows)
        q = q_ref[sl, :]
        m = jnp.full((rows, 1), -jnp.inf, F32)
        den = jnp.zeros((rows, 1), F32)
        acc = jnp.zeros((rows, o_ref.shape[1]), F32)
        for i in range(n_seg):
            k_ref, v_ref = refs[1 + 2 * i], refs[2 + 2 * i]
            ch = min(kv_chunk, k_ref.shape[0])
            for c0 in range(0, k_ref.shape[0], ch):
                s = lax.dot_general(q, k_ref[c0:c0 + ch, :], (((1,), (1,)), ((), ())), preferred_element_type=F32)
                m_new = jnp.maximum(m, s.max(axis=-1, keepdims=True))
                alpha = jnp.exp2(m - m_new)
                p = jnp.exp2(s - m_new)
                den = alpha * den + p.sum(axis=-1, keepdims=True)
                acc = alpha * acc + jnp.dot(p.astype(BF16), v_ref[c0:c0 + ch, :], preferred_element_type=F32)
                m = m_new
        o_ref[sl, :] = (acc / den).astype(o_ref.dtype)


def attention(q, kvs, tq=1024, part_rows=512, kv_chunk=4096):
    b, h, l, _ = q.shape
    tq = _tile(l, tq)
    n_part = max(1, tq // part_rows)
    assert all(k.shape[2] % min(kv_chunk, k.shape[2]) == 0 for k, _ in kvs)
    specs = [pl.BlockSpec((None, None, tq, 256), lambda bi, hi, i: (bi, hi, i, 0))]
    args = [q]
    for k, v in kvs:
        lk = k.shape[2]
        specs += [pl.BlockSpec((None, None, lk, 256), lambda bi, hi, i: (bi, hi, 0, 0)),
                  pl.BlockSpec((None, None, lk, 128), lambda bi, hi, i: (bi, hi, 0, 0))]
        args += [k, v]
    return _pc(functools.partial(_attn_kernel, n_seg=len(kvs), n_part=n_part, kv_chunk=kv_chunk),
               jax.ShapeDtypeStruct((b, l, h * 128), BF16), (b, h, l // tq), specs,
               pl.BlockSpec((None, tq, 128), lambda bi, hi, i: (bi, i, hi)), name="attention")(*args)


def s5_tables(lam_re, lam_im, log_step, b_re, b_im, c_re, c_im):
    t = S5_T
    step = jnp.exp(log_step.astype(F32))[..., None]
    re, im = lam_re.astype(F32) * step, lam_im.astype(F32) * step
    d = jnp.arange(t + 1, dtype=F32)[:, None, None, None]
    mag = jnp.exp(d * re)
    ar, ai = mag * jnp.cos(d * im), mag * jnp.sin(d * im)
    lr, li = lam_re.astype(F32), lam_im.astype(F32)
    den = lr * lr + li * li
    er, ei = ar[1] - 1.0, ai[1]
    cr, ci = (er * lr + ei * li) / den, (ei * lr - er * li) / den
    br, bi = b_re.astype(F32), b_im.astype(F32)
    bbr = cr[..., None] * br - ci[..., None] * bi
    bbi = cr[..., None] * bi + ci[..., None] * br
    ccr, cci = c_re.astype(F32), c_im.astype(F32)
    car = ccr[None] * ar[:, :, :, None, :] - cci[None] * ai[:, :, :, None, :]
    cai = ccr[None] * ai[:, :, :, None, :] + cci[None] * ar[:, :, :, None, :]
    kd = jnp.einsum("dxgnp,xgpm->xgdnm", car[:t], bbr) - jnp.einsum("dxgnp,xgpm->xgdnm", cai[:t], bbi)
    kcat = jnp.stack([kd[0], kd[1][:, ::-1]], axis=1).transpose(0, 1, 4, 2, 3).reshape(S5_G, 2, S5_N, t * S5_N)
    pw = (jnp.arange(t - 1, -1, -1), jnp.arange(t))
    w_re, w_im = [], []
    for x in range(2):
        a_r, a_i = ar[pw[x], x], ai[pw[x], x]
        w_re.append(a_r[..., None] * bbr[x][None] - a_i[..., None] * bbi[x][None])
        w_im.append(a_r[..., None] * bbi[x][None] + a_i[..., None] * bbr[x][None])
    ws = jnp.concatenate(w_re + w_im, axis=2).transpose(1, 0, 3, 2).reshape(S5_G, t * S5_N, 4 * S5_P)
    pv = (jnp.arange(1, t + 1), jnp.arange(t, 0, -1))
    vm = jnp.concatenate([car[pv[0], 0], car[pv[1], 1], -cai[pv[0], 0], -cai[pv[1], 1]], axis=-1)
    vm = vm.transpose(1, 3, 0, 2).reshape(S5_G, 4 * S5_P, t * S5_N)
    dec = jnp.stack([jnp.concatenate([ar[t, 0], ar[t, 1]], axis=-1),
                     jnp.concatenate([ai[t, 0], ai[t, 1]], axis=-1)], axis=1)
    return kcat, ws.astype(BF16), vm.astype(BF16), dec


def _s5a_kernel(u_ref, k_ref, ws_ref, y_ref, s_ref, m_ref):
    kf, kb = k_ref[0], k_ref[1]
    lane = lax.broadcasted_iota(jnp.int32, kf.shape, 1)
    for s in range(S5_T):
        lo, hi = S5_N * s, S5_N * (s + 1)
        f = kf if s == 0 else jnp.where(lane >= lo, pltpu.roll(kf, lo, axis=1), 0.0)
        bk = kb if s == S5_T - 1 else jnp.where(lane < hi, pltpu.roll(kb, hi, axis=1), 0.0)
        m_ref[lo:hi, :] = (f + bk).astype(BF16)
    u = u_ref[...]
    y_ref[...] = jnp.dot(u, m_ref[...], preferred_element_type=F32)
    s_ref[...] = jnp.dot(u, ws_ref[...], preferred_element_type=F32)


def _s5b_kernel(yi_ref, s_ref, x0_ref, dec_ref, vm_ref, y_ref, xf_ref, xa_ref, xb_ref, *, nchunk, nbatch):
    cps = 8 // nbatch
    nslab = nchunk // cps
    ar = dec_ref[0:1, :]
    ai = dec_ref[1:2, :]
    fwd_lane = lax.broadcasted_iota(jnp.int32, (nbatch, 128), 1) < S5_P

    def step(jj, carry):
        xr, xi = carry
        rf = pl.ds(pl.multiple_of(jj * 8, 8), 8)
        rb = pl.ds(pl.multiple_of((nslab - 1 - jj) * 8, 8), 8)
        sf, sb = s_ref[rf, :], s_ref[rb, :]
        seen = []
        for i in range(cps):
            lo, hi = i * nbatch, (i + 1) * nbatch
            ml, mh = (cps - 1 - i) * nbatch, (cps - i) * nbatch
            seen.append(jnp.concatenate([xr, xi], axis=1))
            sr = jnp.where(fwd_lane, sf[lo:hi, :128], sb[ml:mh, :128])
            si = jnp.where(fwd_lane, sf[lo:hi, 128:], sb[ml:mh, 128:])
            xr, xi = ar * xr - ai * xi + sr, ar * xi + ai * xr + si
        xa_ref[rf, :] = jnp.concatenate(seen, axis=0)
        xb_ref[rb, :] = jnp.concatenate(seen[::-1], axis=0)
        return xr, xi

    x0 = x0_ref[0:nbatch, :]
    xr, xi = lax.fori_loop(0, nslab, step, (x0[:, :128], x0[:, 128:]), unroll=2)
    xf_ref[...] = jnp.zeros_like(xf_ref)
    xf_ref[0:nbatch, :] = jnp.concatenate([xr, xi], axis=1)
    pick = (lax.broadcasted_iota(jnp.int32, xa_ref.shape, 1) % 128) < S5_P
    xin = jnp.where(pick, xa_ref[...], xb_ref[...])
    y = yi_ref[...] + jnp.dot(xin.astype(BF16), vm_ref[...], preferred_element_type=F32)
    y_ref[...] = y.astype(y_ref.dtype)


def s5_scan(u, x0, tabs):
    kcat, ws, vm, dec = tabs
    b, l, _ = u.shape
    j = l // S5_T
    r = b * j
    assert 8 % b == 0 and j % (8 // b) == 0
    ug = u.astype(BF16).reshape(b, j, S5_T, S5_G, S5_N).transpose(3, 1, 0, 2, 4).reshape(S5_G, r, GROUP_W)
    g3 = lambda w: pl.BlockSpec((None, r, w), lambda g: (g, 0, 0))
    yi, s = _pc(_s5a_kernel,
                (jax.ShapeDtypeStruct((S5_G, r, GROUP_W), F32), jax.ShapeDtypeStruct((S5_G, r, 256), F32)),
                (S5_G,),
                [g3(GROUP_W), pl.BlockSpec((None, 2, S5_N, GROUP_W), lambda g: (g, 0, 0, 0)),
                 pl.BlockSpec((None, GROUP_W, 256), lambda g: (g, 0, 0))],
                (g3(GROUP_W), g3(256)),
                scratch=[pltpu.VMEM((GROUP_W, GROUP_W), BF16)], name="s5a")(ug, kcat, ws)
    y, xf = _pc(functools.partial(_s5b_kernel, nchunk=j, nbatch=b),
                (jax.ShapeDtypeStruct((S5_G, r, GROUP_W), BF16), jax.ShapeDtypeStruct((S5_G, 8, 256), F32)),
                (S5_G,),
                [g3(GROUP_W), g3(256),
                 pl.BlockSpec((None, 8, 256), lambda g: (g, 0, 0)),
                 pl.BlockSpec((None, 2, 128), lambda g: (g, 0, 0)),
                 pl.BlockSpec((None, 256, GROUP_W), lambda g: (g, 0, 0))],
                (g3(GROUP_W), pl.BlockSpec((None, 8, 256), lambda g: (g, 0, 0))),
                scratch=[pltpu.VMEM((r, 256), F32), pltpu.VMEM((r, 256), F32)], name="s5b")(yi, s, x0, dec, vm)
    y = y.reshape(S5_G, j, b, S5_T, S5_N).transpose(2, 1, 3, 0, 4).reshape(b, l, GROUP_W)
    return y, xf


def _s5_out_kernel(y_ref, u_ref, d_ref, w_ref, b_ref, o_ref):
    y = jax.nn.gelu(y_ref[...].astype(F32) + d_ref[...] * u_ref[...].astype(F32))
    z = jnp.dot(y.astype(BF16), w_ref[...], preferred_element_type=F32) + b_ref[...]
    o_ref[...] = (y * jax.nn.sigmoid(z)).astype(o_ref.dtype)


def s5_out(y, proj, d, w, bias, tm=512):
    b, l, _ = y.shape
    tm = _tile(l, tm)
    row = pl.BlockSpec((1, GROUP_W), lambda bi, i: (0, 0))
    return _pc(_s5_out_kernel, jax.ShapeDtypeStruct((b, l, GROUP_W), BF16), (b, l // tm),
               [pl.BlockSpec((None, tm, GROUP_W), lambda bi, i: (bi, i, 0)),
                pl.BlockSpec((None, tm, GROUP_W), lambda bi, i: (bi, i, PB_S5U)),
                row, pl.BlockSpec((GROUP_W, GROUP_W), lambda bi, i: (0, 0)), row],
               pl.BlockSpec((None, tm, GROUP_W), lambda bi, i: (bi, i, 0)), name="s5_out")(y, proj, d, w, bias)


def _hgrn_kernel(z_ref, i_ref, q_ref, la_ref, lb_ref, s0_ref, o_ref, sfin_ref, st_ref, *, reverse, nsub):
    c = HG_SUB

    @pl.when(pl.program_id(1) == 0)
    def _():
        st_ref[...] = s0_ref[...]

    hlf = c // 2
    ones = jnp.ones((HG_D, HG_D), BF16)
    row = lax.broadcasted_iota(jnp.int32, (c, HG_D), 0)
    row8 = lax.broadcasted_iota(jnp.int32, (hlf, HG_D), 0)

    def sub(jj, carry):
        j = (nsub - 1 - jj) if reverse else jj
        rows = pl.ds(pl.multiple_of(j * c, c), c)
        for h in range(HG_HEADS):
            sl = slice(HG_D * h, HG_D * (h + 1))
            z = z_ref[rows, sl].astype(F32)
            v = i_ref[rows, sl].astype(F32)
            q = _silu(q_ref[rows, sl].astype(F32))
            a = la_ref[:, sl]
            bb = lb_ref[:, sl] + (jnp.minimum(z, 0.0) - jnp.log1p(jnp.exp(-jnp.abs(z))))
            g = jnp.maximum(a, bb) + jnp.log1p(jnp.exp(-jnp.abs(a - bb)))
            k = 1.0 - jnp.exp(g)
            gc = g
            for sh in (1, 2, 4, 8):
                if reverse:
                    gc = gc + jnp.where(row < c - sh, pltpu.roll(gc, c - sh, axis=0), 0.0)
                else:
                    gc = gc + jnp.where(row >= sh, pltpu.roll(gc, sh, axis=0), 0.0)
            gc = gc * LOG2E
            gk = gc - jnp.log(jnp.maximum(k, 0.0)) * LOG2E
            blocks, meta = [], []
            for s in range(c):
                for half in range(2):
                    r0 = half * hlf
                    if (r0 > s) if reverse else (r0 + hlf - 1 < s):
                        continue
                    e = gc[r0:r0 + hlf, :] - gk[s:s + 1, :]
                    if not ((r0 + hlf - 1 <= s) if reverse else (r0 >= s)):
                        keep = (row8 + r0 <= s) if reverse else (row8 + r0 >= s)
                        e = jnp.where(keep, e, -1e30)
                    blocks.append(q[r0:r0 + hlf, :] * jnp.exp2(e))
                    meta.append((half, s))
            rsum = jnp.dot(jnp.concatenate(blocks, axis=0).astype(BF16), ones, preferred_element_type=F32)
            halves = [jnp.zeros((hlf, HG_D), F32), jnp.zeros((hlf, HG_D), F32)]
            for i, (half, s) in enumerate(meta):
                halves[half] = halves[half] + rsum[hlf * i:hlf * (i + 1), :] * v[s:s + 1, :]
            o = jnp.concatenate(halves, axis=0)
            st = st_ref[h]
            o = o + lax.dot_general((q * jnp.exp2(gc)).astype(BF16), st.astype(BF16),
                                    (((1,), (1,)), ((), ())), preferred_element_type=F32)
            gl = gc[0:1, :] if reverse else gc[c - 1:c, :]
            kd = k * jnp.exp2(gl - gc)
            ds = lax.dot_general(v.astype(BF16), kd.astype(BF16), (((0,), (0,)), ((), ())),
                                 preferred_element_type=F32)
            st_ref[h] = st * jnp.exp2(gl) + ds
            o_ref[rows, sl] = o.astype(o_ref.dtype)
        return carry

    lax.fori_loop(0, nsub, sub, 0, unroll=4)

    @pl.when(pl.program_id(1) == pl.num_programs(1) - 1)
    def _():
        sfin_ref[...] = st_ref[...]


def hgrn_dir(proj, la, lb1, s0, reverse, tl=256):
    b, l, _ = proj.shape
    tl = _tile(l, tl)
    nb = l // tl
    blk = (lambda i: nb - 1 - i) if reverse else (lambda i: i)
    col = lambda c: pl.BlockSpec((None, tl, GROUP_W), lambda bi, i: (bi, blk(i), c))
    row = pl.BlockSpec((1, GROUP_W), lambda bi, i: (0, 0))
    st = pl.BlockSpec((None, HG_HEADS, HG_D, HG_D), lambda bi, i: (bi, 0, 0, 0))
    return _pc(functools.partial(_hgrn_kernel, reverse=reverse, nsub=tl // HG_SUB),
               (jax.ShapeDtypeStruct((b, l, GROUP_W), BF16), jax.ShapeDtypeStruct(s0.shape, F32)),
               (b, nb),
               [col(PB_HGB if reverse else PB_HGF), col(PB_HGI), col(PB_HGQ), row, row, st],
               (pl.BlockSpec((None, tl, GROUP_W), lambda bi, i: (bi, blk(i), 0)), st),
               scratch=[pltpu.VMEM((HG_HEADS, HG_D, HG_D), F32)],
               name="hgrn_bwd" if reverse else "hgrn_fwd")(proj, proj, proj, la, lb1, s0)


def _hg_out_kernel(of_ref, ob_ref, g_ref, gn_ref, y_ref):
    o = of_ref[...].astype(F32) + ob_ref[...].astype(F32)
    gate = _silu(g_ref[...].astype(F32))
    for h in range(HG_HEADS):
        sl = slice(HG_D * h, HG_D * (h + 1))
        oh = o[:, sl]
        y = oh * lax.rsqrt(jnp.mean(oh * oh, axis=-1, keepdims=True) + EPS) * gn_ref[...]
        y_ref[:, sl] = (y * gate[:, sl]).astype(y_ref.dtype)


def hg_out(of, ob, proj, gn, tm=512):
    b, l, _ = of.shape
    tm = _tile(l, tm)
    t3 = lambda c: pl.BlockSpec((None, tm, GROUP_W), lambda bi, i: (bi, i, c))
    return _pc(_hg_out_kernel, jax.ShapeDtypeStruct((b, l, GROUP_W), BF16), (b, l // tm),
               [t3(0), t3(0), t3(PB_HGG), pl.BlockSpec((1, HG_D), lambda bi, i: (0, 0))],
               t3(0), name="hg_out")(of, ob, proj, gn)


def _fft_factors(n_fft):
    n1 = {8192: 64, 512: 32, 1024: 32, 2048: 32, 4096: 64}[n_fft]
    return n1, n_fft // n1


def _dft_tables(na, nb, ka, mb, inverse):
    n = na * nb
    sgn = 1.0 if inverse else -1.0

    def stacked(phase, scale=1.0):
        ang = sgn * 2.0 * np.pi * (phase % n) / n
        fr, fi = np.cos(ang) * scale, np.sin(ang) * scale
        return np.concatenate([np.concatenate([fr, -fi], axis=-1), np.concatenate([fi, fr], axis=-1)], axis=-2)

    fa = stacked(np.outer(np.arange(na), np.arange(ka)) * nb)
    p, q, b = np.arange(na)[:, None, None], np.arange(mb)[None, :, None], np.arange(nb)[None, None, :]
    fb = stacked(b * p + b * q * na, (1.0 / n) if inverse else 1.0)
    return jnp.asarray(fa, BF16), jnp.asarray(fb, BF16)


def _fft_a_kernel(z_ref, f_ref, o_ref, *, g, na, is_complex):
    f = f_ref[...]
    for i in range(g):
        z = z_ref[:, i] if is_complex else z_ref[i]
        if is_complex:
            z = z.reshape(z.shape[0] * z.shape[1], z.shape[2])
        r = jnp.dot(f, z, preferred_element_type=F32)
        o_ref[0, i] = r[:na]
        o_ref[1, i] = r[na:]


def fft_stage_a(z, fa, g=8):
    is_complex = z.ndim == 4
    nb, ka, pc = z.shape[-3:]
    na = fa.shape[0] // 2
    g = min(g, nb)
    zspec = (pl.BlockSpec((2, g, ka, pc), lambda j: (0, j, 0, 0)) if is_complex
             else pl.BlockSpec((g, ka, pc), lambda j: (j, 0, 0)))
    return _pc(functools.partial(_fft_a_kernel, g=g, na=na, is_complex=is_complex),
               jax.ShapeDtypeStruct((2, nb, na, pc), F32), (nb // g,),
               [zspec, pl.BlockSpec(fa.shape, lambda j: (0, 0))],
               pl.BlockSpec((2, g, na, pc), lambda j: (0, j, 0, 0)), name="fft_a")(z, fa)


def _fft_b_kernel(*refs, mode, mb, kb):
    a_ref, f_ref = refs[0], refs[1]
    o_ref = refs[-1]
    nb = a_ref.shape[1]
    for i in range(kb):
        f = f_ref[i].astype(F32)
        z = (jnp.dot(f[:, :nb], a_ref[0, :, i, :], preferred_element_type=F32)
             + jnp.dot(f[:, nb:], a_ref[1, :, i, :], preferred_element_type=F32))
        zr, zi = z[:mb], z[mb:]
        if mode == "filter":
            h_ref = refs[2]
            hr, hi = h_ref[0, i].astype(F32), h_ref[1, i].astype(F32)
            zr, zi = zr * hr - zi * hi, zr * hi + zi * hr
        elif mode == "final":
            v_ref, g_ref, b_ref = refs[2], refs[3], refs[4]
            zr = g_ref[0, i].astype(F32) * (zr + v_ref[0, i].astype(F32) * b_ref[...])
            zi = g_ref[1, i].astype(F32) * (zi + v_ref[1, i].astype(F32) * b_ref[...])
        o_ref[0, i] = zr.astype(o_ref.dtype)
        o_ref[1, i] = zi.astype(o_ref.dtype)


def fft_stage_b(a, fb, mode, extra=(), order=0, kb=8):
    _, nb, na, pc = a.shape
    mb = fb.shape[1] // 2
    c = GROUP_W
    kb = min(kb, na)
    blk = pl.BlockSpec((2, kb, mb, c), lambda k, p: (0, k, 0, p))
    specs = [pl.BlockSpec((2, nb, kb, c), lambda k, p: (0, 0, k, p)),
             pl.BlockSpec((kb, 2 * mb, 2 * nb), lambda k, p: (k, 0, 0))]
    args = [a, fb]
    if mode == "filter":
        (h,) = extra
        specs.append(pl.BlockSpec((2, kb, mb, c), lambda k, p: (0, k, 0, order)))
        args.append(h)
    elif mode == "final":
        v, gate, bias = extra
        specs += [blk, blk, pl.BlockSpec((1, c), lambda k, p: (0, 0))]
        args += [v, gate, bias]
    return _pc(functools.partial(_fft_b_kernel, mode=mode, mb=mb, kb=kb),
               jax.ShapeDtypeStruct((2, na, mb, pc), BF16), (na // kb, pc // c), specs, blk,
               name="fft_b_" + mode)(*args)


def _hy_filter_kernel(emb_ref, w1_ref, b1_ref, w2_ref, b2_ref, w3_ref, win_ref, o_ref, h_ref, *, na):
    hp = lax.Precision.HIGHEST

    @pl.when(pl.program_id(0) == 0)
    def _():
        rc = min(512, h_ref.shape[0])

        def chunk(i, carry):
            rows = pl.ds(pl.multiple_of(i * rc, rc), rc)
            h = jnp.sin(jnp.dot(emb_ref[rows, :], w1_ref[...], precision=hp, preferred_element_type=F32)
                        + b1_ref[...])
            h = jnp.sin(jnp.dot(h, w2_ref[...], precision=hp, preferred_element_type=F32) + b2_ref[...])
            row = lax.broadcasted_iota(jnp.int32, h.shape, 0) + i * rc
            causal = (row & (na - 1)) < na // 2
            h_ref[rows, :] = jnp.concatenate([jnp.where(causal, h, 0.0), jnp.where(causal, 0.0, h)],
                                             axis=1).astype(BF16)
            return carry

        lax.fori_loop(0, h_ref.shape[0] // rc, chunk, 0)

    taps = jnp.dot(h_ref[...], w3_ref[...], preferred_element_type=F32) * win_ref[...]
    o_ref[...] = (taps / jnp.sum(jnp.abs(taps), axis=0, keepdims=True)).astype(o_ref.dtype)


def hyena_filters(n, w1, b1, w2, b2, w3):
    na, nb = _fft_factors(2 * n)
    t = np.arange(n, dtype=np.float32)
    t_norm = t / max(n - 1, 1)
    bands = np.linspace(1e-4, HY_BANDS - 1, HY_BANDS, dtype=np.float32)
    ang = (2.0 * math.pi * t / n)[:, None] * bands[None, :]
    emb = np.concatenate([t_norm[:, None], np.cos(ang), -np.sin(ang)], axis=-1).astype(np.float32)
    emb = np.pad(emb, ((0, 0), (0, LANES - HY_EMB)))
    deltas = np.linspace(math.log(HY_TARGET) / HY_SLOW, math.log(HY_TARGET) / HY_FAST, GROUP_W, dtype=np.float32)
    win = (np.exp(-t_norm[:, None] * np.abs(deltas)[None, :]) + np.float32(HY_SHIFT)).astype(np.float32)
    back = np.concatenate([[0], np.arange(n - 2, -1, -1)])
    win_b = win[back]
    win_b[0] = 0.0
    order = (np.arange(na)[None, :] * nb + np.arange(nb)[:, None]).reshape(-1)
    emb2 = jnp.asarray(np.concatenate([emb, emb[back]], axis=0)[order])
    win2 = jnp.asarray(np.concatenate([win, win_b], axis=0)[order])
    w1p = jnp.pad(w1.astype(F32), ((0, LANES - HY_EMB), (0, 0)))
    w3r = w3.astype(F32).reshape(HY_HIDDEN, 2, 2, GROUP_W)
    w3c = jnp.concatenate([w3r[:, :, 0].reshape(HY_HIDDEN, 2 * GROUP_W),
                           w3r[:, :, 1].reshape(HY_HIDDEN, 2 * GROUP_W)], axis=0).astype(BF16)
    tc = LANES
    full = lambda a: pl.BlockSpec(a.shape, lambda j: (0, 0))
    b1r, b2r = b1.reshape(1, -1).astype(F32), b2.reshape(1, -1).astype(F32)
    w2f = w2.astype(F32)
    return _pc(functools.partial(_hy_filter_kernel, na=na),
               jax.ShapeDtypeStruct((2 * n, 2 * GROUP_W), BF16), (2 * GROUP_W // tc,),
               [full(emb2), full(w1p), full(b1r), full(w2f), full(b2r),
                pl.BlockSpec((2 * HY_HIDDEN, tc), lambda j: (0, j)),
                pl.BlockSpec((2 * n, tc), lambda j: (0, j % (GROUP_W // tc)))],
               pl.BlockSpec((2 * n, tc), lambda j: (0, j)),
               scratch=[pltpu.VMEM((2 * n, 2 * HY_HIDDEN), BF16)], name="hy_filter")(
        emb2, w1p, b1r, w2f, b2r, w3c, win2)


def hyena_spectrum(n, w1, b1, w2, b2, w3):
    na, nb = _fft_factors(2 * n)
    circ = hyena_filters(n, w1, b1, w2, b2, w3).reshape(nb, na, 2 * GROUP_W)
    fa, fb = _dft_tables(na, nb, na, nb, inverse=False)
    return fft_stage_b(fft_stage_a(circ, fa[:, :na]), fb, "plain")


def _shortconv_kernel(p_ref, w_ref, b_ref, o_ref):
    p = p_ref[...].astype(F32)
    n = p.shape[0]
    row = lax.broadcasted_iota(jnp.int32, p.shape, 0)
    prev = jnp.where(row >= 1, pltpu.roll(p, 1, axis=0), 0.0)
    nxt = jnp.where(row < n - 1, pltpu.roll(p, n - 1, axis=0), 0.0)
    o_ref[...] = (prev * w_ref[0:1, :] + p * w_ref[1:2, :] + nxt * w_ref[2:3, :] + b_ref[...]).astype(o_ref.dtype)


def hyena_shortconv(proj, w, bias):
    b, n, _ = proj.shape
    return _pc(_shortconv_kernel, jax.ShapeDtypeStruct((3, 2, n, (b // 2) * GROUP_W), BF16), (b, 3),
               [pl.BlockSpec((None, n, GROUP_W), lambda bi, j: (bi, 0, PB_HY + j)),
                pl.BlockSpec((3, GROUP_W), lambda bi, j: (0, j)),
                pl.BlockSpec((1, GROUP_W), lambda bi, j: (0, j))],
               pl.BlockSpec((None, None, n, GROUP_W), lambda bi, j: (j, bi % 2, 0, bi // 2)),
               name="shortconv")(proj, w, bias)


def hyena_mixer(proj, spec, conv_w, conv_b, bias):
    b, n, _ = proj.shape
    na, nb = _fft_factors(2 * n)
    pc = (b // 2) * GROUP_W
    u = hyena_shortconv(proj, conv_w, conv_b).reshape(3, 2, na // 2, nb, pc).swapaxes(2, 3)
    fa, fb = _dft_tables(na, nb, na // 2, nb, inverse=False)
    ga, gb = _dft_tables(nb, na, nb, na // 2, inverse=True)
    z = u[0]
    for o in range(2):
        zf = fft_stage_b(fft_stage_a(z, fa), fb, "filter", (spec,), order=o)
        z = fft_stage_b(fft_stage_a(zf, ga), gb, "final", (z, u[1 + o], bias[o:o + 1]))
    return z.swapaxes(1, 2).reshape(2, n, pc)


def _rot_cols(w):
    return jnp.concatenate([-w[..., 16:32], w[..., 0:16], -w[..., 48:64], w[..., 32:48]], axis=-1)


def _rot_perm(g):
    return jnp.concatenate([g[16:32], g[0:16], g[48:64], g[32:48]])


def _prep_layer(w_in, w_uq, w_ukv, q_a_g, kv_a_g, q_g, k_g):
    d = w_in.shape[0]
    ckv, krope, rest = w_in[:, :256], w_in[:, 256:320], w_in[:, 320:]
    w_in_p = jnp.concatenate([rest, ckv, krope, _rot_cols(krope), jnp.zeros((d, 128), w_in.dtype)], axis=1)
    wq = w_uq.reshape(GROUP_W, MLA_HEADS, MLA_QK)
    wq = jnp.concatenate([wq, _rot_cols(wq[..., MLA_NOPE:])], axis=-1).reshape(GROUP_W, MLA_HEADS * 256)
    r2 = lambda a: a.reshape(1, -1).astype(F32)
    mla = dict(wkv=w_ukv.astype(BF16), wq=wq.astype(BF16), gkv=r2(kv_a_g), gqa=r2(q_a_g),
               gkn=r2(k_g[:MLA_NOPE]), gkt=r2(jnp.concatenate([k_g[MLA_NOPE:], _rot_perm(k_g[MLA_NOPE:])])),
               gqn=r2(q_g[:MLA_NOPE]), gqt=r2(jnp.concatenate([q_g[MLA_NOPE:], _rot_perm(q_g[MLA_NOPE:])])))
    return w_in_p.astype(BF16), mla


def _rope_table(n_rows):
    row = np.repeat(np.arange(n_rows, dtype=np.float32), GRID_W)
    col = np.tile(np.arange(GRID_W, dtype=np.float32), n_rows)
    n_freq = MLA_ROPE // 4
    inv_freq = (ROPE_BASE ** (-np.arange(n_freq, dtype=np.float32) / n_freq)).astype(np.float32)
    ang_r, ang_c = row[:, None] * inv_freq, col[:, None] * inv_freq
    ang = np.concatenate([ang_r, ang_r, ang_c, ang_c], axis=-1)
    return jnp.asarray(np.concatenate([np.cos(ang), np.sin(ang)], axis=-1), F32)


def kernel(x, c, ctx, c_ctx, w_mod, b_mod, norm1, norm2, w_in, w_out, mla_q_a_norm, mla_kv_a_norm, mla_w_uq, mla_w_ukv, mla_q_norm, mla_k_norm, s5_lam_re, s5_lam_im, s5_log_step, s5_b_re, s5_b_im, s5_c_re, s5_c_im, s5_d, s5_w_glu, s5_b_glu, hg_lower_bounds, hg_o_norm, hy_conv_w, hy_conv_b, hy_w1, hy_b1, hy_w2, hy_b2, hy_w3, hy_bias, mlp_w1, mlp_w2):
    bsz, seq, dm = x.shape
    n_ctx = ctx.shape[1]
    depth = w_mod.shape[0]
    assert bsz % 2 == 0 and bsz <= 8 and seq % GRID_W == 0 and S5_T * S5_N == GROUP_W
    r2 = lambda a: a.reshape(1, -1).astype(F32)

    cs_l = _rope_table(seq // GRID_W)
    cs_c = jnp.concatenate([jnp.ones((n_ctx, MLA_ROPE), F32), jnp.zeros((n_ctx, MLA_ROPE), F32)], axis=-1)
    sm = jax.nn.softmax(hg_lower_bounds.astype(F32), axis=1)
    lower = jnp.clip(jnp.cumsum(sm, axis=1) - sm[:, :1], 0.0, 1.0)
    c8 = jnp.concatenate([c.astype(F32), c_ctx.astype(F32)[None], jnp.zeros((7 - bsz, dm), F32)], axis=0)
    zero_state = jnp.zeros((bsz, HG_HEADS, HG_D, HG_D), F32)
    zero_x = jnp.zeros((S5_G, 8, 256), F32)

    xc = ctx
    for l in range(depth):
        last = l == depth - 1
        mod = modvec(c8, w_mod[l].astype(F32), r2(b_mod[l]))
        mod_l = [mod[:bsz, i * dm:(i + 1) * dm][:, None, :] for i in range(N_MOD)]
        mod_c = [mod[bsz:bsz + 1, i * dm:(i + 1) * dm][:, None, :] for i in range(N_MOD)]
        w_in_p, mla_p = _prep_layer(w_in[l], mla_w_uq[l], mla_w_ukv[l], mla_q_a_norm[l], mla_kv_a_norm[l],
                                    mla_q_norm[l], mla_k_norm[l])
        pl_ = modproj(x, mod_l[0], mod_l[1], r2(norm1[l]), w_in_p, BF16, tm=1024, tn=N_PROJ // 4)
        pc_ = modproj(xc, mod_c[0], mod_c[1], r2(norm1[l]), w_in_p, BF16, tm=1024, tn=N_PROJ // 4)

        k_c, v_c, q_c = mla_prep(pc_, cs_c, mla_p, with_q=not last)
        k_l, v_l, q_l = mla_prep(pl_, cs_l, mla_p, with_q=True)
        y_a = attention(q_l, [(k_l, v_l), (k_c, v_c)])

        tabs = s5_tables(s5_lam_re[l], s5_lam_im[l], s5_log_step[l], s5_b_re[l], s5_b_im[l], s5_c_re[l], s5_c_im[l])
        yc_s5, fin = s5_scan(pc_[..., PB_S5U * GROUP_W:(PB_S5U + 1) * GROUP_W], zero_x, tabs)
        yl_s5, _ = s5_scan(pl_[..., PB_S5U * GROUP_W:(PB_S5U + 1) * GROUP_W], fin, tabs)
        glu_w, glu_b = s5_w_glu[l].astype(BF16), r2(s5_b_glu[l])
        y_b = s5_out(yl_s5, pl_, r2(s5_d[l]), glu_w, glu_b)

        la_f, lb_f = r2(jnp.log(lower[0, l])), r2(jnp.log1p(-lower[0, l]))
        la_b, lb_b = r2(jnp.log(lower[1, l])), r2(jnp.log1p(-lower[1, l]))
        oc_f, s_f = hgrn_dir(pc_, la_f, lb_f, zero_state, reverse=False)
        oc_b, s_b = hgrn_dir(pc_, la_b, lb_b, zero_state, reverse=True)
        ol_f, _ = hgrn_dir(pl_, la_f, lb_f, s_f, reverse=False)
        ol_b, _ = hgrn_dir(pl_, la_b, lb_b, s_b, reverse=True)
        y_c = hg_out(ol_f, ol_b, pl_, r2(hg_o_norm[l]))

        hy_args = (hy_conv_w[l].astype(F32), r2(hy_conv_b[l]), hy_bias[l].astype(F32))
        spec_l = hyena_spectrum(seq, hy_w1[l], hy_b1[l], hy_w2[l], hy_b2[l], hy_w3[l])
        y_d = hyena_mixer(pl_, spec_l, *hy_args)

        w_out_b = w_out[l].astype(BF16)
        w1_b, w2_b = mlp_w1[l].astype(BF16), mlp_w2[l].astype(BF16)
        x = mix_out([y_a, y_b, y_c], y_d, w_out_b, x, mod_l[2])
        h1 = modproj(x, mod_l[3], mod_l[4], r2(norm2[l]), w1_b, BF16, act="relu2", tm=1024, tn=2048)
        x = mlp_out(h1, w2_b, x, mod_l[5])

        if not last:
            y_ac = attention(q_c, [(k_c, v_c)])
            y_bc = s5_out(yc_s5, pc_, r2(s5_d[l]), glu_w, glu_b)
            y_cc = hg_out(oc_f, oc_b, pc_, r2(hg_o_norm[l]))
            spec_c = hyena_spectrum(n_ctx, hy_w1[l], hy_b1[l], hy_w2[l], hy_b2[l], hy_w3[l])
            y_dc = hyena_mixer(pc_, spec_c, *hy_args)
            xc = mix_out([y_ac, y_bc, y_cc], y_dc, w_out_b, xc, mod_c[2])
            h1c = modproj(xc, mod_c[3], mod_c[4], r2(norm2[l]), w1_b, BF16, act="relu2", tm=1024)
            xc = mlp_out(h1c, w2_b, xc, mod_c[5])
    return x
```

```python
import functools
import math

import numpy as np
import jax
import jax.numpy as jnp
from jax import lax
from jax.experimental import pallas as pl
from jax.experimental.pallas import tpu as pltpu

F32 = jnp.float32
BF16 = jnp.bfloat16
EPS = 1e-6

GRID_W = 64
GROUP_W = 512
N_MOD = 6
MLA_NOPE = 128
MLA_ROPE = 64
MLA_QK = MLA_NOPE + MLA_ROPE
MLA_HEADS = 4
KV_LORA = 256
MLA_SCALE = 1.0 / math.sqrt(MLA_QK)
LOG2E = 1.0 / math.log(2.0)
ROPE_BASE = 10000.0
S5_G = 32
S5_N = 16
S5_P = 64
S5_T = 32
HG_HEADS = 4
HG_D = 128
HG_SUB = 16
HY_EMB = 33
HY_BANDS = 16
HY_HIDDEN = 64
HY_TARGET = 1e-2
HY_FAST = 0.3
HY_SLOW = 1.5
HY_SHIFT = 0.05

LANES = 128
VMEM_LIMIT_MB = 56

PB_S5U, PB_HGF, PB_HGB, PB_HGI, PB_CQ, PB_HGQ, PB_HGG, PB_HY, PB_KV = 0, 1, 2, 3, 4, 5, 6, 7, 10
N_PROJ = 11 * GROUP_W


def _pc(kernel, out_shape, grid, in_specs, out_specs, scratch=(), vmem_mb=VMEM_LIMIT_MB, name=None):
    return pl.pallas_call(
        kernel, out_shape=out_shape, grid=grid, in_specs=in_specs, out_specs=out_specs,
        scratch_shapes=list(scratch), name=name,
        compiler_params=pltpu.CompilerParams(
            dimension_semantics=("arbitrary",) * len(grid), vmem_limit_bytes=vmem_mb << 20))


def _tile(n, pref):
    t = min(n, pref)
    while n % t:
        t //= 2
    return t


def _silu(x):
    return x * jax.nn.sigmoid(x)


def _modvec_kernel(c_ref, w_ref, b_ref, o_ref):
    s = _silu(c_ref[...])
    o_ref[...] = jnp.dot(s.astype(BF16), w_ref[...].astype(BF16), preferred_element_type=F32) + b_ref[...]


def modvec(c8, w, b):
    d, n = w.shape
    tn = _tile(n, 1536)
    return _pc(_modvec_kernel, jax.ShapeDtypeStruct((8, n), F32), (n // tn,),
               [pl.BlockSpec((8, d), lambda j: (0, 0)),
                pl.BlockSpec((d, tn), lambda j: (0, j)),
                pl.BlockSpec((1, tn), lambda j: (0, j))],
               pl.BlockSpec((8, tn), lambda j: (0, j)), name="modvec")(c8, w, b)


def _modproj_kernel(x_ref, sh_ref, sc_ref, g_ref, w_ref, o_ref, h_ref, *, act):
    @pl.when(pl.program_id(2) == 0)
    def _():
        x = x_ref[...]
        ms = jnp.mean(x * x, axis=-1, keepdims=True)
        y = x * lax.rsqrt(ms + EPS) * g_ref[...]
        h_ref[...] = (y * (1.0 + sc_ref[...]) + sh_ref[...]).astype(BF16)

    acc = jnp.dot(h_ref[...], w_ref[...], preferred_element_type=F32)
    if act == "relu2":
        acc = jnp.square(jnp.maximum(acc, 0.0))
    o_ref[...] = acc.astype(o_ref.dtype)


def modproj(x, shift, scale, g, w, out_dtype, act=None, tm=512, tn=512):
    b, l, d = x.shape
    n = w.shape[1]
    tm, tn = _tile(l, tm), _tile(n, tn)
    per_b = shift.shape[0] == b
    mod_map = (lambda bi, i, j: (bi, 0, 0)) if per_b else (lambda bi, i, j: (0, 0, 0))
    return _pc(functools.partial(_modproj_kernel, act=act),
               jax.ShapeDtypeStruct((b, l, n), out_dtype), (b, l // tm, n // tn),
               [pl.BlockSpec((None, tm, d), lambda bi, i, j: (bi, i, 0)),
                pl.BlockSpec((None, 1, d), mod_map),
                pl.BlockSpec((None, 1, d), mod_map),
                pl.BlockSpec((1, d), lambda bi, i, j: (0, 0)),
                pl.BlockSpec((d, tn), lambda bi, i, j: (0, j))],
               pl.BlockSpec((None, tm, tn), lambda bi, i, j: (bi, i, j)),
               scratch=[pltpu.VMEM((tm, d), BF16)], name="modproj_" + (act or "lin"))(x, shift, scale, g, w)


def _mmres_kernel(*refs, n_lhs, kw):
    a_refs = refs[:n_lhs]
    w_ref, res_ref, gate_ref, o_ref = refs[n_lhs:]
    acc = None
    for i, a in enumerate(a_refs):
        d = jnp.dot(a[...], w_ref[i * kw:(i + 1) * kw, :], preferred_element_type=F32)
        acc = d if acc is None else acc + d
    o_ref[...] = res_ref[...] + gate_ref[...] * acc


def mix_out(ys, yd_hlay, w, res, gate, tm=1024, tn=1024):
    b, l, kw = ys[0].shape
    n = w.shape[1]
    tm, tn = _tile(l, tm), _tile(n, tn)
    per_b = gate.shape[0] == b
    gmap = (lambda bi, i, j: (bi, 0, j)) if per_b else (lambda bi, i, j: (0, 0, j))
    y_spec = pl.BlockSpec((None, tm, kw), lambda bi, i, j: (bi, i, 0))
    yd_spec = pl.BlockSpec((None, tm, kw), lambda bi, i, j: (bi % 2, i, bi // 2))
    return _pc(functools.partial(_mmres_kernel, n_lhs=4, kw=kw),
               jax.ShapeDtypeStruct(res.shape, F32), (b, l // tm, n // tn),
               [y_spec, y_spec, y_spec, yd_spec,
                pl.BlockSpec((4 * kw, tn), lambda bi, i, j: (0, j)),
                pl.BlockSpec((None, tm, tn), lambda bi, i, j: (bi, i, j)),
                pl.BlockSpec((None, 1, tn), gmap)],
               pl.BlockSpec((None, tm, tn), lambda bi, i, j: (bi, i, j)), name="mix_out")(*ys, yd_hlay, w, res, gate)


def _mmresk_kernel(a_ref, w_ref, res_ref, gate_ref, o_ref, acc_ref):
    k = pl.program_id(3)

    @pl.when(k == 0)
    def _():
        acc_ref[...] = jnp.zeros_like(acc_ref)

    acc_ref[...] += jnp.dot(a_ref[...], w_ref[...], preferred_element_type=F32)

    @pl.when(k == pl.num_programs(3) - 1)
    def _():
        o_ref[...] = res_ref[...] + gate_ref[...] * acc_ref[...]


def mlp_out(a, w, res, gate, tm=1024, tn=1024, tk=2048):
    b, l, kk = a.shape
    n = w.shape[1]
    tm, tn, tk = _tile(l, tm), _tile(n, tn), _tile(kk, tk)
    per_b = gate.shape[0] == b
    gmap = (lambda bi, i, j, k: (bi, 0, j)) if per_b else (lambda bi, i, j, k: (0, 0, j))
    return _pc(_mmresk_kernel, jax.ShapeDtypeStruct(res.shape, F32), (b, l // tm, n // tn, kk // tk),
               [pl.BlockSpec((None, tm, tk), lambda bi, i, j, k: (bi, i, k)),
                pl.BlockSpec((tk, tn), lambda bi, i, j, k: (k, j)),
                pl.BlockSpec((None, tm, tn), lambda bi, i, j, k: (bi, i, j)),
                pl.BlockSpec((None, 1, tn), gmap)],
               pl.BlockSpec((None, tm, tn), lambda bi, i, j, k: (bi, i, j)),
               scratch=[pltpu.VMEM((tm, tn), F32)], name="mlp_out")(a, w, res, gate)


def _mla_prep_kernel(kv_ref, cq_ref, cs_ref, gkv_ref, gqa_ref, gkn_ref, gkt_ref, gqn_ref, gqt_ref,
                     wkv_ref, wq_ref, k_ref, v_ref, q_ref):
    lane = lax.broadcasted_iota(jnp.int32, (1, LANES), 1)
    low = (lane < MLA_ROPE).astype(F32)
    cs = cs_ref[...]
    kv = kv_ref[...].astype(F32)
    ckv = kv[:, :KV_LORA]
    ckvn = ckv * lax.rsqrt(jnp.mean(ckv * ckv, axis=-1, keepdims=True) + EPS) * gkv_ref[...]
    kvp = jnp.dot(ckvn.astype(BF16), wkv_ref[...], preferred_element_type=F32)
    pe2 = kv[:, KV_LORA:KV_LORA + LANES]
    ss_pe = jnp.sum(pe2 * pe2 * low, axis=-1, keepdims=True)
    pe_tab = cs * gkt_ref[...]
    for h in range(MLA_HEADS):
        knope = kvp[:, 256 * h:256 * h + 128]
        rs = lax.rsqrt((jnp.sum(knope * knope, axis=-1, keepdims=True) + ss_pe) * (1.0 / MLA_QK) + EPS)
        t = pe2 * pe_tab * rs
        t = (t + pltpu.roll(t, MLA_ROPE, axis=1)) * low
        k_ref[h, :, 0:128] = (knope * gkn_ref[...] * rs).astype(BF16)
        k_ref[h, :, 128:256] = t.astype(BF16)
        v_ref[h] = kvp[:, 256 * h + 128:256 * h + 256].astype(BF16)
    if q_ref is not None:
        cq = cq_ref[...].astype(F32)
        cqn = cq * lax.rsqrt(jnp.mean(cq * cq, axis=-1, keepdims=True) + EPS) * gqa_ref[...]
        qp = jnp.dot(cqn.astype(BF16), wq_ref[...], preferred_element_type=F32)
        q_tab = cs * gqt_ref[...]
        for h in range(MLA_HEADS):
            qn = qp[:, 256 * h:256 * h + 128]
            qe = qp[:, 256 * h + 128:256 * h + 256]
            ss = jnp.sum(qn * qn + qe * qe * low, axis=-1, keepdims=True)
            rs = lax.rsqrt(ss * (1.0 / MLA_QK) + EPS) * (MLA_SCALE * LOG2E)
            t = qe * q_tab * rs
            t = t + pltpu.roll(t, MLA_ROPE, axis=1)
            q_ref[h, :, 0:128] = (qn * gqn_ref[...] * rs).astype(BF16)
            q_ref[h, :, 128:256] = t.astype(BF16)


def _mla_prep_kernel_noq(kv_ref, cs_ref, gkv_ref, gkn_ref, gkt_ref, wkv_ref, k_ref, v_ref):
    _mla_prep_kernel(kv_ref, None, cs_ref, gkv_ref, None, gkn_ref, gkt_ref, None, None, wkv_ref, None,
                     k_ref, v_ref, None)


def mla_prep(proj, cs, p, with_q, tm=1024):
    b, l, _ = proj.shape
    tm = _tile(l, tm)
    row = lambda w: pl.BlockSpec((1, w), lambda bi, i: (0, 0))
    kv_spec = pl.BlockSpec((None, tm, GROUP_W), lambda bi, i: (bi, i, PB_KV))
    cq_spec = pl.BlockSpec((None, tm, GROUP_W), lambda bi, i: (bi, i, PB_CQ))
    cs_spec = pl.BlockSpec((tm, LANES), lambda bi, i: (i, 0))
    wkv_spec = pl.BlockSpec((KV_LORA, 1024), lambda bi, i: (0, 0))
    wq_spec = pl.BlockSpec((GROUP_W, 1024), lambda bi, i: (0, 0))
    hd = lambda w: pl.BlockSpec((None, MLA_HEADS, tm, w), lambda bi, i: (bi, 0, i, 0))
    k_sh = jax.ShapeDtypeStruct((b, MLA_HEADS, l, 256), BF16)
    v_sh = jax.ShapeDtypeStruct((b, MLA_HEADS, l, 128), BF16)
    if with_q:
        return _pc(_mla_prep_kernel, (k_sh, v_sh, k_sh), (b, l // tm),
                   [kv_spec, cq_spec, cs_spec, row(KV_LORA), row(GROUP_W), row(128), row(128), row(128), row(128),
                    wkv_spec, wq_spec],
                   (hd(256), hd(128), hd(256)), name="mla_prep")(
            proj, proj, cs, p["gkv"], p["gqa"], p["gkn"], p["gkt"], p["gqn"], p["gqt"], p["wkv"], p["wq"])
    k, v = _pc(_mla_prep_kernel_noq, (k_sh, v_sh), (b, l // tm),
               [kv_spec, cs_spec, row(KV_LORA), row(128), row(128), wkv_spec],
               (hd(256), hd(128)), name="mla_prep_kv")(proj, cs, p["gkv"], p["gkn"], p["gkt"], p["wkv"])
    return k, v, None


def _attn_kernel(*refs, n_seg, n_part, kv_chunk):
    q_ref = refs[0]
    o_ref = refs[1 + 2 * n_seg]
    rows = q_ref.shape[0] // n_part
    for part in range(n_part):
        sl = slice(part * rows, (part + 1) * rows)
        q = q_ref[sl, :]
        m = jnp.full((rows, 1), -jnp.inf, F32)
        den = jnp.zeros((rows, 1), F32)
        acc = jnp.zeros((rows, o_ref.shape[1]), F32)
        for i in range(n_seg):
            k_ref, v_ref = refs[1 + 2 * i], refs[2 + 2 * i]
            ch = min(kv_chunk, k_ref.shape[0])
            for c0 in range(0, k_ref.shape[0], ch):
                s = lax.dot_general(q, k_ref[c0:c0 + ch, :], (((1,), (1,)), ((), ())), preferred_element_type=F32)
                m_new = jnp.maximum(m, s.max(axis=-1, keepdims=True))
                alpha = jnp.exp2(m - m_new)
                p = jnp.exp2(s - m_new)
                den = alpha * den + p.sum(axis=-1, keepdims=True)
                acc = alpha * acc + jnp.dot(p.astype(BF16), v_ref[c0:c0 + ch, :], preferred_element_type=F32)
                m = m_new
        o_ref[sl, :] = (acc / den).astype(o_ref.dtype)


def attention(q, kvs, tq=1024, part_rows=512, kv_chunk=4096):
    b, h, l, _ = q.shape
    tq = _tile(l, tq)
    n_part = max(1, tq // part_rows)
    assert all(k.shape[2] % min(kv_chunk, k.shape[2]) == 0 for k, _ in kvs)
    specs = [pl.BlockSpec((None, None, tq, 256), lambda bi, hi, i: (bi, hi, i, 0))]
    args = [q]
    for k, v in kvs:
        lk = k.shape[2]
        specs += [pl.BlockSpec((None, None, lk, 256), lambda bi, hi, i: (bi, hi, 0, 0)),
                  pl.BlockSpec((None, None, lk, 128), lambda bi, hi, i: (bi, hi, 0, 0))]
        args += [k, v]
    return _pc(functools.partial(_attn_kernel, n_seg=len(kvs), n_part=n_part, kv_chunk=kv_chunk),
               jax.ShapeDtypeStruct((b, l, h * 128), BF16), (b, h, l // tq), specs,
               pl.BlockSpec((None, tq, 128), lambda bi, hi, i: (bi, i, hi)), name="attention")(*args)


def s5_tables(lam_re, lam_im, log_step, b_re, b_im, c_re, c_im):
    t = S5_T
    step = jnp.exp(log_step.astype(F32))[..., None]
    re, im = lam_re.astype(F32) * step, lam_im.astype(F32) * step
    d = jnp.arange(t + 1, dtype=F32)[:, None, None, None]
    mag = jnp.exp(d * re)
    ar, ai = mag * jnp.cos(d * im), mag * jnp.sin(d * im)
    lr, li = lam_re.astype(F32), lam_im.astype(F32)
    den = lr * lr + li * li
    er, ei = ar[1] - 1.0, ai[1]
    cr, ci = (er * lr + ei * li) / den, (ei * lr - er * li) / den
    br, bi = b_re.astype(F32), b_im.astype(F32)
    bbr = cr[..., None] * br - ci[..., None] * bi
    bbi = cr[..., None] * bi + ci[..., None] * br
    ccr, cci = c_re.astype(F32), c_im.astype(F32)
    car = ccr[None] * ar[:, :, :, None, :] - cci[None] * ai[:, :, :, None, :]
    cai = ccr[None] * ai[:, :, :, None, :] + cci[None] * ar[:, :, :, None, :]
    kd = jnp.einsum("dxgnp,xgpm->xgdnm", car[:t], bbr) - jnp.einsum("dxgnp,xgpm->xgdnm", cai[:t], bbi)
    kcat = jnp.stack([kd[0], kd[1][:, ::-1]], axis=1).transpose(0, 1, 4, 2, 3).reshape(S5_G, 2, S5_N, t * S5_N)
    pw = (jnp.arange(t - 1, -1, -1), jnp.arange(t))
    w_re, w_im = [], []
    for x in range(2):
        a_r, a_i = ar[pw[x], x], ai[pw[x], x]
        w_re.append(a_r[..., None] * bbr[x][None] - a_i[..., None] * bbi[x][None])
        w_im.append(a_r[..., None] * bbi[x][None] + a_i[..., None] * bbr[x][None])
    ws = jnp.concatenate(w_re + w_im, axis=2).transpose(1, 0, 3, 2).reshape(S5_G, t * S5_N, 4 * S5_P)
    pv = (jnp.arange(1, t + 1), jnp.arange(t, 0, -1))
    vm = jnp.concatenate([car[pv[0], 0], car[pv[1], 1], -cai[pv[0], 0], -cai[pv[1], 1]], axis=-1)
    vm = vm.transpose(1, 3, 0, 2).reshape(S5_G, 4 * S5_P, t * S5_N)
    dec = jnp.stack([jnp.concatenate([ar[t, 0], ar[t, 1]], axis=-1),
                     jnp.concatenate([ai[t, 0], ai[t, 1]], axis=-1)], axis=1)
    return kcat, ws.astype(BF16), vm.astype(BF16), dec


def _s5a_kernel(u_ref, k_ref, ws_ref, y_ref, s_ref, m_ref):
    kf, kb = k_ref[0], k_ref[1]
    lane = lax.broadcasted_iota(jnp.int32, kf.shape, 1)
    for s in range(S5_T):
        lo, hi = S5_N * s, S5_N * (s + 1)
        f = kf if s == 0 else jnp.where(lane >= lo, pltpu.roll(kf, lo, axis=1), 0.0)
        bk = kb if s == S5_T - 1 else jnp.where(lane < hi, pltpu.roll(kb, hi, axis=1), 0.0)
        m_ref[lo:hi, :] = (f + bk).astype(BF16)
    u = u_ref[...]
    y_ref[...] = jnp.dot(u, m_ref[...], preferred_element_type=F32)
    s_ref[...] = jnp.dot(u, ws_ref[...], preferred_element_type=F32)


def _s5b_kernel(yi_ref, s_ref, x0_ref, dec_ref, vm_ref, y_ref, xf_ref, xa_ref, xb_ref, *, nchunk, nbatch):
    cps = 8 // nbatch
    nslab = nchunk // cps
    ar = dec_ref[0:1, :]
    ai = dec_ref[1:2, :]
    fwd_lane = lax.broadcasted_iota(jnp.int32, (nbatch, 128), 1) < S5_P

    def step(jj, carry):
        xr, xi = carry
        rf = pl.ds(pl.multiple_of(jj * 8, 8), 8)
        rb = pl.ds(pl.multiple_of((nslab - 1 - jj) * 8, 8), 8)
        sf, sb = s_ref[rf, :], s_ref[rb, :]
        seen = []
        for i in range(cps):
            lo, hi = i * nbatch, (i + 1) * nbatch
            ml, mh = (cps - 1 - i) * nbatch, (cps - i) * nbatch
            seen.append(jnp.concatenate([xr, xi], axis=1))
            sr = jnp.where(fwd_lane, sf[lo:hi, :128], sb[ml:mh, :128])
            si = jnp.where(fwd_lane, sf[lo:hi, 128:], sb[ml:mh, 128:])
            xr, xi = ar * xr - ai * xi + sr, ar * xi + ai * xr + si
        xa_ref[rf, :] = jnp.concatenate(seen, axis=0)
        xb_ref[rb, :] = jnp.concatenate(seen[::-1], axis=0)
        return xr, xi

    x0 = x0_ref[0:nbatch, :]
    xr, xi = lax.fori_loop(0, nslab, step, (x0[:, :128], x0[:, 128:]), unroll=2)
    xf_ref[...] = jnp.zeros_like(xf_ref)
    xf_ref[0:nbatch, :] = jnp.concatenate([xr, xi], axis=1)
    pick = (lax.broadcasted_iota(jnp.int32, xa_ref.shape, 1) % 128) < S5_P
    xin = jnp.where(pick, xa_ref[...], xb_ref[...])
    y = yi_ref[...] + jnp.dot(xin.astype(BF16), vm_ref[...], preferred_element_type=F32)
    y_ref[...] = y.astype(y_ref.dtype)


def s5_scan(u, x0, tabs):
    kcat, ws, vm, dec = tabs
    b, l, _ = u.shape
    j = l // S5_T
    r = b * j
    assert 8 % b == 0 and j % (8 // b) == 0
    ug = u.astype(BF16).reshape(b, j, S5_T, S5_G, S5_N).transpose(3, 1, 0, 2, 4).reshape(S5_G, r, GROUP_W)
    g3 = lambda w: pl.BlockSpec((None, r, w), lambda g: (g, 0, 0))
    yi, s = _pc(_s5a_kernel,
                (jax.ShapeDtypeStruct((S5_G, r, GROUP_W), F32), jax.ShapeDtypeStruct((S5_G, r, 256), F32)),
                (S5_G,),
                [g3(GROUP_W), pl.BlockSpec((None, 2, S5_N, GROUP_W), lambda g: (g, 0, 0, 0)),
                 pl.BlockSpec((None, GROUP_W, 256), lambda g: (g, 0, 0))],
                (g3(GROUP_W), g3(256)),
                scratch=[pltpu.VMEM((GROUP_W, GROUP_W), BF16)], name="s5a")(ug, kcat, ws)
    y, xf = _pc(functools.partial(_s5b_kernel, nchunk=j, nbatch=b),
                (jax.ShapeDtypeStruct((S5_G, r, GROUP_W), BF16), jax.ShapeDtypeStruct((S5_G, 8, 256), F32)),
                (S5_G,),
                [g3(GROUP_W), g3(256),
                 pl.BlockSpec((None, 8, 256), lambda g: (g, 0, 0)),
                 pl.BlockSpec((None, 2, 128), lambda g: (g, 0, 0)),
                 pl.BlockSpec((None, 256, GROUP_W), lambda g: (g, 0, 0))],
                (g3(GROUP_W), pl.BlockSpec((None, 8, 256), lambda g: (g, 0, 0))),
                scratch=[pltpu.VMEM((r, 256), F32), pltpu.VMEM((r, 256), F32)], name="s5b")(yi, s, x0, dec, vm)
    y = y.reshape(S5_G, j, b, S5_T, S5_N).transpose(2, 1, 3, 0, 4).reshape(b, l, GROUP_W)
    return y, xf


def _s5_out_kernel(y_ref, u_ref, d_ref, w_ref, b_ref, o_ref):
    y = jax.nn.gelu(y_ref[...].astype(F32) + d_ref[...] * u_ref[...].astype(F32))
    z = jnp.dot(y.astype(BF16), w_ref[...], preferred_element_type=F32) + b_ref[...]
    o_ref[...] = (y * jax.nn.sigmoid(z)).astype(o_ref.dtype)


def s5_out(y, proj, d, w, bias, tm=1024):
    b, l, _ = y.shape
    tm = _tile(l, tm)
    row = pl.BlockSpec((1, GROUP_W), lambda bi, i: (0, 0))
    return _pc(_s5_out_kernel, jax.ShapeDtypeStruct((b, l, GROUP_W), BF16), (b, l // tm),
               [pl.BlockSpec((None, tm, GROUP_W), lambda bi, i: (bi, i, 0)),
                pl.BlockSpec((None, tm, GROUP_W), lambda bi, i: (bi, i, PB_S5U)),
                row, pl.BlockSpec((GROUP_W, GROUP_W), lambda bi, i: (0, 0)), row],
               pl.BlockSpec((None, tm, GROUP_W), lambda bi, i: (bi, i, 0)), name="s5_out")(y, proj, d, w, bias)


def _hgrn_kernel(z_ref, i_ref, q_ref, la_ref, lb_ref, s0_ref, o_ref, sfin_ref, st_ref, *, reverse, nsub):
    c = HG_SUB

    @pl.when(pl.program_id(1) == 0)
    def _():
        st_ref[...] = s0_ref[...]

    hlf = c // 2
    ones = jnp.ones((HG_D, HG_D), BF16)
    row = lax.broadcasted_iota(jnp.int32, (c, HG_D), 0)
    row8 = lax.broadcasted_iota(jnp.int32, (hlf, HG_D), 0)

    def sub(jj, carry):
        j = (nsub - 1 - jj) if reverse else jj
        rows = pl.ds(pl.multiple_of(j * c, c), c)
        for h in range(HG_HEADS):
            sl = slice(HG_D * h, HG_D * (h + 1))
            z = z_ref[rows, sl].astype(F32)
            v = i_ref[rows, sl].astype(F32)
            q = _silu(q_ref[rows, sl].astype(F32))
            a = la_ref[:, sl]
            bb = lb_ref[:, sl] + (jnp.minimum(z, 0.0) - jnp.log1p(jnp.exp(-jnp.abs(z))))
            g = jnp.maximum(a, bb) + jnp.log1p(jnp.exp(-jnp.abs(a - bb)))
            k = 1.0 - jnp.exp(g)
            gc = g
            for sh in (1, 2, 4, 8):
                if reverse:
                    gc = gc + jnp.where(row < c - sh, pltpu.roll(gc, c - sh, axis=0), 0.0)
                else:
                    gc = gc + jnp.where(row >= sh, pltpu.roll(gc, sh, axis=0), 0.0)
            gc = gc * LOG2E
            gk = gc - jnp.log(jnp.maximum(k, 0.0)) * LOG2E
            blocks, meta = [], []
            for s in range(c):
                for half in range(2):
                    r0 = half * hlf
                    if (r0 > s) if reverse else (r0 + hlf - 1 < s):
                        continue
                    e = gc[r0:r0 + hlf, :] - gk[s:s + 1, :]
                    if not ((r0 + hlf - 1 <= s) if reverse else (r0 >= s)):
                        keep = (row8 + r0 <= s) if reverse else (row8 + r0 >= s)
                        e = jnp.where(keep, e, -1e30)
                    blocks.append(q[r0:r0 + hlf, :] * jnp.exp2(e))
                    meta.append((half, s))
            rsum = jnp.dot(jnp.concatenate(blocks, axis=0).astype(BF16), ones, preferred_element_type=F32)
            halves = [jnp.zeros((hlf, HG_D), F32), jnp.zeros((hlf, HG_D), F32)]
            for i, (half, s) in enumerate(meta):
                halves[half] = halves[half] + rsum[hlf * i:hlf * (i + 1), :] * v[s:s + 1, :]
            o = jnp.concatenate(halves, axis=0)
            st = st_ref[h]
            o = o + lax.dot_general((q * jnp.exp2(gc)).astype(BF16), st.astype(BF16),
                                    (((1,), (1,)), ((), ())), preferred_element_type=F32)
            gl = gc[0:1, :] if reverse else gc[c - 1:c, :]
            kd = k * jnp.exp2(gl - gc)
            ds = lax.dot_general(v.astype(BF16), kd.astype(BF16), (((0,), (0,)), ((), ())),
                                 preferred_element_type=F32)
            st_ref[h] = st * jnp.exp2(gl) + ds
            o_ref[rows, sl] = o.astype(o_ref.dtype)
        return carry

    lax.fori_loop(0, nsub, sub, 0, unroll=4)

    @pl.when(pl.program_id(1) == pl.num_programs(1) - 1)
    def _():
        sfin_ref[...] = st_ref[...]


def hgrn_dir(proj, la, lb1, s0, reverse, tl=512):
    b, l, _ = proj.shape
    tl = _tile(l, tl)
    nb = l // tl
    blk = (lambda i: nb - 1 - i) if reverse else (lambda i: i)
    col = lambda c: pl.BlockSpec((None, tl, GROUP_W), lambda bi, i: (bi, blk(i), c))
    row = pl.BlockSpec((1, GROUP_W), lambda bi, i: (0, 0))
    st = pl.BlockSpec((None, HG_HEADS, HG_D, HG_D), lambda bi, i: (bi, 0, 0, 0))
    return _pc(functools.partial(_hgrn_kernel, reverse=reverse, nsub=tl // HG_SUB),
               (jax.ShapeDtypeStruct((b, l, GROUP_W), BF16), jax.ShapeDtypeStruct(s0.shape, F32)),
               (b, nb),
               [col(PB_HGB if reverse else PB_HGF), col(PB_HGI), col(PB_HGQ), row, row, st],
               (pl.BlockSpec((None, tl, GROUP_W), lambda bi, i: (bi, blk(i), 0)), st),
               scratch=[pltpu.VMEM((HG_HEADS, HG_D, HG_D), F32)],
               name="hgrn_bwd" if reverse else "hgrn_fwd")(proj, proj, proj, la, lb1, s0)


def _hg_out_kernel(of_ref, ob_ref, g_ref, gn_ref, y_ref):
    o = of_ref[...].astype(F32) + ob_ref[...].astype(F32)
    gate = _silu(g_ref[...].astype(F32))
    for h in range(HG_HEADS):
        sl = slice(HG_D * h, HG_D * (h + 1))
        oh = o[:, sl]
        y = oh * lax.rsqrt(jnp.mean(oh * oh, axis=-1, keepdims=True) + EPS) * gn_ref[...]
        y_ref[:, sl] = (y * gate[:, sl]).astype(y_ref.dtype)


def hg_out(of, ob, proj, gn, tm=1024):
    b, l, _ = of.shape
    tm = _tile(l, tm)
    t3 = lambda c: pl.BlockSpec((None, tm, GROUP_W), lambda bi, i: (bi, i, c))
    return _pc(_hg_out_kernel, jax.ShapeDtypeStruct((b, l, GROUP_W), BF16), (b, l // tm),
               [t3(0), t3(0), t3(PB_HGG), pl.BlockSpec((1, HG_D), lambda bi, i: (0, 0))],
               t3(0), name="hg_out")(of, ob, proj, gn)


def _fft_factors(n_fft):
    n1 = {8192: 64, 512: 32, 1024: 32, 2048: 32, 4096: 64}[n_fft]
    return n1, n_fft // n1


def _dft_tables(na, nb, ka, mb, inverse):
    n = na * nb
    sgn = 1.0 if inverse else -1.0

    def stacked(phase, scale=1.0):
        ang = sgn * 2.0 * np.pi * (phase % n) / n
        fr, fi = np.cos(ang) * scale, np.sin(ang) * scale
        return np.concatenate([np.concatenate([fr, -fi], axis=-1), np.concatenate([fi, fr], axis=-1)], axis=-2)

    fa = stacked(np.outer(np.arange(na), np.arange(ka)) * nb)
    p, q, b = np.arange(na)[:, None, None], np.arange(mb)[None, :, None], np.arange(nb)[None, None, :]
    fb = stacked(b * p + b * q * na, (1.0 / n) if inverse else 1.0)
    return jnp.asarray(fa, BF16), jnp.asarray(fb, BF16)


def _fft_a_kernel(z_ref, f_ref, o_ref, *, g, na, is_complex):
    f = f_ref[...]
    for i in range(g):
        z = z_ref[:, i] if is_complex else z_ref[i]
        if is_complex:
            z = z.reshape(z.shape[0] * z.shape[1], z.shape[2])
        r = jnp.dot(f, z, preferred_element_type=F32)
        o_ref[0, i] = r[:na]
        o_ref[1, i] = r[na:]


def fft_stage_a(z, fa, g=8):
    is_complex = z.ndim == 4
    nb, ka, pc = z.shape[-3:]
    na = fa.shape[0] // 2
    g = min(g, nb)
    zspec = (pl.BlockSpec((2, g, ka, pc), lambda j: (0, j, 0, 0)) if is_complex
             else pl.BlockSpec((g, ka, pc), lambda j: (j, 0, 0)))
    return _pc(functools.partial(_fft_a_kernel, g=g, na=na, is_complex=is_complex),
               jax.ShapeDtypeStruct((2, nb, na, pc), F32), (nb // g,),
               [zspec, pl.BlockSpec(fa.shape, lambda j: (0, 0))],
               pl.BlockSpec((2, g, na, pc), lambda j: (0, j, 0, 0)), name="fft_a")(z, fa)


def _fft_b_kernel(*refs, mode, mb, kb):
    a_ref, f_ref = refs[0], refs[1]
    o_ref = refs[-1]
    nb = a_ref.shape[1]
    for i in range(kb):
        f = f_ref[i].astype(F32)
        z = (jnp.dot(f[:, :nb], a_ref[0, :, i, :], preferred_element_type=F32)
             + jnp.dot(f[:, nb:], a_ref[1, :, i, :], preferred_element_type=F32))
        zr, zi = z[:mb], z[mb:]
        if mode == "filter":
            h_ref = refs[2]
            hr, hi = h_ref[0, i].astype(F32), h_ref[1, i].astype(F32)
            zr, zi = zr * hr - zi * hi, zr * hi + zi * hr
        elif mode == "final":
            v_ref, g_ref, b_ref = refs[2], refs[3], refs[4]
            zr = g_ref[0, i].astype(F32) * (zr + v_ref[0, i].astype(F32) * b_ref[...])
            zi = g_ref[1, i].astype(F32) * (zi + v_ref[1, i].astype(F32) * b_ref[...])
        o_ref[0, i] = zr.astype(o_ref.dtype)
        o_ref[1, i] = zi.astype(o_ref.dtype)


def fft_stage_b(a, fb, mode, extra=(), order=0, kb=16):
    _, nb, na, pc = a.shape
    mb = fb.shape[1] // 2
    c = GROUP_W
    kb = min(kb, na)
    blk = pl.BlockSpec((2, kb, mb, c), lambda k, p: (0, k, 0, p))
    specs = [pl.BlockSpec((2, nb, kb, c), lambda k, p: (0, 0, k, p)),
             pl.BlockSpec((kb, 2 * mb, 2 * nb), lambda k, p: (k, 0, 0))]
    args = [a, fb]
    if mode == "filter":
        (h,) = extra
        specs.append(pl.BlockSpec((2, kb, mb, c), lambda k, p: (0, k, 0, order)))
        args.append(h)
    elif mode == "final":
        v, gate, bias = extra
        specs += [blk, blk, pl.BlockSpec((1, c), lambda k, p: (0, 0))]
        args += [v, gate, bias]
    return _pc(functools.partial(_fft_b_kernel, mode=mode, mb=mb, kb=kb),
               jax.ShapeDtypeStruct((2, na, mb, pc), BF16), (na // kb, pc // c), specs, blk,
               name="fft_b_" + mode)(*args)


def _hy_filter_kernel(emb_ref, w1_ref, b1_ref, w2_ref, b2_ref, w3_ref, win_ref, o_ref, h_ref, *, na):
    hp = lax.Precision.HIGHEST

    @pl.when(pl.program_id(0) == 0)
    def _():
        rc = min(512, h_ref.shape[0])

        def chunk(i, carry):
            rows = pl.ds(pl.multiple_of(i * rc, rc), rc)
            h = jnp.sin(jnp.dot(emb_ref[rows, :], w1_ref[...], precision=hp, preferred_element_type=F32)
                        + b1_ref[...])
            h = jnp.sin(jnp.dot(h, w2_ref[...], precision=hp, preferred_element_type=F32) + b2_ref[...])
            row = lax.broadcasted_iota(jnp.int32, h.shape, 0) + i * rc
            causal = (row & (na - 1)) < na // 2
            h_ref[rows, :] = jnp.concatenate([jnp.where(causal, h, 0.0), jnp.where(causal, 0.0, h)],
                                             axis=1).astype(BF16)
            return carry

        lax.fori_loop(0, h_ref.shape[0] // rc, chunk, 0)

    taps = jnp.dot(h_ref[...], w3_ref[...], preferred_element_type=F32) * win_ref[...]
    o_ref[...] = (taps / jnp.sum(jnp.abs(taps), axis=0, keepdims=True)).astype(o_ref.dtype)


def hyena_filters(n, w1, b1, w2, b2, w3):
    na, nb = _fft_factors(2 * n)
    t = np.arange(n, dtype=np.float32)
    t_norm = t / max(n - 1, 1)
    bands = np.linspace(1e-4, HY_BANDS - 1, HY_BANDS, dtype=np.float32)
    ang = (2.0 * math.pi * t / n)[:, None] * bands[None, :]
    emb = np.concatenate([t_norm[:, None], np.cos(ang), -np.sin(ang)], axis=-1).astype(np.float32)
    emb = np.pad(emb, ((0, 0), (0, LANES - HY_EMB)))
    deltas = np.linspace(math.log(HY_TARGET) / HY_SLOW, math.log(HY_TARGET) / HY_FAST, GROUP_W, dtype=np.float32)
    win = (np.exp(-t_norm[:, None] * np.abs(deltas)[None, :]) + np.float32(HY_SHIFT)).astype(np.float32)
    back = np.concatenate([[0], np.arange(n - 2, -1, -1)])
    win_b = win[back]
    win_b[0] = 0.0
    order = (np.arange(na)[None, :] * nb + np.arange(nb)[:, None]).reshape(-1)
    emb2 = jnp.asarray(np.concatenate([emb, emb[back]], axis=0)[order])
    win2 = jnp.asarray(np.concatenate([win, win_b], axis=0)[order])
    w1p = jnp.pad(w1.astype(F32), ((0, LANES - HY_EMB), (0, 0)))
    w3r = w3.astype(F32).reshape(HY_HIDDEN, 2, 2, GROUP_W)
    w3c = jnp.concatenate([w3r[:, :, 0].reshape(HY_HIDDEN, 2 * GROUP_W),
                           w3r[:, :, 1].reshape(HY_HIDDEN, 2 * GROUP_W)], axis=0).astype(BF16)
    tc = LANES
    full = lambda a: pl.BlockSpec(a.shape, lambda j: (0, 0))
    b1r, b2r = b1.reshape(1, -1).astype(F32), b2.reshape(1, -1).astype(F32)
    w2f = w2.astype(F32)
    return _pc(functools.partial(_hy_filter_kernel, na=na),
               jax.ShapeDtypeStruct((2 * n, 2 * GROUP_W), BF16), (2 * GROUP_W // tc,),
               [full(emb2), full(w1p), full(b1r), full(w2f), full(b2r),
                pl.BlockSpec((2 * HY_HIDDEN, tc), lambda j: (0, j)),
                pl.BlockSpec((2 * n, tc), lambda j: (0, j % (GROUP_W // tc)))],
               pl.BlockSpec((2 * n, tc), lambda j: (0, j)),
               scratch=[pltpu.VMEM((2 * n, 2 * HY_HIDDEN), BF16)], name="hy_filter")(
        emb2, w1p, b1r, w2f, b2r, w3c, win2)


def hyena_spectrum(n, w1, b1, w2, b2, w3):
    na, nb = _fft_factors(2 * n)
    circ = hyena_filters(n, w1, b1, w2, b2, w3).reshape(nb, na, 2 * GROUP_W)
    fa, fb = _dft_tables(na, nb, na, nb, inverse=False)
    return fft_stage_b(fft_stage_a(circ, fa[:, :na], g=16), fb, "plain")


def _shortconv_kernel(p_ref, w_ref, b_ref, o_ref):
    p = p_ref[...].astype(F32)
    n = p.shape[0]
    row = lax.broadcasted_iota(jnp.int32, p.shape, 0)
    prev = jnp.where(row >= 1, pltpu.roll(p, 1, axis=0), 0.0)
    nxt = jnp.where(row < n - 1, pltpu.roll(p, n - 1, axis=0), 0.0)
    o_ref[...] = (prev * w_ref[0:1, :] + p * w_ref[1:2, :] + nxt * w_ref[2:3, :] + b_ref[...]).astype(o_ref.dtype)


def hyena_shortconv(proj, w, bias):
    b, n, _ = proj.shape
    return _pc(_shortconv_kernel, jax.ShapeDtypeStruct((3, 2, n, (b // 2) * GROUP_W), BF16), (b, 3),
               [pl.BlockSpec((None, n, GROUP_W), lambda bi, j: (bi, 0, PB_HY + j)),
                pl.BlockSpec((3, GROUP_W), lambda bi, j: (0, j)),
                pl.BlockSpec((1, GROUP_W), lambda bi, j: (0, j))],
               pl.BlockSpec((None, None, n, GROUP_W), lambda bi, j: (j, bi % 2, 0, bi // 2)),
               name="shortconv")(proj, w, bias)


def hyena_mixer(proj, spec, conv_w, conv_b, bias):
    b, n, _ = proj.shape
    na, nb = _fft_factors(2 * n)
    pc = (b // 2) * GROUP_W
    u = hyena_shortconv(proj, conv_w, conv_b).reshape(3, 2, na // 2, nb, pc).swapaxes(2, 3)
    fa, fb = _dft_tables(na, nb, na // 2, nb, inverse=False)
    ga, gb = _dft_tables(nb, na, nb, na // 2, inverse=True)
    z = u[0]
    for o in range(2):
        zf = fft_stage_b(fft_stage_a(z, fa, g=16), fb, "filter", (spec,), order=o)
        z = fft_stage_b(fft_stage_a(zf, ga), gb, "final", (z, u[1 + o], bias[o:o + 1]))
    return z.swapaxes(1, 2).reshape(2, n, pc)


def _rot_cols(w):
    return jnp.concatenate([-w[..., 16:32], w[..., 0:16], -w[..., 48:64], w[..., 32:48]], axis=-1)


def _rot_perm(g):
    return jnp.concatenate([g[16:32], g[0:16], g[48:64], g[32:48]])


def _prep_layer(w_in, w_uq, w_ukv, q_a_g, kv_a_g, q_g, k_g):
    d = w_in.shape[0]
    ckv, krope, rest = w_in[:, :256], w_in[:, 256:320], w_in[:, 320:]
    w_in_p = jnp.concatenate([rest, ckv, krope, _rot_cols(krope), jnp.zeros((d, 128), w_in.dtype)], axis=1)
    wq = w_uq.reshape(GROUP_W, MLA_HEADS, MLA_QK)
    wq = jnp.concatenate([wq, _rot_cols(wq[..., MLA_NOPE:])], axis=-1).reshape(GROUP_W, MLA_HEADS * 256)
    r2 = lambda a: a.reshape(1, -1).astype(F32)
    mla = dict(wkv=w_ukv.astype(BF16), wq=wq.astype(BF16), gkv=r2(kv_a_g), gqa=r2(q_a_g),
               gkn=r2(k_g[:MLA_NOPE]), gkt=r2(jnp.concatenate([k_g[MLA_NOPE:], _rot_perm(k_g[MLA_NOPE:])])),
               gqn=r2(q_g[:MLA_NOPE]), gqt=r2(jnp.concatenate([q_g[MLA_NOPE:], _rot_perm(q_g[MLA_NOPE:])])))
    return w_in_p.astype(BF16), mla


def _rope_table(n_rows):
    row = np.repeat(np.arange(n_rows, dtype=np.float32), GRID_W)
    col = np.tile(np.arange(GRID_W, dtype=np.float32), n_rows)
    n_freq = MLA_ROPE // 4
    inv_freq = (ROPE_BASE ** (-np.arange(n_freq, dtype=np.float32) / n_freq)).astype(np.float32)
    ang_r, ang_c = row[:, None] * inv_freq, col[:, None] * inv_freq
    ang = np.concatenate([ang_r, ang_r, ang_c, ang_c], axis=-1)
    return jnp.asarray(np.concatenate([np.cos(ang), np.sin(ang)], axis=-1), F32)


def kernel(x, c, ctx, c_ctx, w_mod, b_mod, norm1, norm2, w_in, w_out, mla_q_a_norm, mla_kv_a_norm, mla_w_uq, mla_w_ukv, mla_q_norm, mla_k_norm, s5_lam_re, s5_lam_im, s5_log_step, s5_b_re, s5_b_im, s5_c_re, s5_c_im, s5_d, s5_w_glu, s5_b_glu, hg_lower_bounds, hg_o_norm, hy_conv_w, hy_conv_b, hy_w1, hy_b1, hy_w2, hy_b2, hy_w3, hy_bias, mlp_w1, mlp_w2):
    bsz, seq, dm = x.shape
    n_ctx = ctx.shape[1]
    depth = w_mod.shape[0]
    assert bsz % 2 == 0 and bsz <= 8 and seq % GRID_W == 0 and S5_T * S5_N == GROUP_W
    r2 = lambda a: a.reshape(1, -1).astype(F32)

    cs_l = _rope_table(seq // GRID_W)
    cs_c = jnp.concatenate([jnp.ones((n_ctx, MLA_ROPE), F32), jnp.zeros((n_ctx, MLA_ROPE), F32)], axis=-1)
    sm = jax.nn.softmax(hg_lower_bounds.astype(F32), axis=1)
    lower = jnp.clip(jnp.cumsum(sm, axis=1) - sm[:, :1], 0.0, 1.0)
    c8 = jnp.concatenate([c.astype(F32), c_ctx.astype(F32)[None], jnp.zeros((7 - bsz, dm), F32)], axis=0)
    zero_state = jnp.zeros((bsz, HG_HEADS, HG_D, HG_D), F32)
    zero_x = jnp.zeros((S5_G, 8, 256), F32)

    xc = ctx
    for l in range(depth):
        last = l == depth - 1
        mod = modvec(c8, w_mod[l].astype(F32), r2(b_mod[l]))
        mod_l = [mod[:bsz, i * dm:(i + 1) * dm][:, None, :] for i in range(N_MOD)]
        mod_c = [mod[bsz:bsz + 1, i * dm:(i + 1) * dm][:, None, :] for i in range(N_MOD)]
        w_in_p, mla_p = _prep_layer(w_in[l], mla_w_uq[l], mla_w_ukv[l], mla_q_a_norm[l], mla_kv_a_norm[l],
                                    mla_q_norm[l], mla_k_norm[l])
        pl_ = modproj(x, mod_l[0], mod_l[1], r2(norm1[l]), w_in_p, BF16, tm=1024, tn=N_PROJ // 4)
        pc_ = modproj(xc, mod_c[0], mod_c[1], r2(norm1[l]), w_in_p, BF16, tm=1024, tn=N_PROJ // 4)

        k_c, v_c, q_c = mla_prep(pc_, cs_c, mla_p, with_q=not last)
        k_l, v_l, q_l = mla_prep(pl_, cs_l, mla_p, with_q=True)
        y_a = attention(q_l, [(k_l, v_l), (k_c, v_c)])

        tabs = s5_tables(s5_lam_re[l], s5_lam_im[l], s5_log_step[l], s5_b_re[l], s5_b_im[l], s5_c_re[l], s5_c_im[l])
        yc_s5, fin = s5_scan(pc_[..., PB_S5U * GROUP_W:(PB_S5U + 1) * GROUP_W], zero_x, tabs)
        yl_s5, _ = s5_scan(pl_[..., PB_S5U * GROUP_W:(PB_S5U + 1) * GROUP_W], fin, tabs)
        glu_w, glu_b = s5_w_glu[l].astype(BF16), r2(s5_b_glu[l])
        y_b = s5_out(yl_s5, pl_, r2(s5_d[l]), glu_w, glu_b)

        la_f, lb_f = r2(jnp.log(lower[0, l])), r2(jnp.log1p(-lower[0, l]))
        la_b, lb_b = r2(jnp.log(lower[1, l])), r2(jnp.log1p(-lower[1, l]))
        oc_f, s_f = hgrn_dir(pc_, la_f, lb_f, zero_state, reverse=False)
        oc_b, s_b = hgrn_dir(pc_, la_b, lb_b, zero_state, reverse=True)
        ol_f, _ = hgrn_dir(pl_, la_f, lb_f, s_f, reverse=False)
        ol_b, _ = hgrn_dir(pl_, la_b, lb_b, s_b, reverse=True)
        y_c = hg_out(ol_f, ol_b, pl_, r2(hg_o_norm[l]))

        hy_args = (hy_conv_w[l].astype(F32), r2(hy_conv_b[l]), hy_bias[l].astype(F32))
        spec_l = hyena_spectrum(seq, hy_w1[l], hy_b1[l], hy_w2[l], hy_b2[l], hy_w3[l])
        y_d = hyena_mixer(pl_, spec_l, *hy_args)

        w_out_b = w_out[l].astype(BF16)
        w1_b, w2_b = mlp_w1[l].astype(BF16), mlp_w2[l].astype(BF16)
        x = mix_out([y_a, y_b, y_c], y_d, w_out_b, x, mod_l[2])
        h1 = modproj(x, mod_l[3], mod_l[4], r2(norm2[l]), w1_b, BF16, act="relu2", tm=1024, tn=2048)
        x = mlp_out(h1, w2_b, x, mod_l[5])

        if not last:
            y_ac = attention(q_c, [(k_c, v_c)])
            y_bc = s5_out(yc_s5, pc_, r2(s5_d[l]), glu_w, glu_b)
            y_cc = hg_out(oc_f, oc_b, pc_, r2(hg_o_norm[l]))
            spec_c = hyena_spectrum(n_ctx, hy_w1[l], hy_b1[l], hy_w2[l], hy_b2[l], hy_w3[l])
            y_dc = hyena_mixer(pc_, spec_c, *hy_args)
            xc = mix_out([y_ac, y_bc, y_cc], y_dc, w_out_b, xc, mod_c[2])
            h1c = modproj(xc, mod_c[3], mod_c[4], r2(norm2[l]), w1_b, BF16, act="relu2", tm=1024)
            xc = mlp_out(h1c, w2_b, xc, mod_c[5])
    return x
```

```python
import functools
import math

import numpy as np
import jax
import jax.numpy as jnp
from jax import lax
from jax.experimental import pallas as pl
from jax.experimental.pallas import tpu as pltpu

F32 = jnp.float32
BF16 = jnp.bfloat16
EPS = 1e-6

GRID_W = 64
GROUP_W = 512
N_MOD = 6
MLA_NOPE = 128
MLA_ROPE = 64
MLA_QK = MLA_NOPE + MLA_ROPE
MLA_HEADS = 4
KV_LORA = 256
MLA_SCALE = 1.0 / math.sqrt(MLA_QK)
LOG2E = 1.0 / math.log(2.0)
ROPE_BASE = 10000.0
S5_G = 32
S5_N = 16
S5_P = 64
S5_T = 32
HG_HEADS = 4
HG_D = 128
HG_SUB = 16
HY_EMB = 33
HY_BANDS = 16
HY_HIDDEN = 64
HY_TARGET = 1e-2
HY_FAST = 0.3
HY_SLOW = 1.5
HY_SHIFT = 0.05

LANES = 128
VMEM_LIMIT_MB = 56

PB_S5U, PB_HGF, PB_HGB, PB_HGI, PB_CQ, PB_HGQ, PB_HGG, PB_HY, PB_KV = 0, 1, 2, 3, 4, 5, 6, 7, 10
N_PROJ = 11 * GROUP_W


def _pc(kernel, out_shape, grid, in_specs, out_specs, scratch=(), vmem_mb=VMEM_LIMIT_MB, name=None):
    return pl.pallas_call(
        kernel, out_shape=out_shape, grid=grid, in_specs=in_specs, out_specs=out_specs,
        scratch_shapes=list(scratch), name=name,
        compiler_params=pltpu.CompilerParams(
            dimension_semantics=("arbitrary",) * len(grid), vmem_limit_bytes=vmem_mb << 20))


def _tile(n, pref):
    t = min(n, pref)
    while n % t:
        t //= 2
    return t


def _silu(x):
    return x * jax.nn.sigmoid(x)


def _modvec_kernel(c_ref, w_ref, b_ref, o_ref):
    s = _silu(c_ref[...])
    o_ref[...] = jnp.dot(s.astype(BF16), w_ref[...].astype(BF16), preferred_element_type=F32) + b_ref[...]


def modvec(c8, w, b):
    d, n = w.shape
    tn = _tile(n, 1536)
    return _pc(_modvec_kernel, jax.ShapeDtypeStruct((8, n), F32), (n // tn,),
               [pl.BlockSpec((8, d), lambda j: (0, 0)),
                pl.BlockSpec((d, tn), lambda j: (0, j)),
                pl.BlockSpec((1, tn), lambda j: (0, j))],
               pl.BlockSpec((8, tn), lambda j: (0, j)), name="modvec")(c8, w, b)


def _modproj_kernel(x_ref, sh_ref, sc_ref, g_ref, w_ref, o_ref, h_ref, *, act):
    @pl.when(pl.program_id(2) == 0)
    def _():
        x = x_ref[...]
        ms = jnp.mean(x * x, axis=-1, keepdims=True)
        y = x * lax.rsqrt(ms + EPS) * g_ref[...]
        h_ref[...] = (y * (1.0 + sc_ref[...]) + sh_ref[...]).astype(BF16)

    acc = jnp.dot(h_ref[...], w_ref[...], preferred_element_type=F32)
    if act == "relu2":
        acc = jnp.square(jnp.maximum(acc, 0.0))
    o_ref[...] = acc.astype(o_ref.dtype)


def modproj(x, shift, scale, g, w, out_dtype, act=None, tm=512, tn=512):
    b, l, d = x.shape
    n = w.shape[1]
    tm, tn = _tile(l, tm), _tile(n, tn)
    per_b = shift.shape[0] == b
    mod_map = (lambda bi, i, j: (bi, 0, 0)) if per_b else (lambda bi, i, j: (0, 0, 0))
    return _pc(functools.partial(_modproj_kernel, act=act),
               jax.ShapeDtypeStruct((b, l, n), out_dtype), (b, l // tm, n // tn),
               [pl.BlockSpec((None, tm, d), lambda bi, i, j: (bi, i, 0)),
                pl.BlockSpec((None, 1, d), mod_map),
                pl.BlockSpec((None, 1, d), mod_map),
                pl.BlockSpec((1, d), lambda bi, i, j: (0, 0)),
                pl.BlockSpec((d, tn), lambda bi, i, j: (0, j))],
               pl.BlockSpec((None, tm, tn), lambda bi, i, j: (bi, i, j)),
               scratch=[pltpu.VMEM((tm, d), BF16)], name="modproj_" + (act or "lin"))(x, shift, scale, g, w)


def _mmres_kernel(*refs, n_lhs, kw):
    a_refs = refs[:n_lhs]
    w_ref, res_ref, gate_ref, o_ref = refs[n_lhs:]
    acc = None
    for i, a in enumerate(a_refs):
        d = jnp.dot(a[...], w_ref[i * kw:(i + 1) * kw, :], preferred_element_type=F32)
        acc = d if acc is None else acc + d
    o_ref[...] = res_ref[...] + gate_ref[...] * acc


def mix_out(ys, yd_hlay, w, res, gate, tm=1024, tn=1024):
    b, l, kw = ys[0].shape
    n = w.shape[1]
    tm, tn = _tile(l, tm), _tile(n, tn)
    per_b = gate.shape[0] == b
    gmap = (lambda bi, i, j: (bi, 0, j)) if per_b else (lambda bi, i, j: (0, 0, j))
    y_spec = pl.BlockSpec((None, tm, kw), lambda bi, i, j: (bi, i, 0))
    yd_spec = pl.BlockSpec((None, tm, kw), lambda bi, i, j: (bi % 2, i, bi // 2))
    return _pc(functools.partial(_mmres_kernel, n_lhs=4, kw=kw),
               jax.ShapeDtypeStruct(res.shape, F32), (b, l // tm, n // tn),
               [y_spec, y_spec, y_spec, yd_spec,
                pl.BlockSpec((4 * kw, tn), lambda bi, i, j: (0, j)),
                pl.BlockSpec((None, tm, tn), lambda bi, i, j: (bi, i, j)),
                pl.BlockSpec((None, 1, tn), gmap)],
               pl.BlockSpec((None, tm, tn), lambda bi, i, j: (bi, i, j)), name="mix_out")(*ys, yd_hlay, w, res, gate)


def _mmresk_kernel(a_ref, w_ref, res_ref, gate_ref, o_ref, acc_ref):
    k = pl.program_id(3)

    @pl.when(k == 0)
    def _():
        acc_ref[...] = jnp.zeros_like(acc_ref)

    acc_ref[...] += jnp.dot(a_ref[...], w_ref[...], preferred_element_type=F32)

    @pl.when(k == pl.num_programs(3) - 1)
    def _():
        o_ref[...] = res_ref[...] + gate_ref[...] * acc_ref[...]


def mlp_out(a, w, res, gate, tm=1024, tn=1024, tk=2048):
    b, l, kk = a.shape
    n = w.shape[1]
    tm, tn, tk = _tile(l, tm), _tile(n, tn), _tile(kk, tk)
    per_b = gate.shape[0] == b
    gmap = (lambda bi, i, j, k: (bi, 0, j)) if per_b else (lambda bi, i, j, k: (0, 0, j))
    return _pc(_mmresk_kernel, jax.ShapeDtypeStruct(res.shape, F32), (b, l // tm, n // tn, kk // tk),
               [pl.BlockSpec((None, tm, tk), lambda bi, i, j, k: (bi, i, k)),
                pl.BlockSpec((tk, tn), lambda bi, i, j, k: (k, j)),
                pl.BlockSpec((None, tm, tn), lambda bi, i, j, k: (bi, i, j)),
                pl.BlockSpec((None, 1, tn), gmap)],
               pl.BlockSpec((None, tm, tn), lambda bi, i, j, k: (bi, i, j)),
               scratch=[pltpu.VMEM((tm, tn), F32)], name="mlp_out")(a, w, res, gate)


def _mla_prep_kernel(kv_ref, cq_ref, cs_ref, gkv_ref, gqa_ref, gkn_ref, gkt_ref, gqn_ref, gqt_ref,
                     wkv_ref, wq_ref, k_ref, v_ref, q_ref):
    lane = lax.broadcasted_iota(jnp.int32, (1, LANES), 1)
    low = (lane < MLA_ROPE).astype(F32)
    cs = cs_ref[...]
    kv = kv_ref[...].astype(F32)
    ckv = kv[:, :KV_LORA]
    ckvn = ckv * lax.rsqrt(jnp.mean(ckv * ckv, axis=-1, keepdims=True) + EPS) * gkv_ref[...]
    kvp = jnp.dot(ckvn.astype(BF16), wkv_ref[...], preferred_element_type=F32)
    pe2 = kv[:, KV_LORA:KV_LORA + LANES]
    ss_pe = jnp.sum(pe2 * pe2 * low, axis=-1, keepdims=True)
    pe_tab = cs * gkt_ref[...]
    for h in range(MLA_HEADS):
        knope = kvp[:, 256 * h:256 * h + 128]
        rs = lax.rsqrt((jnp.sum(knope * knope, axis=-1, keepdims=True) + ss_pe) * (1.0 / MLA_QK) + EPS)
        t = pe2 * pe_tab * rs
        t = (t + pltpu.roll(t, MLA_ROPE, axis=1)) * low
        k_ref[h, :, 0:128] = (knope * gkn_ref[...] * rs).astype(BF16)
        k_ref[h, :, 128:256] = t.astype(BF16)
        v_ref[h] = kvp[:, 256 * h + 128:256 * h + 256].astype(BF16)
    if q_ref is not None:
        cq = cq_ref[...].astype(F32)
        cqn = cq * lax.rsqrt(jnp.mean(cq * cq, axis=-1, keepdims=True) + EPS) * gqa_ref[...]
        qp = jnp.dot(cqn.astype(BF16), wq_ref[...], preferred_element_type=F32)
        q_tab = cs * gqt_ref[...]
        for h in range(MLA_HEADS):
            qn = qp[:, 256 * h:256 * h + 128]
            qe = qp[:, 256 * h + 128:256 * h + 256]
            ss = jnp.sum(qn * qn + qe * qe * low, axis=-1, keepdims=True)
            rs = lax.rsqrt(ss * (1.0 / MLA_QK) + EPS) * (MLA_SCALE * LOG2E)
            t = qe * q_tab * rs
            t = t + pltpu.roll(t, MLA_ROPE, axis=1)
            q_ref[h, :, 0:128] = (qn * gqn_ref[...] * rs).astype(BF16)
            q_ref[h, :, 128:256] = t.astype(BF16)


def _mla_prep_kernel_noq(kv_ref, cs_ref, gkv_ref, gkn_ref, gkt_ref, wkv_ref, k_ref, v_ref):
    _mla_prep_kernel(kv_ref, None, cs_ref, gkv_ref, None, gkn_ref, gkt_ref, None, None, wkv_ref, None,
                     k_ref, v_ref, None)


def mla_prep(proj, cs, p, with_q, tm=1024):
    b, l, _ = proj.shape
    tm = _tile(l, tm)
    row = lambda w: pl.BlockSpec((1, w), lambda bi, i: (0, 0))
    kv_spec = pl.BlockSpec((None, tm, GROUP_W), lambda bi, i: (bi, i, PB_KV))
    cq_spec = pl.BlockSpec((None, tm, GROUP_W), lambda bi, i: (bi, i, PB_CQ))
    cs_spec = pl.BlockSpec((tm, LANES), lambda bi, i: (i, 0))
    wkv_spec = pl.BlockSpec((KV_LORA, 1024), lambda bi, i: (0, 0))
    wq_spec = pl.BlockSpec((GROUP_W, 1024), lambda bi, i: (0, 0))
    hd = lambda w: pl.BlockSpec((None, MLA_HEADS, tm, w), lambda bi, i: (bi, 0, i, 0))
    k_sh = jax.ShapeDtypeStruct((b, MLA_HEADS, l, 256), BF16)
    v_sh = jax.ShapeDtypeStruct((b, MLA_HEADS, l, 128), BF16)
    if with_q:
        return _pc(_mla_prep_kernel, (k_sh, v_sh, k_sh), (b, l // tm),
                   [kv_spec, cq_spec, cs_spec, row(KV_LORA), row(GROUP_W), row(128), row(128), row(128), row(128),
                    wkv_spec, wq_spec],
                   (hd(256), hd(128), hd(256)), name="mla_prep")(
            proj, proj, cs, p["gkv"], p["gqa"], p["gkn"], p["gkt"], p["gqn"], p["gqt"], p["wkv"], p["wq"])
    k, v = _pc(_mla_prep_kernel_noq, (k_sh, v_sh), (b, l // tm),
               [kv_spec, cs_spec, row(KV_LORA), row(128), row(128), wkv_spec],
               (hd(256), hd(128)), name="mla_prep_kv")(proj, cs, p["gkv"], p["gkn"], p["gkt"], p["wkv"])
    return k, v, None


def _attn_kernel(*refs, n_seg, n_part, kv_chunk):
    q_ref = refs[0]
    o_ref = refs[1 + 2 * n_seg]
    rows = q_ref.shape[0] // n_part
    for part in range(n_part):
        sl = slice(part * rows, (part + 1) * rows)
        q = q_ref[sl, :]
        m = jnp.full((rows, 1), -jnp.inf, F32)
        den = jnp.zeros((rows, 1), F32)
        acc = jnp.zeros((rows, o_ref.shape[1]), F32)
        for i in range(n_seg):
            k_ref, v_ref = refs[1 + 2 * i], refs[2 + 2 * i]
            ch = min(kv_chunk, k_ref.shape[0])
            for c0 in range(0, k_ref.shape[0], ch):
                s = lax.dot_general(q, k_ref[c0:c0 + ch, :], (((1,), (1,)), ((), ())), preferred_element_type=F32)
                m_new = jnp.maximum(m, s.max(axis=-1, keepdims=True))
                alpha = jnp.exp2(m - m_new)
                p = jnp.exp2(s - m_new)
                den = alpha * den + p.sum(axis=-1, keepdims=True)
                acc = alpha * acc + jnp.dot(p.astype(BF16), v_ref[c0:c0 + ch, :], preferred_element_type=F32)
                m = m_new
        o_ref[sl, :] = (acc / den).astype(o_ref.dtype)


def attention(q, kvs, tq=1024, part_rows=512, kv_chunk=4096):
    b, h, l, _ = q.shape
    tq = _tile(l, tq)
    n_part = max(1, tq // part_rows)
    assert all(k.shape[2] % min(kv_chunk, k.shape[2]) == 0 for k, _ in kvs)
    specs = [pl.BlockSpec((None, None, tq, 256), lambda bi, hi, i: (bi, hi, i, 0))]
    args = [q]
    for k, v in kvs:
        lk = k.shape[2]
        specs += [pl.BlockSpec((None, None, lk, 256), lambda bi, hi, i: (bi, hi, 0, 0)),
                  pl.BlockSpec((None, None, lk, 128), lambda bi, hi, i: (bi, hi, 0, 0))]
        args += [k, v]
    return _pc(functools.partial(_attn_kernel, n_seg=len(kvs), n_part=n_part, kv_chunk=kv_chunk),
               jax.ShapeDtypeStruct((b, l, h * 128), BF16), (b, h, l // tq), specs,
               pl.BlockSpec((None, tq, 128), lambda bi, hi, i: (bi, i, hi)), name="attention")(*args)


def s5_tables(lam_re, lam_im, log_step, b_re, b_im, c_re, c_im):
    t = S5_T
    step = jnp.exp(log_step.astype(F32))[..., None]
    re, im = lam_re.astype(F32) * step, lam_im.astype(F32) * step
    d = jnp.arange(t + 1, dtype=F32)[:, None, None, None]
    mag = jnp.exp(d * re)
    ar, ai = mag * jnp.cos(d * im), mag * jnp.sin(d * im)
    lr, li = lam_re.astype(F32), lam_im.astype(F32)
    den = lr * lr + li * li
    er, ei = ar[1] - 1.0, ai[1]
    cr, ci = (er * lr + ei * li) / den, (ei * lr - er * li) / den
    br, bi = b_re.astype(F32), b_im.astype(F32)
    bbr = cr[..., None] * br - ci[..., None] * bi
    bbi = cr[..., None] * bi + ci[..., None] * br
    ccr, cci = c_re.astype(F32), c_im.astype(F32)
    car = ccr[None] * ar[:, :, :, None, :] - cci[None] * ai[:, :, :, None, :]
    cai = ccr[None] * ai[:, :, :, None, :] + cci[None] * ar[:, :, :, None, :]
    kd = jnp.einsum("dxgnp,xgpm->xgdnm", car[:t], bbr) - jnp.einsum("dxgnp,xgpm->xgdnm", cai[:t], bbi)
    kcat = jnp.stack([kd[0], kd[1][:, ::-1]], axis=1).transpose(0, 1, 4, 2, 3).reshape(S5_G, 2, S5_N, t * S5_N)
    pw = (jnp.arange(t - 1, -1, -1), jnp.arange(t))
    w_re, w_im = [], []
    for x in range(2):
        a_r, a_i = ar[pw[x], x], ai[pw[x], x]
        w_re.append(a_r[..., None] * bbr[x][None] - a_i[..., None] * bbi[x][None])
        w_im.append(a_r[..., None] * bbi[x][None] + a_i[..., None] * bbr[x][None])
    ws = jnp.concatenate(w_re + w_im, axis=2).transpose(1, 0, 3, 2).reshape(S5_G, t * S5_N, 4 * S5_P)
    pv = (jnp.arange(1, t + 1), jnp.arange(t, 0, -1))
    vm = jnp.concatenate([car[pv[0], 0], car[pv[1], 1], -cai[pv[0], 0], -cai[pv[1], 1]], axis=-1)
    vm = vm.transpose(1, 3, 0, 2).reshape(S5_G, 4 * S5_P, t * S5_N)
    dec = jnp.stack([jnp.concatenate([ar[t, 0], ar[t, 1]], axis=-1),
                     jnp.concatenate([ai[t, 0], ai[t, 1]], axis=-1)], axis=1)
    return kcat, ws.astype(BF16), vm.astype(BF16), dec


def _s5a_kernel(u_ref, k_ref, ws_ref, y_ref, s_ref, m_ref):
    kf, kb = k_ref[0], k_ref[1]
    lane = lax.broadcasted_iota(jnp.int32, kf.shape, 1)
    for s in range(S5_T):
        lo, hi = S5_N * s, S5_N * (s + 1)
        f = kf if s == 0 else jnp.where(lane >= lo, pltpu.roll(kf, lo, axis=1), 0.0)
        bk = kb if s == S5_T - 1 else jnp.where(lane < hi, pltpu.roll(kb, hi, axis=1), 0.0)
        m_ref[lo:hi, :] = (f + bk).astype(BF16)
    u = u_ref[...]
    y_ref[...] = jnp.dot(u, m_ref[...], preferred_element_type=F32)
    s_ref[...] = jnp.dot(u, ws_ref[...], preferred_element_type=F32)


def _s5b_kernel(yi_ref, s_ref, x0_ref, dec_ref, vm_ref, y_ref, xf_ref, xa_ref, xb_ref, *, nchunk, nbatch):
    cps = 8 // nbatch
    nslab = nchunk // cps
    ar = dec_ref[0:1, :]
    ai = dec_ref[1:2, :]
    fwd_lane = lax.broadcasted_iota(jnp.int32, (nbatch, 128), 1) < S5_P

    def step(jj, carry):
        xr, xi = carry
        rf = pl.ds(pl.multiple_of(jj * 8, 8), 8)
        rb = pl.ds(pl.multiple_of((nslab - 1 - jj) * 8, 8), 8)
        sf, sb = s_ref[rf, :], s_ref[rb, :]
        seen = []
        for i in range(cps):
            lo, hi = i * nbatch, (i + 1) * nbatch
            ml, mh = (cps - 1 - i) * nbatch, (cps - i) * nbatch
            seen.append(jnp.concatenate([xr, xi], axis=1))
            sr = jnp.where(fwd_lane, sf[lo:hi, :128], sb[ml:mh, :128])
            si = jnp.where(fwd_lane, sf[lo:hi, 128:], sb[ml:mh, 128:])
            xr, xi = ar * xr - ai * xi + sr, ar * xi + ai * xr + si
        xa_ref[rf, :] = jnp.concatenate(seen, axis=0)
        xb_ref[rb, :] = jnp.concatenate(seen[::-1], axis=0)
        return xr, xi

    x0 = x0_ref[0:nbatch, :]
    xr, xi = lax.fori_loop(0, nslab, step, (x0[:, :128], x0[:, 128:]), unroll=2)
    xf_ref[...] = jnp.zeros_like(xf_ref)
    xf_ref[0:nbatch, :] = jnp.concatenate([xr, xi], axis=1)
    pick = (lax.broadcasted_iota(jnp.int32, xa_ref.shape, 1) % 128) < S5_P
    xin = jnp.where(pick, xa_ref[...], xb_ref[...])
    y = yi_ref[...] + jnp.dot(xin.astype(BF16), vm_ref[...], preferred_element_type=F32)
    y_ref[...] = y.astype(y_ref.dtype)


def s5_scan(u, x0, tabs):
    kcat, ws, vm, dec = tabs
    b, l, _ = u.shape
    j = l // S5_T
    r = b * j
    assert 8 % b == 0 and j % (8 // b) == 0
    ug = u.astype(BF16).reshape(b, j, S5_T, S5_G, S5_N).transpose(3, 1, 0, 2, 4).reshape(S5_G, r, GROUP_W)
    g3 = lambda w: pl.BlockSpec((None, r, w), lambda g: (g, 0, 0))
    yi, s = _pc(_s5a_kernel,
                (jax.ShapeDtypeStruct((S5_G, r, GROUP_W), F32), jax.ShapeDtypeStruct((S5_G, r, 256), F32)),
                (S5_G,),
                [g3(GROUP_W), pl.BlockSpec((None, 2, S5_N, GROUP_W), lambda g: (g, 0, 0, 0)),
                 pl.BlockSpec((None, GROUP_W, 256), lambda g: (g, 0, 0))],
                (g3(GROUP_W), g3(256)),
                scratch=[pltpu.VMEM((GROUP_W, GROUP_W), BF16)], name="s5a")(ug, kcat, ws)
    y, xf = _pc(functools.partial(_s5b_kernel, nchunk=j, nbatch=b),
                (jax.ShapeDtypeStruct((S5_G, r, GROUP_W), BF16), jax.ShapeDtypeStruct((S5_G, 8, 256), F32)),
                (S5_G,),
                [g3(GROUP_W), g3(256),
                 pl.BlockSpec((None, 8, 256), lambda g: (g, 0, 0)),
                 pl.BlockSpec((None, 2, 128), lambda g: (g, 0, 0)),
                 pl.BlockSpec((None, 256, GROUP_W), lambda g: (g, 0, 0))],
                (g3(GROUP_W), pl.BlockSpec((None, 8, 256), lambda g: (g, 0, 0))),
                scratch=[pltpu.VMEM((r, 256), F32), pltpu.VMEM((r, 256), F32)], name="s5b")(yi, s, x0, dec, vm)
    y = y.reshape(S5_G, j, b, S5_T, S5_N).transpose(2, 1, 3, 0, 4).reshape(b, l, GROUP_W)
    return y, xf


def _s5_out_kernel(y_ref, u_ref, d_ref, w_ref, b_ref, o_ref):
    y = jax.nn.gelu(y_ref[...].astype(F32) + d_ref[...] * u_ref[...].astype(F32))
    z = jnp.dot(y.astype(BF16), w_ref[...], preferred_element_type=F32) + b_ref[...]
    o_ref[...] = (y * jax.nn.sigmoid(z)).astype(o_ref.dtype)


def s5_out(y, proj, d, w, bias, tm=1024):
    b, l, _ = y.shape
    tm = _tile(l, tm)
    row = pl.BlockSpec((1, GROUP_W), lambda bi, i: (0, 0))
    return _pc(_s5_out_kernel, jax.ShapeDtypeStruct((b, l, GROUP_W), BF16), (b, l // tm),
               [pl.BlockSpec((None, tm, GROUP_W), lambda bi, i: (bi, i, 0)),
                pl.BlockSpec((None, tm, GROUP_W), lambda bi, i: (bi, i, PB_S5U)),
                row, pl.BlockSpec((GROUP_W, GROUP_W), lambda bi, i: (0, 0)), row],
               pl.BlockSpec((None, tm, GROUP_W), lambda bi, i: (bi, i, 0)), name="s5_out")(y, proj, d, w, bias)


def _hgrn_kernel(z_ref, i_ref, q_ref, la_ref, lb_ref, s0_ref, o_ref, sfin_ref, st_ref, *, reverse, nsub):
    c = HG_SUB

    @pl.when(pl.program_id(1) == 0)
    def _():
        st_ref[...] = s0_ref[...]

    hlf = c // 2
    ones = jnp.ones((HG_D, HG_D), BF16)
    row = lax.broadcasted_iota(jnp.int32, (c, HG_D), 0)
    row8 = lax.broadcasted_iota(jnp.int32, (hlf, HG_D), 0)

    def sub(jj, carry):
        j = (nsub - 1 - jj) if reverse else jj
        rows = pl.ds(pl.multiple_of(j * c, c), c)
        for h in range(HG_HEADS):
            sl = slice(HG_D * h, HG_D * (h + 1))
            z = z_ref[rows, sl].astype(F32)
            v = i_ref[rows, sl].astype(F32)
            q = _silu(q_ref[rows, sl].astype(F32))
            a = la_ref[:, sl]
            bb = lb_ref[:, sl] + (jnp.minimum(z, 0.0) - jnp.log1p(jnp.exp(-jnp.abs(z))))
            g = jnp.maximum(a, bb) + jnp.log1p(jnp.exp(-jnp.abs(a - bb)))
            k = 1.0 - jnp.exp(g)
            gc = g
            for sh in (1, 2, 4, 8):
                if reverse:
                    gc = gc + jnp.where(row < c - sh, pltpu.roll(gc, c - sh, axis=0), 0.0)
                else:
                    gc = gc + jnp.where(row >= sh, pltpu.roll(gc, sh, axis=0), 0.0)
            gc = gc * LOG2E
            gk = gc - jnp.log(jnp.maximum(k, 0.0)) * LOG2E
            blocks, meta = [], []
            for s in range(c):
                for half in range(2):
                    r0 = half * hlf
                    if (r0 > s) if reverse else (r0 + hlf - 1 < s):
                        continue
                    e = gc[r0:r0 + hlf, :] - gk[s:s + 1, :]
                    if not ((r0 + hlf - 1 <= s) if reverse else (r0 >= s)):
                        keep = (row8 + r0 <= s) if reverse else (row8 + r0 >= s)
                        e = jnp.where(keep, e, -1e30)
                    blocks.append(q[r0:r0 + hlf, :] * jnp.exp2(e))
                    meta.append((half, s))
            rsum = jnp.dot(jnp.concatenate(blocks, axis=0).astype(BF16), ones, preferred_element_type=F32)
            halves = [jnp.zeros((hlf, HG_D), F32), jnp.zeros((hlf, HG_D), F32)]
            for i, (half, s) in enumerate(meta):
                halves[half] = halves[half] + rsum[hlf * i:hlf * (i + 1), :] * v[s:s + 1, :]
            o = jnp.concatenate(halves, axis=0)
            st = st_ref[h]
            o = o + lax.dot_general((q * jnp.exp2(gc)).astype(BF16), st.astype(BF16),
                                    (((1,), (1,)), ((), ())), preferred_element_type=F32)
            gl = gc[0:1, :] if reverse else gc[c - 1:c, :]
            kd = k * jnp.exp2(gl - gc)
            ds = lax.dot_general(v.astype(BF16), kd.astype(BF16), (((0,), (0,)), ((), ())),
                                 preferred_element_type=F32)
            st_ref[h] = st * jnp.exp2(gl) + ds
            o_ref[rows, sl] = o.astype(o_ref.dtype)
        return carry

    lax.fori_loop(0, nsub, sub, 0, unroll=8)

    @pl.when(pl.program_id(1) == pl.num_programs(1) - 1)
    def _():
        sfin_ref[...] = st_ref[...]


def hgrn_dir(proj, la, lb1, s0, reverse, tl=512):
    b, l, _ = proj.shape
    tl = _tile(l, tl)
    nb = l // tl
    blk = (lambda i: nb - 1 - i) if reverse else (lambda i: i)
    col = lambda c: pl.BlockSpec((None, tl, GROUP_W), lambda bi, i: (bi, blk(i), c))
    row = pl.BlockSpec((1, GROUP_W), lambda bi, i: (0, 0))
    st = pl.BlockSpec((None, HG_HEADS, HG_D, HG_D), lambda bi, i: (bi, 0, 0, 0))
    return _pc(functools.partial(_hgrn_kernel, reverse=reverse, nsub=tl // HG_SUB),
               (jax.ShapeDtypeStruct((b, l, GROUP_W), BF16), jax.ShapeDtypeStruct(s0.shape, F32)),
               (b, nb),
               [col(PB_HGB if reverse else PB_HGF), col(PB_HGI), col(PB_HGQ), row, row, st],
               (pl.BlockSpec((None, tl, GROUP_W), lambda bi, i: (bi, blk(i), 0)), st),
               scratch=[pltpu.VMEM((HG_HEADS, HG_D, HG_D), F32)],
               name="hgrn_bwd" if reverse else "hgrn_fwd")(proj, proj, proj, la, lb1, s0)


def _hg_out_kernel(of_ref, ob_ref, g_ref, gn_ref, y_ref):
    o = of_ref[...].astype(F32) + ob_ref[...].astype(F32)
    gate = _silu(g_ref[...].astype(F32))
    for h in range(HG_HEADS):
        sl = slice(HG_D * h, HG_D * (h + 1))
        oh = o[:, sl]
        y = oh * lax.rsqrt(jnp.mean(oh * oh, axis=-1, keepdims=True) + EPS) * gn_ref[...]
        y_ref[:, sl] = (y * gate[:, sl]).astype(y_ref.dtype)


def hg_out(of, ob, proj, gn, tm=1024):
    b, l, _ = of.shape
    tm = _tile(l, tm)
    t3 = lambda c: pl.BlockSpec((None, tm, GROUP_W), lambda bi, i: (bi, i, c))
    return _pc(_hg_out_kernel, jax.ShapeDtypeStruct((b, l, GROUP_W), BF16), (b, l // tm),
               [t3(0), t3(0), t3(PB_HGG), pl.BlockSpec((1, HG_D), lambda bi, i: (0, 0))],
               t3(0), name="hg_out")(of, ob, proj, gn)


def _fft_factors(n_fft):
    n1 = {8192: 64, 512: 32, 1024: 32, 2048: 32, 4096: 64}[n_fft]
    return n1, n_fft // n1


def _dft_tables(na, nb, ka, mb, inverse):
    n = na * nb
    sgn = 1.0 if inverse else -1.0

    def stacked(phase, scale=1.0):
        ang = sgn * 2.0 * np.pi * (phase % n) / n
        fr, fi = np.cos(ang) * scale, np.sin(ang) * scale
        return np.concatenate([np.concatenate([fr, -fi], axis=-1), np.concatenate([fi, fr], axis=-1)], axis=-2)

    fa = stacked(np.outer(np.arange(na), np.arange(ka)) * nb)
    p, q, b = np.arange(na)[:, None, None], np.arange(mb)[None, :, None], np.arange(nb)[None, None, :]
    fb = stacked(b * p + b * q * na, (1.0 / n) if inverse else 1.0)
    return jnp.asarray(fa, BF16), jnp.asarray(fb, BF16)


def _fft_a_kernel(z_ref, f_ref, o_ref, *, g, na, is_complex):
    f = f_ref[...]
    for i in range(g):
        z = z_ref[:, i] if is_complex else z_ref[i]
        if is_complex:
            z = z.reshape(z.shape[0] * z.shape[1], z.shape[2])
        r = jnp.dot(f, z, preferred_element_type=F32)
        o_ref[0, i] = r[:na]
        o_ref[1, i] = r[na:]


def fft_stage_a(z, fa, g=8):
    is_complex = z.ndim == 4
    nb, ka, pc = z.shape[-3:]
    na = fa.shape[0] // 2
    g = min(g, nb)
    zspec = (pl.BlockSpec((2, g, ka, pc), lambda j: (0, j, 0, 0)) if is_complex
             else pl.BlockSpec((g, ka, pc), lambda j: (j, 0, 0)))
    return _pc(functools.partial(_fft_a_kernel, g=g, na=na, is_complex=is_complex),
               jax.ShapeDtypeStruct((2, nb, na, pc), F32), (nb // g,),
               [zspec, pl.BlockSpec(fa.shape, lambda j: (0, 0))],
               pl.BlockSpec((2, g, na, pc), lambda j: (0, j, 0, 0)), name="fft_a")(z, fa)


def _fft_b_kernel(*refs, mode, mb, kb):
    a_ref, f_ref = refs[0], refs[1]
    o_ref = refs[-1]
    nb = a_ref.shape[1]
    for i in range(kb):
        f = f_ref[i].astype(F32)
        z = (jnp.dot(f[:, :nb], a_ref[0, :, i, :], preferred_element_type=F32)
             + jnp.dot(f[:, nb:], a_ref[1, :, i, :], preferred_element_type=F32))
        zr, zi = z[:mb], z[mb:]
        if mode == "filter":
            h_ref = refs[2]
            hr, hi = h_ref[0, i].astype(F32), h_ref[1, i].astype(F32)
            zr, zi = zr * hr - zi * hi, zr * hi + zi * hr
        elif mode == "final":
            v_ref, g_ref, b_ref = refs[2], refs[3], refs[4]
            zr = g_ref[0, i].astype(F32) * (zr + v_ref[0, i].astype(F32) * b_ref[...])
            zi = g_ref[1, i].astype(F32) * (zi + v_ref[1, i].astype(F32) * b_ref[...])
        o_ref[0, i] = zr.astype(o_ref.dtype)
        o_ref[1, i] = zi.astype(o_ref.dtype)


def fft_stage_b(a, fb, mode, extra=(), order=0, kb=16):
    _, nb, na, pc = a.shape
    mb = fb.shape[1] // 2
    c = GROUP_W
    kb = min(kb, na)
    blk = pl.BlockSpec((2, kb, mb, c), lambda k, p: (0, k, 0, p))
    specs = [pl.BlockSpec((2, nb, kb, c), lambda k, p: (0, 0, k, p)),
             pl.BlockSpec((kb, 2 * mb, 2 * nb), lambda k, p: (k, 0, 0))]
    args = [a, fb]
    if mode == "filter":
        (h,) = extra
        specs.append(pl.BlockSpec((2, kb, mb, c), lambda k, p: (0, k, 0, order)))
        args.append(h)
    elif mode == "final":
        v, gate, bias = extra
        specs += [blk, blk, pl.BlockSpec((1, c), lambda k, p: (0, 0))]
        args += [v, gate, bias]
    return _pc(functools.partial(_fft_b_kernel, mode=mode, mb=mb, kb=kb),
               jax.ShapeDtypeStruct((2, na, mb, pc), BF16), (na // kb, pc // c), specs, blk,
               name="fft_b_" + mode)(*args)


def _hy_filter_kernel(emb_ref, w1_ref, b1_ref, w2_ref, b2_ref, w3_ref, win_ref, o_ref, h_ref, *, na):
    hp = lax.Precision.HIGHEST

    @pl.when(pl.program_id(0) == 0)
    def _():
        rc = min(512, h_ref.shape[0])

        def chunk(i, carry):
            rows = pl.ds(pl.multiple_of(i * rc, rc), rc)
            h = jnp.sin(jnp.dot(emb_ref[rows, :], w1_ref[...], precision=hp, preferred_element_type=F32)
                        + b1_ref[...])
            h = jnp.sin(jnp.dot(h, w2_ref[...], precision=hp, preferred_element_type=F32) + b2_ref[...])
            row = lax.broadcasted_iota(jnp.int32, h.shape, 0) + i * rc
            causal = (row & (na - 1)) < na // 2
            h_ref[rows, :] = jnp.concatenate([jnp.where(causal, h, 0.0), jnp.where(causal, 0.0, h)],
                                             axis=1).astype(BF16)
            return carry

        lax.fori_loop(0, h_ref.shape[0] // rc, chunk, 0)

    taps = jnp.dot(h_ref[...], w3_ref[...], preferred_element_type=F32) * win_ref[...]
    o_ref[...] = (taps / jnp.sum(jnp.abs(taps), axis=0, keepdims=True)).astype(o_ref.dtype)


def hyena_filters(n, w1, b1, w2, b2, w3):
    na, nb = _fft_factors(2 * n)
    t = np.arange(n, dtype=np.float32)
    t_norm = t / max(n - 1, 1)
    bands = np.linspace(1e-4, HY_BANDS - 1, HY_BANDS, dtype=np.float32)
    ang = (2.0 * math.pi * t / n)[:, None] * bands[None, :]
    emb = np.concatenate([t_norm[:, None], np.cos(ang), -np.sin(ang)], axis=-1).astype(np.float32)
    emb = np.pad(emb, ((0, 0), (0, LANES - HY_EMB)))
    deltas = np.linspace(math.log(HY_TARGET) / HY_SLOW, math.log(HY_TARGET) / HY_FAST, GROUP_W, dtype=np.float32)
    win = (np.exp(-t_norm[:, None] * np.abs(deltas)[None, :]) + np.float32(HY_SHIFT)).astype(np.float32)
    back = np.concatenate([[0], np.arange(n - 2, -1, -1)])
    win_b = win[back]
    win_b[0] = 0.0
    order = (np.arange(na)[None, :] * nb + np.arange(nb)[:, None]).reshape(-1)
    emb2 = jnp.asarray(np.concatenate([emb, emb[back]], axis=0)[order])
    win2 = jnp.asarray(np.concatenate([win, win_b], axis=0)[order])
    w1p = jnp.pad(w1.astype(F32), ((0, LANES - HY_EMB), (0, 0)))
    w3r = w3.astype(F32).reshape(HY_HIDDEN, 2, 2, GROUP_W)
    w3c = jnp.concatenate([w3r[:, :, 0].reshape(HY_HIDDEN, 2 * GROUP_W),
                           w3r[:, :, 1].reshape(HY_HIDDEN, 2 * GROUP_W)], axis=0).astype(BF16)
    tc = LANES
    full = lambda a: pl.BlockSpec(a.shape, lambda j: (0, 0))
    b1r, b2r = b1.reshape(1, -1).astype(F32), b2.reshape(1, -1).astype(F32)
    w2f = w2.astype(F32)
    return _pc(functools.partial(_hy_filter_kernel, na=na),
               jax.ShapeDtypeStruct((2 * n, 2 * GROUP_W), BF16), (2 * GROUP_W // tc,),
               [full(emb2), full(w1p), full(b1r), full(w2f), full(b2r),
                pl.BlockSpec((2 * HY_HIDDEN, tc), lambda j: (0, j)),
                pl.BlockSpec((2 * n, tc), lambda j: (0, j % (GROUP_W // tc)))],
               pl.BlockSpec((2 * n, tc), lambda j: (0, j)),
               scratch=[pltpu.VMEM((2 * n, 2 * HY_HIDDEN), BF16)], name="hy_filter")(
        emb2, w1p, b1r, w2f, b2r, w3c, win2)


def hyena_spectrum(n, w1, b1, w2, b2, w3):
    na, nb = _fft_factors(2 * n)
    circ = hyena_filters(n, w1, b1, w2, b2, w3).reshape(nb, na, 2 * GROUP_W)
    fa, fb = _dft_tables(na, nb, na, nb, inverse=False)
    return fft_stage_b(fft_stage_a(circ, fa[:, :na], g=16), fb, "plain")


def _shortconv_kernel(p_ref, w_ref, b_ref, o_ref):
    p = p_ref[...].astype(F32)
    n = p.shape[0]
    row = lax.broadcasted_iota(jnp.int32, p.shape, 0)
    prev = jnp.where(row >= 1, pltpu.roll(p, 1, axis=0), 0.0)
    nxt = jnp.where(row < n - 1, pltpu.roll(p, n - 1, axis=0), 0.0)
    o_ref[...] = (prev * w_ref[0:1, :] + p * w_ref[1:2, :] + nxt * w_ref[2:3, :] + b_ref[...]).astype(o_ref.dtype)


def hyena_shortconv(proj, w, bias):
    b, n, _ = proj.shape
    return _pc(_shortconv_kernel, jax.ShapeDtypeStruct((3, 2, n, (b // 2) * GROUP_W), BF16), (b, 3),
               [pl.BlockSpec((None, n, GROUP_W), lambda bi, j: (bi, 0, PB_HY + j)),
                pl.BlockSpec((3, GROUP_W), lambda bi, j: (0, j)),
                pl.BlockSpec((1, GROUP_W), lambda bi, j: (0, j))],
               pl.BlockSpec((None, None, n, GROUP_W), lambda bi, j: (j, bi % 2, 0, bi // 2)),
               name="shortconv")(proj, w, bias)


def hyena_mixer(proj, spec, conv_w, conv_b, bias):
    b, n, _ = proj.shape
    na, nb = _fft_factors(2 * n)
    pc = (b // 2) * GROUP_W
    u = hyena_shortconv(proj, conv_w, conv_b).reshape(3, 2, na // 2, nb, pc).swapaxes(2, 3)
    fa, fb = _dft_tables(na, nb, na // 2, nb, inverse=False)
    ga, gb = _dft_tables(nb, na, nb, na // 2, inverse=True)
    z = u[0]
    for o in range(2):
        zf = fft_stage_b(fft_stage_a(z, fa, g=16), fb, "filter", (spec,), order=o)
        z = fft_stage_b(fft_stage_a(zf, ga), gb, "final", (z, u[1 + o], bias[o:o + 1]))
    return z.swapaxes(1, 2).reshape(2, n, pc)


def _rot_cols(w):
    return jnp.concatenate([-w[..., 16:32], w[..., 0:16], -w[..., 48:64], w[..., 32:48]], axis=-1)


def _rot_perm(g):
    return jnp.concatenate([g[16:32], g[0:16], g[48:64], g[32:48]])


def _prep_layer(w_in, w_uq, w_ukv, q_a_g, kv_a_g, q_g, k_g):
    d = w_in.shape[0]
    ckv, krope, rest = w_in[:, :256], w_in[:, 256:320], w_in[:, 320:]
    w_in_p = jnp.concatenate([rest, ckv, krope, _rot_cols(krope), jnp.zeros((d, 128), w_in.dtype)], axis=1)
    wq = w_uq.reshape(GROUP_W, MLA_HEADS, MLA_QK)
    wq = jnp.concatenate([wq, _rot_cols(wq[..., MLA_NOPE:])], axis=-1).reshape(GROUP_W, MLA_HEADS * 256)
    r2 = lambda a: a.reshape(1, -1).astype(F32)
    mla = dict(wkv=w_ukv.astype(BF16), wq=wq.astype(BF16), gkv=r2(kv_a_g), gqa=r2(q_a_g),
               gkn=r2(k_g[:MLA_NOPE]), gkt=r2(jnp.concatenate([k_g[MLA_NOPE:], _rot_perm(k_g[MLA_NOPE:])])),
               gqn=r2(q_g[:MLA_NOPE]), gqt=r2(jnp.concatenate([q_g[MLA_NOPE:], _rot_perm(q_g[MLA_NOPE:])])))
    return w_in_p.astype(BF16), mla


def _rope_table(n_rows):
    row = np.repeat(np.arange(n_rows, dtype=np.float32), GRID_W)
    col = np.tile(np.arange(GRID_W, dtype=np.float32), n_rows)
    n_freq = MLA_ROPE // 4
    inv_freq = (ROPE_BASE ** (-np.arange(n_freq, dtype=np.float32) / n_freq)).astype(np.float32)
    ang_r, ang_c = row[:, None] * inv_freq, col[:, None] * inv_freq
    ang = np.concatenate([ang_r, ang_r, ang_c, ang_c], axis=-1)
    return jnp.asarray(np.concatenate([np.cos(ang), np.sin(ang)], axis=-1), F32)


def kernel(x, c, ctx, c_ctx, w_mod, b_mod, norm1, norm2, w_in, w_out, mla_q_a_norm, mla_kv_a_norm, mla_w_uq, mla_w_ukv, mla_q_norm, mla_k_norm, s5_lam_re, s5_lam_im, s5_log_step, s5_b_re, s5_b_im, s5_c_re, s5_c_im, s5_d, s5_w_glu, s5_b_glu, hg_lower_bounds, hg_o_norm, hy_conv_w, hy_conv_b, hy_w1, hy_b1, hy_w2, hy_b2, hy_w3, hy_bias, mlp_w1, mlp_w2):
    bsz, seq, dm = x.shape
    n_ctx = ctx.shape[1]
    depth = w_mod.shape[0]
    assert bsz % 2 == 0 and bsz <= 8 and seq % GRID_W == 0 and S5_T * S5_N == GROUP_W
    r2 = lambda a: a.reshape(1, -1).astype(F32)

    cs_l = _rope_table(seq // GRID_W)
    cs_c = jnp.concatenate([jnp.ones((n_ctx, MLA_ROPE), F32), jnp.zeros((n_ctx, MLA_ROPE), F32)], axis=-1)
    sm = jax.nn.softmax(hg_lower_bounds.astype(F32), axis=1)
    lower = jnp.clip(jnp.cumsum(sm, axis=1) - sm[:, :1], 0.0, 1.0)
    c8 = jnp.concatenate([c.astype(F32), c_ctx.astype(F32)[None], jnp.zeros((7 - bsz, dm), F32)], axis=0)
    zero_state = jnp.zeros((bsz, HG_HEADS, HG_D, HG_D), F32)
    zero_x = jnp.zeros((S5_G, 8, 256), F32)

    xc = ctx
    for l in range(depth):
        last = l == depth - 1
        mod = modvec(c8, w_mod[l].astype(F32), r2(b_mod[l]))
        mod_l = [mod[:bsz, i * dm:(i + 1) * dm][:, None, :] for i in range(N_MOD)]
        mod_c = [mod[bsz:bsz + 1, i * dm:(i + 1) * dm][:, None, :] for i in range(N_MOD)]
        w_in_p, mla_p = _prep_layer(w_in[l], mla_w_uq[l], mla_w_ukv[l], mla_q_a_norm[l], mla_kv_a_norm[l],
                                    mla_q_norm[l], mla_k_norm[l])
        pl_ = modproj(x, mod_l[0], mod_l[1], r2(norm1[l]), w_in_p, BF16, tm=1024, tn=N_PROJ // 4)
        pc_ = modproj(xc, mod_c[0], mod_c[1], r2(norm1[l]), w_in_p, BF16, tm=1024, tn=N_PROJ // 4)

        k_c, v_c, q_c = mla_prep(pc_, cs_c, mla_p, with_q=not last)
        k_l, v_l, q_l = mla_prep(pl_, cs_l, mla_p, with_q=True)
        y_a = attention(q_l, [(k_l, v_l), (k_c, v_c)])

        tabs = s5_tables(s5_lam_re[l], s5_lam_im[l], s5_log_step[l], s5_b_re[l], s5_b_im[l], s5_c_re[l], s5_c_im[l])
        yc_s5, fin = s5_scan(pc_[..., PB_S5U * GROUP_W:(PB_S5U + 1) * GROUP_W], zero_x, tabs)
        yl_s5, _ = s5_scan(pl_[..., PB_S5U * GROUP_W:(PB_S5U + 1) * GROUP_W], fin, tabs)
        glu_w, glu_b = s5_w_glu[l].astype(BF16), r2(s5_b_glu[l])
        y_b = s5_out(yl_s5, pl_, r2(s5_d[l]), glu_w, glu_b)

        la_f, lb_f = r2(jnp.log(lower[0, l])), r2(jnp.log1p(-lower[0, l]))
        la_b, lb_b = r2(jnp.log(lower[1, l])), r2(jnp.log1p(-lower[1, l]))
        oc_f, s_f = hgrn_dir(pc_, la_f, lb_f, zero_state, reverse=False)
        oc_b, s_b = hgrn_dir(pc_, la_b, lb_b, zero_state, reverse=True)
        ol_f, _ = hgrn_dir(pl_, la_f, lb_f, s_f, reverse=False)
        ol_b, _ = hgrn_dir(pl_, la_b, lb_b, s_b, reverse=True)
        y_c = hg_out(ol_f, ol_b, pl_, r2(hg_o_norm[l]))

        hy_args = (hy_conv_w[l].astype(F32), r2(hy_conv_b[l]), hy_bias[l].astype(F32))
        spec_l = hyena_spectrum(seq, hy_w1[l], hy_b1[l], hy_w2[l], hy_b2[l], hy_w3[l])
        y_d = hyena_mixer(pl_, spec_l, *hy_args)

        w_out_b = w_out[l].astype(BF16)
        w1_b, w2_b = mlp_w1[l].astype(BF16), mlp_w2[l].astype(BF16)
        x = mix_out([y_a, y_b, y_c], y_d, w_out_b, x, mod_l[2])
        h1 = modproj(x, mod_l[3], mod_l[4], r2(norm2[l]), w1_b, BF16, act="relu2", tm=1024, tn=2048)
        x = mlp_out(h1, w2_b, x, mod_l[5])

        if not last:
            y_ac = attention(q_c, [(k_c, v_c)])
            y_bc = s5_out(yc_s5, pc_, r2(s5_d[l]), glu_w, glu_b)
            y_cc = hg_out(oc_f, oc_b, pc_, r2(hg_o_norm[l]))
            spec_c = hyena_spectrum(n_ctx, hy_w1[l], hy_b1[l], hy_w2[l], hy_b2[l], hy_w3[l])
            y_dc = hyena_mixer(pc_, spec_c, *hy_args)
            xc = mix_out([y_ac, y_bc, y_cc], y_dc, w_out_b, xc, mod_c[2])
            h1c = modproj(xc, mod_c[3], mod_c[4], r2(norm2[l]), w1_b, BF16, act="relu2", tm=1024)
            xc = mlp_out(h1c, w2_b, xc, mod_c[5])
    return x
```

```python
import functools
import math

import numpy as np
import jax
import jax.numpy as jnp
from jax import lax
from jax.experimental import pallas as pl
from jax.experimental.pallas import tpu as pltpu

F32 = jnp.float32
BF16 = jnp.bfloat16
EPS = 1e-6

GRID_W = 64
GROUP_W = 512
N_MOD = 6
MLA_NOPE = 128
MLA_ROPE = 64
MLA_QK = MLA_NOPE + MLA_ROPE
MLA_HEADS = 4
KV_LORA = 256
MLA_SCALE = 1.0 / math.sqrt(MLA_QK)
LOG2E = 1.0 / math.log(2.0)
ROPE_BASE = 10000.0
S5_G = 32
S5_N = 16
S5_P = 64
S5_T = 32
HG_HEADS = 4
HG_D = 128
HG_SUB = 16
HY_EMB = 33
HY_BANDS = 16
HY_HIDDEN = 64
HY_TARGET = 1e-2
HY_FAST = 0.3
HY_SLOW = 1.5
HY_SHIFT = 0.05

LANES = 128
VMEM_LIMIT_MB = 56

PB_S5U, PB_HGF, PB_HGB, PB_HGI, PB_CQ, PB_HGQ, PB_HGG, PB_HY, PB_KV = 0, 1, 2, 3, 4, 5, 6, 7, 10
N_PROJ = 11 * GROUP_W


def _pc(kernel, out_shape, grid, in_specs, out_specs, scratch=(), vmem_mb=VMEM_LIMIT_MB, name=None):
    return pl.pallas_call(
        kernel, out_shape=out_shape, grid=grid, in_specs=in_specs, out_specs=out_specs,
        scratch_shapes=list(scratch), name=name,
        compiler_params=pltpu.CompilerParams(
            dimension_semantics=("arbitrary",) * len(grid), vmem_limit_bytes=vmem_mb << 20))


def _tile(n, pref):
    t = min(n, pref)
    while n % t:
        t //= 2
    return t


def _silu(x):
    return x * jax.nn.sigmoid(x)


def _modvec_kernel(c_ref, w_ref, b_ref, o_ref):
    s = _silu(c_ref[...])
    o_ref[...] = jnp.dot(s.astype(BF16), w_ref[...].astype(BF16), preferred_element_type=F32) + b_ref[...]


def modvec(c8, w, b):
    d, n = w.shape
    tn = _tile(n, 1536)
    return _pc(_modvec_kernel, jax.ShapeDtypeStruct((8, n), F32), (n // tn,),
               [pl.BlockSpec((8, d), lambda j: (0, 0)),
                pl.BlockSpec((d, tn), lambda j: (0, j)),
                pl.BlockSpec((1, tn), lambda j: (0, j))],
               pl.BlockSpec((8, tn), lambda j: (0, j)), name="modvec")(c8, w, b)


def _modproj_kernel(x_ref, sh_ref, sc_ref, g_ref, w_ref, o_ref, h_ref, *, act):
    @pl.when(pl.program_id(2) == 0)
    def _():
        x = x_ref[...]
        ms = jnp.mean(x * x, axis=-1, keepdims=True)
        y = x * lax.rsqrt(ms + EPS) * g_ref[...]
        h_ref[...] = (y * (1.0 + sc_ref[...]) + sh_ref[...]).astype(BF16)

    acc = jnp.dot(h_ref[...], w_ref[...], preferred_element_type=F32)
    if act == "relu2":
        acc = jnp.square(jnp.maximum(acc, 0.0))
    o_ref[...] = acc.astype(o_ref.dtype)


def modproj(x, shift, scale, g, w, out_dtype, act=None, tm=512, tn=512):
    b, l, d = x.shape
    n = w.shape[1]
    tm, tn = _tile(l, tm), _tile(n, tn)
    per_b = shift.shape[0] == b
    mod_map = (lambda bi, i, j: (bi, 0, 0)) if per_b else (lambda bi, i, j: (0, 0, 0))
    return _pc(functools.partial(_modproj_kernel, act=act),
               jax.ShapeDtypeStruct((b, l, n), out_dtype), (b, l // tm, n // tn),
               [pl.BlockSpec((None, tm, d), lambda bi, i, j: (bi, i, 0)),
                pl.BlockSpec((None, 1, d), mod_map),
                pl.BlockSpec((None, 1, d), mod_map),
                pl.BlockSpec((1, d), lambda bi, i, j: (0, 0)),
                pl.BlockSpec((d, tn), lambda bi, i, j: (0, j))],
               pl.BlockSpec((None, tm, tn), lambda bi, i, j: (bi, i, j)),
               scratch=[pltpu.VMEM((tm, d), BF16)], name="modproj_" + (act or "lin"))(x, shift, scale, g, w)


def _mmres_kernel(*refs, n_lhs, kw):
    a_refs = refs[:n_lhs]
    w_ref, res_ref, gate_ref, o_ref = refs[n_lhs:]
    acc = None
    for i, a in enumerate(a_refs):
        d = jnp.dot(a[...], w_ref[i * kw:(i + 1) * kw, :], preferred_element_type=F32)
        acc = d if acc is None else acc + d
    o_ref[...] = res_ref[...] + gate_ref[...] * acc


def mix_out(ys, yd_hlay, w, res, gate, tm=1024, tn=1024):
    b, l, kw = ys[0].shape
    n = w.shape[1]
    tm, tn = _tile(l, tm), _tile(n, tn)
    per_b = gate.shape[0] == b
    gmap = (lambda bi, i, j: (bi, 0, j)) if per_b else (lambda bi, i, j: (0, 0, j))
    y_spec = pl.BlockSpec((None, tm, kw), lambda bi, i, j: (bi, i, 0))
    yd_spec = pl.BlockSpec((None, tm, kw), lambda bi, i, j: (bi % 2, i, bi // 2))
    return _pc(functools.partial(_mmres_kernel, n_lhs=4, kw=kw),
               jax.ShapeDtypeStruct(res.shape, F32), (b, l // tm, n // tn),
               [y_spec, y_spec, y_spec, yd_spec,
                pl.BlockSpec((4 * kw, tn), lambda bi, i, j: (0, j)),
                pl.BlockSpec((None, tm, tn), lambda bi, i, j: (bi, i, j)),
                pl.BlockSpec((None, 1, tn), gmap)],
               pl.BlockSpec((None, tm, tn), lambda bi, i, j: (bi, i, j)), name="mix_out")(*ys, yd_hlay, w, res, gate)


def _mmresk_kernel(a_ref, w_ref, res_ref, gate_ref, o_ref, acc_ref):
    k = pl.program_id(3)

    @pl.when(k == 0)
    def _():
        acc_ref[...] = jnp.zeros_like(acc_ref)

    acc_ref[...] += jnp.dot(a_ref[...], w_ref[...], preferred_element_type=F32)

    @pl.when(k == pl.num_programs(3) - 1)
    def _():
        o_ref[...] = res_ref[...] + gate_ref[...] * acc_ref[...]


def mlp_out(a, w, res, gate, tm=1024, tn=1024, tk=2048):
    b, l, kk = a.shape
    n = w.shape[1]
    tm, tn, tk = _tile(l, tm), _tile(n, tn), _tile(kk, tk)
    per_b = gate.shape[0] == b
    gmap = (lambda bi, i, j, k: (bi, 0, j)) if per_b else (lambda bi, i, j, k: (0, 0, j))
    return _pc(_mmresk_kernel, jax.ShapeDtypeStruct(res.shape, F32), (b, l // tm, n // tn, kk // tk),
               [pl.BlockSpec((None, tm, tk), lambda bi, i, j, k: (bi, i, k)),
                pl.BlockSpec((tk, tn), lambda bi, i, j, k: (k, j)),
                pl.BlockSpec((None, tm, tn), lambda bi, i, j, k: (bi, i, j)),
                pl.BlockSpec((None, 1, tn), gmap)],
               pl.BlockSpec((None, tm, tn), lambda bi, i, j, k: (bi, i, j)),
               scratch=[pltpu.VMEM((tm, tn), F32)], name="mlp_out")(a, w, res, gate)


def _mla_prep_kernel(kv_ref, cq_ref, cs_ref, gkv_ref, gqa_ref, gkn_ref, gkt_ref, gqn_ref, gqt_ref,
                     wkv_ref, wq_ref, k_ref, v_ref, q_ref):
    lane = lax.broadcasted_iota(jnp.int32, (1, LANES), 1)
    low = (lane < MLA_ROPE).astype(F32)
    cs = cs_ref[...]
    kv = kv_ref[...].astype(F32)
    ckv = kv[:, :KV_LORA]
    ckvn = ckv * lax.rsqrt(jnp.mean(ckv * ckv, axis=-1, keepdims=True) + EPS) * gkv_ref[...]
    kvp = jnp.dot(ckvn.astype(BF16), wkv_ref[...], preferred_element_type=F32)
    pe2 = kv[:, KV_LORA:KV_LORA + LANES]
    ss_pe = jnp.sum(pe2 * pe2 * low, axis=-1, keepdims=True)
    pe_tab = cs * gkt_ref[...]
    for h in range(MLA_HEADS):
        knope = kvp[:, 256 * h:256 * h + 128]
        rs = lax.rsqrt((jnp.sum(knope * knope, axis=-1, keepdims=True) + ss_pe) * (1.0 / MLA_QK) + EPS)
        t = pe2 * pe_tab * rs
        t = (t + pltpu.roll(t, MLA_ROPE, axis=1)) * low
        k_ref[h, :, 0:128] = (knope * gkn_ref[...] * rs).astype(BF16)
        k_ref[h, :, 128:256] = t.astype(BF16)
        v_ref[h] = kvp[:, 256 * h + 128:256 * h + 256].astype(BF16)
    if q_ref is not None:
        cq = cq_ref[...].astype(F32)
        cqn = cq * lax.rsqrt(jnp.mean(cq * cq, axis=-1, keepdims=True) + EPS) * gqa_ref[...]
        qp = jnp.dot(cqn.astype(BF16), wq_ref[...], preferred_element_type=F32)
        q_tab = cs * gqt_ref[...]
        for h in range(MLA_HEADS):
            qn = qp[:, 256 * h:256 * h + 128]
            qe = qp[:, 256 * h + 128:256 * h + 256]
            ss = jnp.sum(qn * qn + qe * qe * low, axis=-1, keepdims=True)
            rs = lax.rsqrt(ss * (1.0 / MLA_QK) + EPS) * (MLA_SCALE * LOG2E)
            t = qe * q_tab * rs
            t = t + pltpu.roll(t, MLA_ROPE, axis=1)
            q_ref[h, :, 0:128] = (qn * gqn_ref[...] * rs).astype(BF16)
            q_ref[h, :, 128:256] = t.astype(BF16)


def _mla_prep_kernel_noq(kv_ref, cs_ref, gkv_ref, gkn_ref, gkt_ref, wkv_ref, k_ref, v_ref):
    _mla_prep_kernel(kv_ref, None, cs_ref, gkv_ref, None, gkn_ref, gkt_ref, None, None, wkv_ref, None,
                     k_ref, v_ref, None)


def mla_prep(proj, cs, p, with_q, tm=1024):
    b, l, _ = proj.shape
    tm = _tile(l, tm)
    row = lambda w: pl.BlockSpec((1, w), lambda bi, i: (0, 0))
    kv_spec = pl.BlockSpec((None, tm, GROUP_W), lambda bi, i: (bi, i, PB_KV))
    cq_spec = pl.BlockSpec((None, tm, GROUP_W), lambda bi, i: (bi, i, PB_CQ))
    cs_spec = pl.BlockSpec((tm, LANES), lambda bi, i: (i, 0))
    wkv_spec = pl.BlockSpec((KV_LORA, 1024), lambda bi, i: (0, 0))
    wq_spec = pl.BlockSpec((GROUP_W, 1024), lambda bi, i: (0, 0))
    hd = lambda w: pl.BlockSpec((None, MLA_HEADS, tm, w), lambda bi, i: (bi, 0, i, 0))
    k_sh = jax.ShapeDtypeStruct((b, MLA_HEADS, l, 256), BF16)
    v_sh = jax.ShapeDtypeStruct((b, MLA_HEADS, l, 128), BF16)
    if with_q:
        return _pc(_mla_prep_kernel, (k_sh, v_sh, k_sh), (b, l // tm),
                   [kv_spec, cq_spec, cs_spec, row(KV_LORA), row(GROUP_W), row(128), row(128), row(128), row(128),
                    wkv_spec, wq_spec],
                   (hd(256), hd(128), hd(256)), name="mla_prep")(
            proj, proj, cs, p["gkv"], p["gqa"], p["gkn"], p["gkt"], p["gqn"], p["gqt"], p["wkv"], p["wq"])
    k, v = _pc(_mla_prep_kernel_noq, (k_sh, v_sh), (b, l // tm),
               [kv_spec, cs_spec, row(KV_LORA), row(128), row(128), wkv_spec],
               (hd(256), hd(128)), name="mla_prep_kv")(proj, cs, p["gkv"], p["gkn"], p["gkt"], p["wkv"])
    return k, v, None


def _attn_kernel(*refs, n_seg, n_part, kv_chunk):
    q_ref = refs[0]
    o_ref = refs[1 + 2 * n_seg]
    rows = q_ref.shape[0] // n_part
    for part in range(n_part):
        sl = slice(part * rows, (part + 1) * rows)
        q = q_ref[sl, :]
        m = jnp.full((rows, 1), -jnp.inf, F32)
        den = jnp.zeros((rows, 1), F32)
        acc = jnp.zeros((rows, o_ref.shape[1]), F32)
        for i in range(n_seg):
            k_ref, v_ref = refs[1 + 2 * i], refs[2 + 2 * i]
            ch = min(kv_chunk, k_ref.shape[0])
            for c0 in range(0, k_ref.shape[0], ch):
                s = lax.dot_general(q, k_ref[c0:c0 + ch, :], (((1,), (1,)), ((), ())), preferred_element_type=F32)
                m_new = jnp.maximum(m, s.max(axis=-1, keepdims=True))
                alpha = jnp.exp2(m - m_new)
                p = jnp.exp2(s - m_new)
                den = alpha * den + p.sum(axis=-1, keepdims=True)
                acc = alpha * acc + jnp.dot(p.astype(BF16), v_ref[c0:c0 + ch, :], preferred_element_type=F32)
                m = m_new
        o_ref[sl, :] = (acc / den).astype(o_ref.dtype)


def attention(q, kvs, tq=1024, part_rows=512, kv_chunk=4096):
    b, h, l, _ = q.shape
    tq = _tile(l, tq)
    n_part = max(1, tq // part_rows)
    assert all(k.shape[2] % min(kv_chunk, k.shape[2]) == 0 for k, _ in kvs)
    specs = [pl.BlockSpec((None, None, tq, 256), lambda bi, hi, i: (bi, hi, i, 0))]
    args = [q]
    for k, v in kvs:
        lk = k.shape[2]
        specs += [pl.BlockSpec((None, None, lk, 256), lambda bi, hi, i: (bi, hi, 0, 0)),
                  pl.BlockSpec((None, None, lk, 128), lambda bi, hi, i: (bi, hi, 0, 0))]
        args += [k, v]
    return _pc(functools.partial(_attn_kernel, n_seg=len(kvs), n_part=n_part, kv_chunk=kv_chunk),
               jax.ShapeDtypeStruct((b, l, h * 128), BF16), (b, h, l // tq), specs,
               pl.BlockSpec((None, tq, 128), lambda bi, hi, i: (bi, i, hi)), name="attention")(*args)


def s5_tables(lam_re, lam_im, log_step, b_re, b_im, c_re, c_im):
    t = S5_T
    step = jnp.exp(log_step.astype(F32))[..., None]
    re, im = lam_re.astype(F32) * step, lam_im.astype(F32) * step
    d = jnp.arange(t + 1, dtype=F32)[:, None, None, None]
    mag = jnp.exp(d * re)
    ar, ai = mag * jnp.cos(d * im), mag * jnp.sin(d * im)
    lr, li = lam_re.astype(F32), lam_im.astype(F32)
    den = lr * lr + li * li
    er, ei = ar[1] - 1.0, ai[1]
    cr, ci = (er * lr + ei * li) / den, (ei * lr - er * li) / den
    br, bi = b_re.astype(F32), b_im.astype(F32)
    bbr = cr[..., None] * br - ci[..., None] * bi
    bbi = cr[..., None] * bi + ci[..., None] * br
    ccr, cci = c_re.astype(F32), c_im.astype(F32)
    car = ccr[None] * ar[:, :, :, None, :] - cci[None] * ai[:, :, :, None, :]
    cai = ccr[None] * ai[:, :, :, None, :] + cci[None] * ar[:, :, :, None, :]
    kd = jnp.einsum("dxgnp,xgpm->xgdnm", car[:t], bbr) - jnp.einsum("dxgnp,xgpm->xgdnm", cai[:t], bbi)
    kcat = jnp.stack([kd[0], kd[1][:, ::-1]], axis=1).transpose(0, 1, 4, 2, 3).reshape(S5_G, 2, S5_N, t * S5_N)
    pw = (jnp.arange(t - 1, -1, -1), jnp.arange(t))
    w_re, w_im = [], []
    for x in range(2):
        a_r, a_i = ar[pw[x], x], ai[pw[x], x]
        w_re.append(a_r[..., None] * bbr[x][None] - a_i[..., None] * bbi[x][None])
        w_im.append(a_r[..., None] * bbi[x][None] + a_i[..., None] * bbr[x][None])
    ws = jnp.concatenate(w_re + w_im, axis=2).transpose(1, 0, 3, 2).reshape(S5_G, t * S5_N, 4 * S5_P)
    pv = (jnp.arange(1, t + 1), jnp.arange(t, 0, -1))
    vm = jnp.concatenate([car[pv[0], 0], car[pv[1], 1], -cai[pv[0], 0], -cai[pv[1], 1]], axis=-1)
    vm = vm.transpose(1, 3, 0, 2).reshape(S5_G, 4 * S5_P, t * S5_N)
    dec = jnp.stack([jnp.concatenate([ar[t, 0], ar[t, 1]], axis=-1),
                     jnp.concatenate([ai[t, 0], ai[t, 1]], axis=-1)], axis=1)
    return kcat, ws.astype(BF16), vm.astype(BF16), dec


def _s5a_kernel(u_ref, k_ref, ws_ref, y_ref, s_ref, m_ref):
    kf, kb = k_ref[0], k_ref[1]
    lane = lax.broadcasted_iota(jnp.int32, kf.shape, 1)
    for s in range(S5_T):
        lo, hi = S5_N * s, S5_N * (s + 1)
        f = kf if s == 0 else jnp.where(lane >= lo, pltpu.roll(kf, lo, axis=1), 0.0)
        bk = kb if s == S5_T - 1 else jnp.where(lane < hi, pltpu.roll(kb, hi, axis=1), 0.0)
        m_ref[lo:hi, :] = (f + bk).astype(BF16)
    u = u_ref[...]
    y_ref[...] = jnp.dot(u, m_ref[...], preferred_element_type=F32)
    s_ref[...] = jnp.dot(u, ws_ref[...], preferred_element_type=F32)


def _s5b_kernel(yi_ref, s_ref, x0_ref, dec_ref, vm_ref, y_ref, xf_ref, xa_ref, xb_ref, *, nchunk, nbatch):
    cps = 8 // nbatch
    nslab = nchunk // cps
    ar = dec_ref[0:1, :]
    ai = dec_ref[1:2, :]
    fwd_lane = lax.broadcasted_iota(jnp.int32, (nbatch, 128), 1) < S5_P

    def step(jj, carry):
        xr, xi = carry
        rf = pl.ds(pl.multiple_of(jj * 8, 8), 8)
        rb = pl.ds(pl.multiple_of((nslab - 1 - jj) * 8, 8), 8)
        sf, sb = s_ref[rf, :], s_ref[rb, :]
        seen = []
        for i in range(cps):
            lo, hi = i * nbatch, (i + 1) * nbatch
            ml, mh = (cps - 1 - i) * nbatch, (cps - i) * nbatch
            seen.append(jnp.concatenate([xr, xi], axis=1))
            sr = jnp.where(fwd_lane, sf[lo:hi, :128], sb[ml:mh, :128])
            si = jnp.where(fwd_lane, sf[lo:hi, 128:], sb[ml:mh, 128:])
            xr, xi = ar * xr - ai * xi + sr, ar * xi + ai * xr + si
        xa_ref[rf, :] = jnp.concatenate(seen, axis=0)
        xb_ref[rb, :] = jnp.concatenate(seen[::-1], axis=0)
        return xr, xi

    x0 = x0_ref[0:nbatch, :]
    xr, xi = lax.fori_loop(0, nslab, step, (x0[:, :128], x0[:, 128:]), unroll=2)
    xf_ref[...] = jnp.zeros_like(xf_ref)
    xf_ref[0:nbatch, :] = jnp.concatenate([xr, xi], axis=1)
    pick = (lax.broadcasted_iota(jnp.int32, xa_ref.shape, 1) % 128) < S5_P
    xin = jnp.where(pick, xa_ref[...], xb_ref[...])
    y = yi_ref[...] + jnp.dot(xin.astype(BF16), vm_ref[...], preferred_element_type=F32)
    y_ref[...] = y.astype(y_ref.dtype)


def s5_scan(u, x0, tabs):
    kcat, ws, vm, dec = tabs
    b, l, _ = u.shape
    j = l // S5_T
    r = b * j
    assert 8 % b == 0 and j % (8 // b) == 0
    ug = u.astype(BF16).reshape(b, j, S5_T, S5_G, S5_N).transpose(3, 1, 0, 2, 4).reshape(S5_G, r, GROUP_W)
    g3 = lambda w: pl.BlockSpec((None, r, w), lambda g: (g, 0, 0))

    def fused(u_ref, k_ref, ws_ref, x0_ref, dec_ref, vm_ref, y_ref, xf_ref, m_ref, yi_ref, s_ref, xa_ref, xb_ref):
        _s5a_kernel(u_ref, k_ref, ws_ref, yi_ref, s_ref, m_ref)
        _s5b_kernel(yi_ref, s_ref, x0_ref, dec_ref, vm_ref, y_ref, xf_ref, xa_ref, xb_ref, nchunk=j, nbatch=b)

    y, xf = _pc(fused,
                (jax.ShapeDtypeStruct((S5_G, r, GROUP_W), BF16), jax.ShapeDtypeStruct((S5_G, 8, 256), F32)),
                (S5_G,),
                [g3(GROUP_W), pl.BlockSpec((None, 2, S5_N, GROUP_W), lambda g: (g, 0, 0, 0)),
                 pl.BlockSpec((None, GROUP_W, 256), lambda g: (g, 0, 0)),
                 pl.BlockSpec((None, 8, 256), lambda g: (g, 0, 0)),
                 pl.BlockSpec((None, 2, 128), lambda g: (g, 0, 0)),
                 pl.BlockSpec((None, 256, GROUP_W), lambda g: (g, 0, 0))],
                (g3(GROUP_W), pl.BlockSpec((None, 8, 256), lambda g: (g, 0, 0))),
                scratch=[pltpu.VMEM((GROUP_W, GROUP_W), BF16), pltpu.VMEM((r, GROUP_W), F32),
                         pltpu.VMEM((r, 256), F32), pltpu.VMEM((r, 256), F32), pltpu.VMEM((r, 256), F32)],
                name="s5_scan")(ug, kcat, ws, x0, dec, vm)
    y = y.reshape(S5_G, j, b, S5_T, S5_N).transpose(2, 1, 3, 0, 4).reshape(b, l, GROUP_W)
    return y, xf


def _s5_out_kernel(y_ref, u_ref, d_ref, w_ref, b_ref, o_ref):
    y = jax.nn.gelu(y_ref[...].astype(F32) + d_ref[...] * u_ref[...].astype(F32))
    z = jnp.dot(y.astype(BF16), w_ref[...], preferred_element_type=F32) + b_ref[...]
    o_ref[...] = (y * jax.nn.sigmoid(z)).astype(o_ref.dtype)


def s5_out(y, proj, d, w, bias, tm=1024):
    b, l, _ = y.shape
    tm = _tile(l, tm)
    row = pl.BlockSpec((1, GROUP_W), lambda bi, i: (0, 0))
    return _pc(_s5_out_kernel, jax.ShapeDtypeStruct((b, l, GROUP_W), BF16), (b, l // tm),
               [pl.BlockSpec((None, tm, GROUP_W), lambda bi, i: (bi, i, 0)),
                pl.BlockSpec((None, tm, GROUP_W), lambda bi, i: (bi, i, PB_S5U)),
                row, pl.BlockSpec((GROUP_W, GROUP_W), lambda bi, i: (0, 0)), row],
               pl.BlockSpec((None, tm, GROUP_W), lambda bi, i: (bi, i, 0)), name="s5_out")(y, proj, d, w, bias)


def _hgrn_kernel(z_ref, i_ref, q_ref, la_ref, lb_ref, s0_ref, o_ref, sfin_ref, st_ref, *, reverse, nsub):
    c = HG_SUB

    @pl.when(pl.program_id(1) == 0)
    def _():
        st_ref[...] = s0_ref[...]

    hlf = c // 2
    ones = jnp.ones((HG_D, HG_D), BF16)
    row = lax.broadcasted_iota(jnp.int32, (c, HG_D), 0)
    row8 = lax.broadcasted_iota(jnp.int32, (hlf, HG_D), 0)

    def sub(jj, carry):
        j = (nsub - 1 - jj) if reverse else jj
        rows = pl.ds(pl.multiple_of(j * c, c), c)
        for h in range(HG_HEADS):
            sl = slice(HG_D * h, HG_D * (h + 1))
            z = z_ref[rows, sl].astype(F32)
            v = i_ref[rows, sl].astype(F32)
            q = _silu(q_ref[rows, sl].astype(F32))
            a = la_ref[:, sl]
            bb = lb_ref[:, sl] + (jnp.minimum(z, 0.0) - jnp.log1p(jnp.exp(-jnp.abs(z))))
            g = jnp.maximum(a, bb) + jnp.log1p(jnp.exp(-jnp.abs(a - bb)))
            k = 1.0 - jnp.exp(g)
            gc = g
            for sh in (1, 2, 4, 8):
                if reverse:
                    gc = gc + jnp.where(row < c - sh, pltpu.roll(gc, c - sh, axis=0), 0.0)
                else:
                    gc = gc + jnp.where(row >= sh, pltpu.roll(gc, sh, axis=0), 0.0)
            gc = gc * LOG2E
            gk = gc - jnp.log(jnp.maximum(k, 0.0)) * LOG2E
            blocks, meta = [], []
            for s in range(c):
                for half in range(2):
                    r0 = half * hlf
                    if (r0 > s) if reverse else (r0 + hlf - 1 < s):
                        continue
                    e = gc[r0:r0 + hlf, :] - gk[s:s + 1, :]
                    if not ((r0 + hlf - 1 <= s) if reverse else (r0 >= s)):
                        keep = (row8 + r0 <= s) if reverse else (row8 + r0 >= s)
                        e = jnp.where(keep, e, -1e30)
                    blocks.append(q[r0:r0 + hlf, :] * jnp.exp2(e))
                    meta.append((half, s))
            rsum = jnp.dot(jnp.concatenate(blocks, axis=0).astype(BF16), ones, preferred_element_type=F32)
            halves = [jnp.zeros((hlf, HG_D), F32), jnp.zeros((hlf, HG_D), F32)]
            for i, (half, s) in enumerate(meta):
                halves[half] = halves[half] + rsum[hlf * i:hlf * (i + 1), :] * v[s:s + 1, :]
            o = jnp.concatenate(halves, axis=0)
            st = st_ref[h]
            o = o + lax.dot_general((q * jnp.exp2(gc)).astype(BF16), st.astype(BF16),
                                    (((1,), (1,)), ((), ())), preferred_element_type=F32)
            gl = gc[0:1, :] if reverse else gc[c - 1:c, :]
            kd = k * jnp.exp2(gl - gc)
            ds = lax.dot_general(v.astype(BF16), kd.astype(BF16), (((0,), (0,)), ((), ())),
                                 preferred_element_type=F32)
            st_ref[h] = st * jnp.exp2(gl) + ds
            o_ref[rows, sl] = o.astype(o_ref.dtype)
        return carry

    lax.fori_loop(0, nsub, sub, 0, unroll=8)

    @pl.when(pl.program_id(1) == pl.num_programs(1) - 1)
    def _():
        sfin_ref[...] = st_ref[...]


def hgrn_dir(proj, la, lb1, s0, reverse, tl=512):
    b, l, _ = proj.shape
    tl = _tile(l, tl)
    nb = l // tl
    blk = (lambda i: nb - 1 - i) if reverse else (lambda i: i)
    col = lambda c: pl.BlockSpec((None, tl, GROUP_W), lambda bi, i: (bi, blk(i), c))
    row = pl.BlockSpec((1, GROUP_W), lambda bi, i: (0, 0))
    st = pl.BlockSpec((None, HG_HEADS, HG_D, HG_D), lambda bi, i: (bi, 0, 0, 0))
    return _pc(functools.partial(_hgrn_kernel, reverse=reverse, nsub=tl // HG_SUB),
               (jax.ShapeDtypeStruct((b, l, GROUP_W), BF16), jax.ShapeDtypeStruct(s0.shape, F32)),
               (b, nb),
               [col(PB_HGB if reverse else PB_HGF), col(PB_HGI), col(PB_HGQ), row, row, st],
               (pl.BlockSpec((None, tl, GROUP_W), lambda bi, i: (bi, blk(i), 0)), st),
               scratch=[pltpu.VMEM((HG_HEADS, HG_D, HG_D), F32)],
               name="hgrn_bwd" if reverse else "hgrn_fwd")(proj, proj, proj, la, lb1, s0)


def _hg_out_kernel(of_ref, ob_ref, g_ref, gn_ref, y_ref):
    o = of_ref[...].astype(F32) + ob_ref[...].astype(F32)
    gate = _silu(g_ref[...].astype(F32))
    for h in range(HG_HEADS):
        sl = slice(HG_D * h, HG_D * (h + 1))
        oh = o[:, sl]
        y = oh * lax.rsqrt(jnp.mean(oh * oh, axis=-1, keepdims=True) + EPS) * gn_ref[...]
        y_ref[:, sl] = (y * gate[:, sl]).astype(y_ref.dtype)


def hg_out(of, ob, proj, gn, tm=1024):
    b, l, _ = of.shape
    tm = _tile(l, tm)
    t3 = lambda c: pl.BlockSpec((None, tm, GROUP_W), lambda bi, i: (bi, i, c))
    return _pc(_hg_out_kernel, jax.ShapeDtypeStruct((b, l, GROUP_W), BF16), (b, l // tm),
               [t3(0), t3(0), t3(PB_HGG), pl.BlockSpec((1, HG_D), lambda bi, i: (0, 0))],
               t3(0), name="hg_out")(of, ob, proj, gn)


def _fft_factors(n_fft):
    n1 = {8192: 64, 512: 32, 1024: 32, 2048: 32, 4096: 64}[n_fft]
    return n1, n_fft // n1


def _dft_tables(na, nb, ka, mb, inverse):
    n = na * nb
    sgn = 1.0 if inverse else -1.0

    def stacked(phase, scale=1.0):
        ang = sgn * 2.0 * np.pi * (phase % n) / n
        fr, fi = np.cos(ang) * scale, np.sin(ang) * scale
        return np.concatenate([np.concatenate([fr, -fi], axis=-1), np.concatenate([fi, fr], axis=-1)], axis=-2)

    fa = stacked(np.outer(np.arange(na), np.arange(ka)) * nb)
    p, q, b = np.arange(na)[:, None, None], np.arange(mb)[None, :, None], np.arange(nb)[None, None, :]
    fb = stacked(b * p + b * q * na, (1.0 / n) if inverse else 1.0)
    return jnp.asarray(fa, BF16), jnp.asarray(fb, BF16)


def _fft_a_kernel(z_ref, f_ref, o_ref, *, g, na, is_complex):
    f = f_ref[...]
    for i in range(g):
        z = z_ref[:, i] if is_complex else z_ref[i]
        if is_complex:
            z = z.reshape(z.shape[0] * z.shape[1], z.shape[2])
        r = jnp.dot(f, z, preferred_element_type=F32)
        o_ref[0, i] = r[:na]
        o_ref[1, i] = r[na:]


def fft_stage_a(z, fa, g=8):
    is_complex = z.ndim == 4
    nb, ka, pc = z.shape[-3:]
    na = fa.shape[0] // 2
    g = min(g, nb)
    zspec = (pl.BlockSpec((2, g, ka, pc), lambda j: (0, j, 0, 0)) if is_complex
             else pl.BlockSpec((g, ka, pc), lambda j: (j, 0, 0)))
    return _pc(functools.partial(_fft_a_kernel, g=g, na=na, is_complex=is_complex),
               jax.ShapeDtypeStruct((2, nb, na, pc), F32), (nb // g,),
               [zspec, pl.BlockSpec(fa.shape, lambda j: (0, 0))],
               pl.BlockSpec((2, g, na, pc), lambda j: (0, j, 0, 0)), name="fft_a")(z, fa)


def _fft_b_kernel(*refs, mode, mb, kb):
    a_ref, f_ref = refs[0], refs[1]
    o_ref = refs[-1]
    nb = a_ref.shape[1]
    for i in range(kb):
        f = f_ref[i].astype(F32)
        z = (jnp.dot(f[:, :nb], a_ref[0, :, i, :], preferred_element_type=F32)
             + jnp.dot(f[:, nb:], a_ref[1, :, i, :], preferred_element_type=F32))
        zr, zi = z[:mb], z[mb:]
        if mode == "filter":
            h_ref = refs[2]
            hr, hi = h_ref[0, i].astype(F32), h_ref[1, i].astype(F32)
            zr, zi = zr * hr - zi * hi, zr * hi + zi * hr
        elif mode == "final":
            v_ref, g_ref, b_ref = refs[2], refs[3], refs[4]
            zr = g_ref[0, i].astype(F32) * (zr + v_ref[0, i].astype(F32) * b_ref[...])
            zi = g_ref[1, i].astype(F32) * (zi + v_ref[1, i].astype(F32) * b_ref[...])
        o_ref[0, i] = zr.astype(o_ref.dtype)
        o_ref[1, i] = zi.astype(o_ref.dtype)


def fft_stage_b(a, fb, mode, extra=(), order=0, kb=16):
    _, nb, na, pc = a.shape
    mb = fb.shape[1] // 2
    c = GROUP_W
    kb = min(kb, na)
    blk = pl.BlockSpec((2, kb, mb, c), lambda k, p: (0, k, 0, p))
    specs = [pl.BlockSpec((2, nb, kb, c), lambda k, p: (0, 0, k, p)),
             pl.BlockSpec((kb, 2 * mb, 2 * nb), lambda k, p: (k, 0, 0))]
    args = [a, fb]
    if mode == "filter":
        (h,) = extra
        specs.append(pl.BlockSpec((2, kb, mb, c), lambda k, p: (0, k, 0, order)))
        args.append(h)
    elif mode == "final":
        v, gate, bias = extra
        specs += [blk, blk, pl.BlockSpec((1, c), lambda k, p: (0, 0))]
        args += [v, gate, bias]
    return _pc(functools.partial(_fft_b_kernel, mode=mode, mb=mb, kb=kb),
               jax.ShapeDtypeStruct((2, na, mb, pc), BF16), (na // kb, pc // c), specs, blk,
               name="fft_b_" + mode)(*args)


def _hy_filter_kernel(emb_ref, w1_ref, b1_ref, w2_ref, b2_ref, w3_ref, win_ref, o_ref, h_ref, *, na):
    hp = lax.Precision.HIGHEST

    @pl.when(pl.program_id(0) == 0)
    def _():
        rc = min(512, h_ref.shape[0])

        def chunk(i, carry):
            rows = pl.ds(pl.multiple_of(i * rc, rc), rc)
            h = jnp.sin(jnp.dot(emb_ref[rows, :], w1_ref[...], precision=hp, preferred_element_type=F32)
                        + b1_ref[...])
            h = jnp.sin(jnp.dot(h, w2_ref[...], precision=hp, preferred_element_type=F32) + b2_ref[...])
            row = lax.broadcasted_iota(jnp.int32, h.shape, 0) + i * rc
            causal = (row & (na - 1)) < na // 2
            h_ref[rows, :] = jnp.concatenate([jnp.where(causal, h, 0.0), jnp.where(causal, 0.0, h)],
                                             axis=1).astype(BF16)
            return carry

        lax.fori_loop(0, h_ref.shape[0] // rc, chunk, 0)

    taps = jnp.dot(h_ref[...], w3_ref[...], preferred_element_type=F32) * win_ref[...]
    o_ref[...] = (taps / jnp.sum(jnp.abs(taps), axis=0, keepdims=True)).astype(o_ref.dtype)


def hyena_filters(n, w1, b1, w2, b2, w3):
    na, nb = _fft_factors(2 * n)
    t = np.arange(n, dtype=np.float32)
    t_norm = t / max(n - 1, 1)
    bands = np.linspace(1e-4, HY_BANDS - 1, HY_BANDS, dtype=np.float32)
    ang = (2.0 * math.pi * t / n)[:, None] * bands[None, :]
    emb = np.concatenate([t_norm[:, None], np.cos(ang), -np.sin(ang)], axis=-1).astype(np.float32)
    emb = np.pad(emb, ((0, 0), (0, LANES - HY_EMB)))
    deltas = np.linspace(math.log(HY_TARGET) / HY_SLOW, math.log(HY_TARGET) / HY_FAST, GROUP_W, dtype=np.float32)
    win = (np.exp(-t_norm[:, None] * np.abs(deltas)[None, :]) + np.float32(HY_SHIFT)).astype(np.float32)
    back = np.concatenate([[0], np.arange(n - 2, -1, -1)])
    win_b = win[back]
    win_b[0] = 0.0
    order = (np.arange(na)[None, :] * nb + np.arange(nb)[:, None]).reshape(-1)
    emb2 = jnp.asarray(np.concatenate([emb, emb[back]], axis=0)[order])
    win2 = jnp.asarray(np.concatenate([win, win_b], axis=0)[order])
    w1p = jnp.pad(w1.astype(F32), ((0, LANES - HY_EMB), (0, 0)))
    w3r = w3.astype(F32).reshape(HY_HIDDEN, 2, 2, GROUP_W)
    w3c = jnp.concatenate([w3r[:, :, 0].reshape(HY_HIDDEN, 2 * GROUP_W),
                           w3r[:, :, 1].reshape(HY_HIDDEN, 2 * GROUP_W)], axis=0).astype(BF16)
    tc = LANES
    full = lambda a: pl.BlockSpec(a.shape, lambda j: (0, 0))
    b1r, b2r = b1.reshape(1, -1).astype(F32), b2.reshape(1, -1).astype(F32)
    w2f = w2.astype(F32)
    return _pc(functools.partial(_hy_filter_kernel, na=na),
               jax.ShapeDtypeStruct((2 * n, 2 * GROUP_W), BF16), (2 * GROUP_W // tc,),
               [full(emb2), full(w1p), full(b1r), full(w2f), full(b2r),
                pl.BlockSpec((2 * HY_HIDDEN, tc), lambda j: (0, j)),
                pl.BlockSpec((2 * n, tc), lambda j: (0, j % (GROUP_W // tc)))],
               pl.BlockSpec((2 * n, tc), lambda j: (0, j)),
               scratch=[pltpu.VMEM((2 * n, 2 * HY_HIDDEN), BF16)], name="hy_filter")(
        emb2, w1p, b1r, w2f, b2r, w3c, win2)


def hyena_spectrum(n, w1, b1, w2, b2, w3):
    na, nb = _fft_factors(2 * n)
    circ = hyena_filters(n, w1, b1, w2, b2, w3).reshape(nb, na, 2 * GROUP_W)
    fa, fb = _dft_tables(na, nb, na, nb, inverse=False)
    return fft_stage_b(fft_stage_a(circ, fa[:, :na], g=16), fb, "plain")


def _shortconv_kernel(p_ref, w_ref, b_ref, o_ref):
    p = p_ref[...].astype(F32)
    n = p.shape[0]
    row = lax.broadcasted_iota(jnp.int32, p.shape, 0)
    prev = jnp.where(row >= 1, pltpu.roll(p, 1, axis=0), 0.0)
    nxt = jnp.where(row < n - 1, pltpu.roll(p, n - 1, axis=0), 0.0)
    o_ref[...] = (prev * w_ref[0:1, :] + p * w_ref[1:2, :] + nxt * w_ref[2:3, :] + b_ref[...]).astype(o_ref.dtype)


def hyena_shortconv(proj, w, bias):
    b, n, _ = proj.shape
    return _pc(_shortconv_kernel, jax.ShapeDtypeStruct((3, 2, n, (b // 2) * GROUP_W), BF16), (b, 3),
               [pl.BlockSpec((None, n, GROUP_W), lambda bi, j: (bi, 0, PB_HY + j)),
                pl.BlockSpec((3, GROUP_W), lambda bi, j: (0, j)),
                pl.BlockSpec((1, GROUP_W), lambda bi, j: (0, j))],
               pl.BlockSpec((None, None, n, GROUP_W), lambda bi, j: (j, bi % 2, 0, bi // 2)),
               name="shortconv")(proj, w, bias)


def hyena_mixer(proj, spec, conv_w, conv_b, bias):
    b, n, _ = proj.shape
    na, nb = _fft_factors(2 * n)
    pc = (b // 2) * GROUP_W
    u = hyena_shortconv(proj, conv_w, conv_b).reshape(3, 2, na // 2, nb, pc).swapaxes(2, 3)
    fa, fb = _dft_tables(na, nb, na // 2, nb, inverse=False)
    ga, gb = _dft_tables(nb, na, nb, na // 2, inverse=True)
    z = u[0]
    for o in range(2):
        zf = fft_stage_b(fft_stage_a(z, fa, g=16), fb, "filter", (spec,), order=o)
        z = fft_stage_b(fft_stage_a(zf, ga), gb, "final", (z, u[1 + o], bias[o:o + 1]))
    return z.swapaxes(1, 2).reshape(2, n, pc)


def _rot_cols(w):
    return jnp.concatenate([-w[..., 16:32], w[..., 0:16], -w[..., 48:64], w[..., 32:48]], axis=-1)


def _rot_perm(g):
    return jnp.concatenate([g[16:32], g[0:16], g[48:64], g[32:48]])


def _prep_layer(w_in, w_uq, w_ukv, q_a_g, kv_a_g, q_g, k_g):
    d = w_in.shape[0]
    ckv, krope, rest = w_in[:, :256], w_in[:, 256:320], w_in[:, 320:]
    w_in_p = jnp.concatenate([rest, ckv, krope, _rot_cols(krope), jnp.zeros((d, 128), w_in.dtype)], axis=1)
    wq = w_uq.reshape(GROUP_W, MLA_HEADS, MLA_QK)
    wq = jnp.concatenate([wq, _rot_cols(wq[..., MLA_NOPE:])], axis=-1).reshape(GROUP_W, MLA_HEADS * 256)
    r2 = lambda a: a.reshape(1, -1).astype(F32)
    mla = dict(wkv=w_ukv.astype(BF16), wq=wq.astype(BF16), gkv=r2(kv_a_g), gqa=r2(q_a_g),
               gkn=r2(k_g[:MLA_NOPE]), gkt=r2(jnp.concatenate([k_g[MLA_NOPE:], _rot_perm(k_g[MLA_NOPE:])])),
               gqn=r2(q_g[:MLA_NOPE]), gqt=r2(jnp.concatenate([q_g[MLA_NOPE:], _rot_perm(q_g[MLA_NOPE:])])))
    return w_in_p.astype(BF16), mla


def _rope_table(n_rows):
    row = np.repeat(np.arange(n_rows, dtype=np.float32), GRID_W)
    col = np.tile(np.arange(GRID_W, dtype=np.float32), n_rows)
    n_freq = MLA_ROPE // 4
    inv_freq = (ROPE_BASE ** (-np.arange(n_freq, dtype=np.float32) / n_freq)).astype(np.float32)
    ang_r, ang_c = row[:, None] * inv_freq, col[:, None] * inv_freq
    ang = np.concatenate([ang_r, ang_r, ang_c, ang_c], axis=-1)
    return jnp.asarray(np.concatenate([np.cos(ang), np.sin(ang)], axis=-1), F32)


def kernel(x, c, ctx, c_ctx, w_mod, b_mod, norm1, norm2, w_in, w_out, mla_q_a_norm, mla_kv_a_norm, mla_w_uq, mla_w_ukv, mla_q_norm, mla_k_norm, s5_lam_re, s5_lam_im, s5_log_step, s5_b_re, s5_b_im, s5_c_re, s5_c_im, s5_d, s5_w_glu, s5_b_glu, hg_lower_bounds, hg_o_norm, hy_conv_w, hy_conv_b, hy_w1, hy_b1, hy_w2, hy_b2, hy_w3, hy_bias, mlp_w1, mlp_w2):
    bsz, seq, dm = x.shape
    n_ctx = ctx.shape[1]
    depth = w_mod.shape[0]
    assert bsz % 2 == 0 and bsz <= 8 and seq % GRID_W == 0 and S5_T * S5_N == GROUP_W
    r2 = lambda a: a.reshape(1, -1).astype(F32)

    cs_l = _rope_table(seq // GRID_W)
    cs_c = jnp.concatenate([jnp.ones((n_ctx, MLA_ROPE), F32), jnp.zeros((n_ctx, MLA_ROPE), F32)], axis=-1)
    sm = jax.nn.softmax(hg_lower_bounds.astype(F32), axis=1)
    lower = jnp.clip(jnp.cumsum(sm, axis=1) - sm[:, :1], 0.0, 1.0)
    c8 = jnp.concatenate([c.astype(F32), c_ctx.astype(F32)[None], jnp.zeros((7 - bsz, dm), F32)], axis=0)
    zero_state = jnp.zeros((bsz, HG_HEADS, HG_D, HG_D), F32)
    zero_x = jnp.zeros((S5_G, 8, 256), F32)

    xc = ctx
    for l in range(depth):
        last = l == depth - 1
        mod = modvec(c8, w_mod[l].astype(F32), r2(b_mod[l]))
        mod_l = [mod[:bsz, i * dm:(i + 1) * dm][:, None, :] for i in range(N_MOD)]
        mod_c = [mod[bsz:bsz + 1, i * dm:(i + 1) * dm][:, None, :] for i in range(N_MOD)]
        w_in_p, mla_p = _prep_layer(w_in[l], mla_w_uq[l], mla_w_ukv[l], mla_q_a_norm[l], mla_kv_a_norm[l],
                                    mla_q_norm[l], mla_k_norm[l])
        pl_ = modproj(x, mod_l[0], mod_l[1], r2(norm1[l]), w_in_p, BF16, tm=1024, tn=N_PROJ // 4)
        pc_ = modproj(xc, mod_c[0], mod_c[1], r2(norm1[l]), w_in_p, BF16, tm=1024, tn=N_PROJ // 4)

        k_c, v_c, q_c = mla_prep(pc_, cs_c, mla_p, with_q=not last)
        k_l, v_l, q_l = mla_prep(pl_, cs_l, mla_p, with_q=True)
        y_a = attention(q_l, [(k_l, v_l), (k_c, v_c)])

        tabs = s5_tables(s5_lam_re[l], s5_lam_im[l], s5_log_step[l], s5_b_re[l], s5_b_im[l], s5_c_re[l], s5_c_im[l])
        yc_s5, fin = s5_scan(pc_[..., PB_S5U * GROUP_W:(PB_S5U + 1) * GROUP_W], zero_x, tabs)
        yl_s5, _ = s5_scan(pl_[..., PB_S5U * GROUP_W:(PB_S5U + 1) * GROUP_W], fin, tabs)
        glu_w, glu_b = s5_w_glu[l].astype(BF16), r2(s5_b_glu[l])
        y_b = s5_out(yl_s5, pl_, r2(s5_d[l]), glu_w, glu_b)

        la_f, lb_f = r2(jnp.log(lower[0, l])), r2(jnp.log1p(-lower[0, l]))
        la_b, lb_b = r2(jnp.log(lower[1, l])), r2(jnp.log1p(-lower[1, l]))
        oc_f, s_f = hgrn_dir(pc_, la_f, lb_f, zero_state, reverse=False)
        oc_b, s_b = hgrn_dir(pc_, la_b, lb_b, zero_state, reverse=True)
        ol_f, _ = hgrn_dir(pl_, la_f, lb_f, s_f, reverse=False)
        ol_b, _ = hgrn_dir(pl_, la_b, lb_b, s_b, reverse=True)
        y_c = hg_out(ol_f, ol_b, pl_, r2(hg_o_norm[l]))

        hy_args = (hy_conv_w[l].astype(F32), r2(hy_conv_b[l]), hy_bias[l].astype(F32))
        spec_l = hyena_spectrum(seq, hy_w1[l], hy_b1[l], hy_w2[l], hy_b2[l], hy_w3[l])
        y_d = hyena_mixer(pl_, spec_l, *hy_args)

        w_out_b = w_out[l].astype(BF16)
        w1_b, w2_b = mlp_w1[l].astype(BF16), mlp_w2[l].astype(BF16)
        x = mix_out([y_a, y_b, y_c], y_d, w_out_b, x, mod_l[2])
        h1 = modproj(x, mod_l[3], mod_l[4], r2(norm2[l]), w1_b, BF16, act="relu2", tm=1024, tn=2048)
        x = mlp_out(h1, w2_b, x, mod_l[5])

        if not last:
            y_ac = attention(q_c, [(k_c, v_c)])
            y_bc = s5_out(yc_s5, pc_, r2(s5_d[l]), glu_w, glu_b)
            y_cc = hg_out(oc_f, oc_b, pc_, r2(hg_o_norm[l]))
            spec_c = hyena_spectrum(n_ctx, hy_w1[l], hy_b1[l], hy_w2[l], hy_b2[l], hy_w3[l])
            y_dc = hyena_mixer(pc_, spec_c, *hy_args)
            xc = mix_out([y_ac, y_bc, y_cc], y_dc, w_out_b, xc, mod_c[2])
            h1c = modproj(xc, mod_c[3], mod_c[4], r2(norm2[l]), w1_b, BF16, act="relu2", tm=1024)
            xc = mlp_out(h1c, w2_b, xc, mod_c[5])
    return x
```
